```python
import jax, jax.numpy as jnp
from jax import lax
import numpy as np

D_MODEL = 1024
BATCH = 8
SEQ = 2048
DEPTH = 1
DEC_BATCH = 128
DEC_SEQ = 1
PAST_LEN = 16384
PAGE_SIZE = 128

CONV_DIM = D_MODEL // 2
CONV_WIDTH = 31
HG_HEADS = 8
HG_DK = 128
HG_DV = 128
HG_DIM = HG_HEADS * HG_DK
HG_VDIM = HG_HEADS * HG_DV
HG_CHUNK = 64
N_GROUPS = 4
EXPERTS_PER_GROUP = 8
N_EXPERTS = N_GROUPS * EXPERTS_PER_GROUP
TOP_K = 2
EXPERT_FF = D_MODEL // 2
EPS = 1e-6
SPLIT_SIZES = (CONV_DIM, CONV_DIM, HG_DIM, HG_DIM, HG_VDIM, HG_VDIM, D_MODEL, D_MODEL)
IN_COLS = 2 * CONV_DIM + 2 * HG_DIM + 2 * HG_VDIM + 2 * D_MODEL

kernel_name = "hybrid_conv_hgrn2_hmoe_step"


def _rmsnorm(x, g):
    xf = x.astype(jnp.float32)
    y = xf * lax.rsqrt(jnp.mean(xf * xf, axis=-1, keepdims=True) + EPS)
    return (y * g.astype(jnp.float32)).astype(x.dtype)


def _layernorm(x, g, b):
    xf = x.astype(jnp.float32)
    mu = jnp.mean(xf, axis=-1, keepdims=True)
    var = jnp.mean(jnp.square(xf - mu), axis=-1, keepdims=True)
    y = (xf - mu) * lax.rsqrt(var + EPS)
    return (y * g.astype(jnp.float32) + b.astype(jnp.float32)).astype(x.dtype)


def _split_offsets():
    offs, acc = [], 0
    for s in SPLIT_SIZES[:-1]:
        acc += s
        offs.append(acc)
    return offs


def _hgrn2_chunked(q, k, v, logf, s0):
    B, T = q.shape[0], q.shape[1]
    C = min(HG_CHUNK, T)
    n = -(-T // C)
    pad = n * C - T
    if pad:
        pw = ((0, 0), (0, pad), (0, 0), (0, 0))
        q, k, v, logf = [jnp.pad(a, pw) for a in (q, k, v, logf)]

    def chunks(a):
        return a.reshape(B, n, C, a.shape[2], a.shape[3]).transpose(1, 0, 3, 2, 4)

    causal = jnp.tril(jnp.ones((C, C), dtype=bool))[:, :, None]

    def step(S, inp):
        qc, kc, vc, gc = inp
        b = jnp.cumsum(gc, axis=2)
        o_inter = jnp.einsum('bhtd,bhdv->bhtv', qc * jnp.exp(b), S)
        diff = b[:, :, :, None, :] - b[:, :, None, :, :]
        decay = jnp.exp(jnp.where(causal, diff, -jnp.inf))
        scores = jnp.einsum('bhtd,bhsd,bhtsd->bhts', qc, kc, decay)
        o = o_inter + jnp.einsum('bhts,bhsv->bhtv', scores, vc)
        b_last = b[:, :, -1]
        k_dec = kc * jnp.exp(b_last[:, :, None, :] - b)
        S = jnp.exp(b_last)[..., None] * S + jnp.einsum('bhsd,bhsv->bhdv', k_dec, vc)
        return S, o

    S, o = lax.scan(step, s0.astype(jnp.float32), (chunks(q), chunks(k), chunks(v), chunks(logf)))
    o = o.transpose(1, 0, 3, 2, 4).reshape(B, n * C, HG_HEADS, HG_DV)[:, :T]
    return o, S


def _mixer(h, conv_state, hgrn_state, lb, w_in, b_gates, conv_dw_w, conv_dw_b, conv_ln_g,
           conv_ln_b, w_conv_out, hgrn_norm_g, w_hgrn_out, w_out):
    B, T, _ = h.shape
    proj = h @ w_in
    cv, cg, q, f, v, og, ga, gb = jnp.split(proj, _split_offsets(), axis=-1)

    u = cv * jax.nn.sigmoid(cg)
    full = jnp.concatenate([conv_state.astype(u.dtype), u], axis=1)
    new_conv = full[:, T:]
    c = lax.conv_general_dilated(full, conv_dw_w[:, None, :].astype(full.dtype), (1,), 'VALID',
                                 dimension_numbers=('NWC', 'WIO', 'NWC'),
                                 feature_group_count=CONV_DIM)
    c = jax.nn.silu(_layernorm(c + conv_dw_b, conv_ln_g, conv_ln_b))
    y_a = c @ w_conv_out

    qf = jax.nn.silu(q.astype(jnp.float32)) * (HG_DK ** -0.5)
    fg = lb + (1.0 - lb) * jax.nn.sigmoid(f.astype(jnp.float32))
    logf = jnp.log(fg)
    kf = 1.0 - fg
    heads = lambda a, d: a.reshape(B, T, HG_HEADS, d)
    o, new_hgrn = _hgrn2_chunked(heads(qf, HG_DK), heads(kf, HG_DK),
                                 heads(v.astype(jnp.float32), HG_DV), heads(logf, HG_DK), hgrn_state)
    o = _rmsnorm(o, hgrn_norm_g.reshape(HG_HEADS, HG_DV)).reshape(B, T, HG_VDIM).astype(h.dtype)
    o = o * jax.nn.silu(og)
    y_b = o @ w_hgrn_out

    m = jax.nn.sigmoid(ga + b_gates[:D_MODEL]) * y_a + jax.nn.sigmoid(gb + b_gates[D_MODEL:]) * y_b
    return m @ w_out, new_conv, new_hgrn.astype(hgrn_state.dtype)


def _hier_moe(h, w_router_group, w_router_expert, w_expert_gate, w_expert_up, w_expert_down):
    B, T, D = h.shape
    ht = h.reshape(B * T, D)
    gp = jax.nn.softmax((ht @ w_router_group).astype(jnp.float32), axis=-1)
    gidx = jnp.argmax(gp, axis=-1)
    gval = jnp.max(gp, axis=-1)
    el = (ht @ w_router_expert).astype(jnp.float32).reshape(-1, N_GROUPS, EXPERTS_PER_GROUP)
    el = jnp.take_along_axis(el, gidx[:, None, None], axis=1)[:, 0]
    ep = jax.nn.softmax(el, axis=-1)
    tv, ti = lax.top_k(ep, TOP_K)
    tv = tv / jnp.sum(tv, axis=-1, keepdims=True)
    eid = gidx[:, None] * EXPERTS_PER_GROUP + ti
    combine = jnp.sum(jax.nn.one_hot(eid, N_EXPERTS, dtype=jnp.float32)
                      * (gval[:, None] * tv)[..., None], axis=1)
    out = jnp.zeros((B * T, D), jnp.float32)
    for e in range(N_EXPERTS):
        a = jax.nn.silu(ht @ w_expert_gate[e]) * (ht @ w_expert_up[e])
        out = out + combine[:, e:e + 1] * (a @ w_expert_down[e]).astype(jnp.float32)
    return out.astype(h.dtype).reshape(B, T, D)


def setup_inputs(seed: int = 0) -> dict:
    key = jax.random.key(seed)
    ks = jax.random.split(key, 24)
    nrm = lambda k, shape, s: jax.random.normal(k, shape, jnp.float32) * s
    gain = lambda k, shape: 1.0 + 0.01 * jax.random.normal(k, shape, jnp.float32)
    L = DEPTH
    return {
        'x_prompt': nrm(ks[0], (BATCH, SEQ, D_MODEL), 1.0),
        'x_sample': nrm(ks[1], (DEC_BATCH, DEC_SEQ, D_MODEL), 1.0),
        'state_conv': nrm(ks[2], (L, DEC_BATCH, CONV_WIDTH - 1, CONV_DIM), 0.5),
        'state_hgrn': nrm(ks[3], (L, DEC_BATCH, HG_HEADS, HG_DK, HG_DV), 0.5),
        'norm_mix_g': gain(ks[4], (L, D_MODEL)),
        'w_in': nrm(ks[5], (L, D_MODEL, IN_COLS), D_MODEL ** -0.5),
        'b_gates': nrm(ks[6], (L, 2 * D_MODEL), 0.02),
        'conv_dw_w': nrm(ks[7], (L, CONV_WIDTH, CONV_DIM), CONV_WIDTH ** -0.5),
        'conv_dw_b': nrm(ks[8], (L, CONV_DIM), 0.02),
        'conv_ln_g': gain(ks[9], (L, CONV_DIM)),
        'conv_ln_b': nrm(ks[10], (L, CONV_DIM), 0.02),
        'w_conv_out': nrm(ks[11], (L, CONV_DIM, D_MODEL), CONV_DIM ** -0.5),
        'hgrn_lb_param': nrm(ks[12], (L + 1, HG_DIM), 0.5),
        'hgrn_norm_g': gain(ks[13], (L, HG_VDIM)),
        'w_hgrn_out': nrm(ks[14], (L, HG_VDIM, D_MODEL), HG_VDIM ** -0.5),
        'w_out': nrm(ks[15], (L, D_MODEL, D_MODEL), D_MODEL ** -0.5),
        'norm_ffn_g': gain(ks[16], (L, D_MODEL)),
        'w_router_group': nrm(ks[17], (L, D_MODEL, N_GROUPS), D_MODEL ** -0.5),
        'w_router_expert': nrm(ks[18], (L, D_MODEL, N_EXPERTS), D_MODEL ** -0.5),
        'w_expert_gate': nrm(ks[19], (L, N_EXPERTS, D_MODEL, EXPERT_FF), D_MODEL ** -0.5),
        'w_expert_up': nrm(ks[20], (L, N_EXPERTS, D_MODEL, EXPERT_FF), D_MODEL ** -0.5),
        'w_expert_down': nrm(ks[21], (L, N_EXPERTS, EXPERT_FF, D_MODEL), EXPERT_FF ** -0.5),
        'norm_final_g': gain(ks[22], (D_MODEL,)),
    }


def _block(x, conv_st, hgrn_st, lb, norm_mix_g, w_in, b_gates, conv_dw_w, conv_dw_b, conv_ln_g,
           conv_ln_b, w_conv_out, hgrn_norm_g, w_hgrn_out, w_out, norm_ffn_g, w_router_group,
           w_router_expert, w_expert_gate, w_expert_up, w_expert_down):
    m, new_conv, new_hgrn = _mixer(_rmsnorm(x, norm_mix_g), conv_st, hgrn_st, lb, w_in, b_gates,
                                   conv_dw_w, conv_dw_b, conv_ln_g, conv_ln_b, w_conv_out,
                                   hgrn_norm_g, w_hgrn_out, w_out)
    x = x + m
    x = x + _hier_moe(_rmsnorm(x, norm_ffn_g), w_router_group, w_router_expert,
                      w_expert_gate, w_expert_up, w_expert_down)
    return x, new_conv, new_hgrn


def reference(x_prompt, x_sample, state_conv, state_hgrn, norm_mix_g, w_in, b_gates, conv_dw_w,
              conv_dw_b, conv_ln_g, conv_ln_b, w_conv_out, hgrn_lb_param, hgrn_norm_g, w_hgrn_out,
              w_out, norm_ffn_g, w_router_group, w_router_expert, w_expert_gate, w_expert_up,
              w_expert_down, norm_final_g):
    lb_all = jnp.cumsum(jax.nn.softmax(hgrn_lb_param.astype(jnp.float32), axis=0), axis=0)
    hp, hs = x_prompt, x_sample
    pc, ph, sc, sh = [], [], [], []
    for l in range(DEPTH):
        w = (norm_mix_g[l], w_in[l], b_gates[l], conv_dw_w[l], conv_dw_b[l], conv_ln_g[l],
             conv_ln_b[l], w_conv_out[l], hgrn_norm_g[l], w_hgrn_out[l], w_out[l], norm_ffn_g[l],
             w_router_group[l], w_router_expert[l], w_expert_gate[l], w_expert_up[l],
             w_expert_down[l])
        Bp = hp.shape[0]
        zc = jnp.zeros((Bp, CONV_WIDTH - 1, CONV_DIM), hp.dtype)
        zh = jnp.zeros((Bp, HG_HEADS, HG_DK, HG_DV), hp.dtype)
        hp, c_p, s_p = _block(hp, zc, zh, lb_all[l], *w)
        hs, c_s, s_s = _block(hs, state_conv[l], state_hgrn[l], lb_all[l], *w)
        pc.append(c_p); ph.append(s_p); sc.append(c_s); sh.append(s_s)
    y_prompt = _rmsnorm(hp, norm_final_g)
    y_sample = _rmsnorm(hs, norm_final_g)
    new_conv_prompt = jnp.stack(pc, axis=0)
    new_hgrn_prompt = jnp.stack(ph, axis=0)
    new_conv_sample = jnp.stack(sc, axis=0)
    new_hgrn_sample = jnp.stack(sh, axis=0)
    return (y_prompt, y_sample, new_conv_prompt, new_hgrn_prompt, new_conv_sample, new_hgrn_sample)
```

```python
import functools

import jax
import jax.numpy as jnp
from jax import lax
from jax.experimental import pallas as pl
from jax.experimental.pallas import tpu as pltpu

D_MODEL = 1024
CONV_DIM = D_MODEL // 2
CONV_WIDTH = 31
HG_HEADS = 8
HG_DK = 128
HG_DV = 128
HG_DIM = HG_HEADS * HG_DK
N_GROUPS = 4
EXPERTS_PER_GROUP = 8
N_EXPERTS = N_GROUPS * EXPERTS_PER_GROUP
EXPERT_FF = D_MODEL // 2
EPS = 1e-6
IN_COLS = 2 * CONV_DIM + 4 * HG_DIM + 2 * D_MODEL

OFF_CV, OFF_CG = 0, CONV_DIM
OFF_Q = 2 * CONV_DIM
OFF_F = OFF_Q + HG_DIM
OFF_V = OFF_F + HG_DIM
OFF_OG = OFF_V + HG_DIM
OFF_GA = OFF_OG + HG_DIM
OFF_GB = OFF_GA + D_MODEL

SUBLANES = 8
LANES = 128
HIST_ROWS = 32
HIST_PAD = HIST_ROWS - (CONV_WIDTH - 1)
CHUNK = 64
NBLK = CHUNK // SUBLANES
ROUTE_LANES = 128
VMEM_LIMIT = 56 * 1024 * 1024

BF = jnp.bfloat16
F32 = jnp.float32


def _dot(a, b):
    return jnp.dot(a, b, preferred_element_type=F32)


def _dot_nt(a, b):
    return lax.dot_general(a, b, (((1,), (1,)), ((), ())), preferred_element_type=F32)


def _dot_tn(a, b, precision=None):
    return lax.dot_general(a, b, (((0,), (0,)), ((), ())), preferred_element_type=F32,
                           precision=precision)


def _sigmoid(x):
    return 1.0 / (1.0 + jnp.exp(-x))


def _silu(x):
    return x * _sigmoid(x)


def _rms(xf, g):
    return xf * lax.rsqrt(jnp.mean(xf * xf, axis=-1, keepdims=True) + EPS) * g


def _lower_bound(lb_param):
    m = jnp.max(lb_param, axis=0, keepdims=True)
    e = jnp.exp(lb_param - m)
    return e[0:1] / jnp.sum(e, axis=0, keepdims=True)


def _conv_post(c, conv_b, ln_g, ln_b):
    c = c + conv_b
    mu = jnp.mean(c, axis=-1, keepdims=True)
    d = c - mu
    var = jnp.mean(d * d, axis=-1, keepdims=True)
    return _silu(d * lax.rsqrt(var + EPS) * ln_g + ln_b)


def _head_norm(o, g):
    return o * lax.rsqrt(jnp.mean(o * o, axis=-1, keepdims=True) + EPS) * g


def _route(logits):
    col = lax.broadcasted_iota(jnp.int32, logits.shape, 1)
    big = jnp.int32(1 << 20)
    neg = jnp.float32(-jnp.inf)
    gmask = col < N_GROUPS
    lg = jnp.where(gmask, logits, neg)
    gmax = jnp.max(lg, axis=-1, keepdims=True)
    gsum = jnp.sum(jnp.where(gmask, jnp.exp(lg - gmax), 0.0), axis=-1, keepdims=True)
    gval = 1.0 / gsum
    gidx = jnp.min(jnp.where(lg == gmax, col, big), axis=-1, keepdims=True)
    lo = N_GROUPS + EXPERTS_PER_GROUP * gidx
    emask = (col >= lo) & (col < lo + EXPERTS_PER_GROUP)
    el = jnp.where(emask, logits, neg)
    m1 = jnp.max(el, axis=-1, keepdims=True)
    i1 = jnp.min(jnp.where(el == m1, col, big), axis=-1, keepdims=True)
    el2 = jnp.where(col == i1, neg, el)
    m2 = jnp.max(el2, axis=-1, keepdims=True)
    i2 = jnp.min(jnp.where(el2 == m2, col, big), axis=-1, keepdims=True)
    r = jnp.exp(m2 - m1)
    w1 = gval / (1.0 + r)
    w2 = gval * r / (1.0 + r)
    e1 = (i1 - N_GROUPS).astype(F32)
    e2 = (i2 - N_GROUPS).astype(F32)
    return jnp.where(col == 0, e1, jnp.where(col == 1, e2, jnp.where(col == 2, w1, jnp.where(col == 3, w2, 0.0))))


def _bcast_blocks(g):
    return jnp.concatenate(
        [jnp.broadcast_to(g[j:j + 1, :], (SUBLANES, g.shape[1])) for j in range(g.shape[0])], axis=0)


def _hgrn_chunk(q, k, v, lf, st, blk_ones, bl_ref, k_ref):
    row = lax.broadcasted_iota(jnp.int32, (CHUNK, LANES), 0)
    rin = row % SUBLANES
    bl = lf
    for s in (1, 2, 4):
        bl = bl + jnp.where(rin >= s, pltpu.roll(bl, s, axis=0), 0.0)
    bl_ref[...] = bl
    k_ref[...] = k
    tot = bl_ref[pl.ds(SUBLANES - 1, NBLK, stride=SUBLANES), :]
    brow = lax.broadcasted_iota(jnp.int32, (NBLK, LANES), 0)
    rb = tot
    for s in (1, 2, 4):
        rb = rb + jnp.where(brow >= s, pltpu.roll(rb, s, axis=0), 0.0)
    rb_prev = rb - tot
    total = rb[NBLK - 1:NBLK, :]

    p_in = jnp.exp(bl)
    k_out = jnp.exp(_bcast_blocks(tot) - bl)
    qp = q * p_in
    kp = k * k_out

    qe = qp * _bcast_blocks(jnp.exp(rb_prev))
    kdec = kp * _bcast_blocks(jnp.exp(total - rb))
    st_b = st.astype(BF)
    o = _dot_nt(qe.astype(BF), st_b)
    v_b = v.astype(BF)
    st_new = st * jnp.exp(total) + _dot_tn(v_b, kdec.astype(BF))

    ti = lax.broadcasted_iota(jnp.int32, (CHUNK, CHUNK), 0)
    si = lax.broadcasted_iota(jnp.int32, (CHUNK, CHUNK), 1)
    scores = jnp.zeros((CHUNK, CHUNK), F32)
    cb = NBLK // 2
    while cb >= 1:
        edges = [(j // (2 * cb)) * (2 * cb) + cb - 1 for j in range(NBLK)]
        rb_edge = jnp.concatenate([rb[e:e + 1, :] for e in edges], axis=0)
        late = (brow // cb) % 2 == 1
        gq = jnp.where(late, jnp.exp(jnp.minimum(rb_prev - rb_edge, 0.0)), 0.0)
        gk = jnp.where(late, 0.0, jnp.exp(jnp.minimum(rb_edge - rb, 0.0)))
        qc = (qp * _bcast_blocks(gq)).astype(BF)
        kc = (kp * _bcast_blocks(gk)).astype(BF)
        span = SUBLANES * 2 * cb
        scores = scores + jnp.where(ti // span == si // span, _dot_nt(qc, kc), 0.0)
        cb //= 2

    parts = []
    for c in range(SUBLANES):
        blc = _bcast_rows_of_block(bl_ref, c)
        kc = _bcast_rows_of_block(k_ref, c)
        parts.append((q * kc * jnp.exp(jnp.minimum(bl - blc, 0.0))).astype(BF))
    diag = _dot(jnp.concatenate(parts, axis=1), blk_ones)
    scores = scores + jnp.where((ti // SUBLANES == si // SUBLANES) & (si <= ti), diag, 0.0)

    o = o + _dot(scores.astype(BF), v_b)
    return o, st_new


def _bcast_rows_of_block(ref, c):
    return jnp.concatenate(
        [jnp.broadcast_to(ref[pl.ds(j * SUBLANES + c, 1), :], (SUBLANES, ref.shape[1])) for j in range(NBLK)], axis=0)


def _mixer_tail(x, y_a, y_b, ga, gb, b_gates, w_out, g_ffn, w_router):
    m = _sigmoid(ga + b_gates[:, :D_MODEL]) * y_a + _sigmoid(gb + b_gates[:, D_MODEL:]) * y_b
    x1 = x + _dot(m.astype(BF), w_out)
    h2 = _rms(x1, g_ffn)
    route = _route(_dot(h2.astype(BF), w_router))
    return x1, h2, route


def _prompt_mixer_kernel(x_ref, gmix_ref, win_ref, bg_ref, cw_ref, cb_ref, lng_ref, lnb_ref, wco_ref,
                         lbp_ref, hng_ref, who_ref, wout_ref, gffn_ref, wr_ref, ones_ref,
                         x1_ref, h2_ref, route_ref, nconv_ref, nhgrn_ref,
                         hist_ref, st_ref, q_s, k_s, v_s, lf_s, o_s, bl_s, kc_s, *, tb):
    t = pl.program_id(1)

    @pl.when(t == 0)
    def _():
        hist_ref[pl.ds(0, HIST_ROWS), :] = jnp.zeros((HIST_ROWS, CONV_DIM), F32)
        st_ref[...] = jnp.zeros_like(st_ref)

    x = x_ref[...]
    h = _rms(x, gmix_ref[...]).astype(BF)

    cv = _dot(h, win_ref[:, OFF_CV:OFF_CV + CONV_DIM])
    cg = _dot(h, win_ref[:, OFF_CG:OFF_CG + CONV_DIM])
    hist_ref[pl.ds(HIST_ROWS, tb), :] = cv * _sigmoid(cg)
    acc = jnp.zeros((tb, CONV_DIM), F32)
    for r in range(SUBLANES):
        taps = range(r, CONV_WIDTH, SUBLANES)
        g = hist_ref[pl.ds(HIST_PAD + r, tb + SUBLANES * (len(taps) - 1)), :]
        for a, j in enumerate(taps):
            acc = acc + g[SUBLANES * a:SUBLANES * a + tb, :] * cw_ref[pl.ds(j, 1), :]
    tail = hist_ref[pl.ds(tb, HIST_ROWS), :]
    hist_ref[pl.ds(0, HIST_ROWS), :] = tail
    nconv_ref[0] = tail
    c = _conv_post(acc, cb_ref[...], lng_ref[...], lnb_ref[...])
    y_a = _dot(c.astype(BF), wco_ref[...])

    lb = _lower_bound(lbp_ref[...])
    qv = _dot(h, win_ref[:, OFF_Q:OFF_Q + HG_DIM])
    q_s[...] = _silu(qv) * (HG_DK ** -0.5)
    fv = _dot(h, win_ref[:, OFF_F:OFF_F + HG_DIM])
    fg = lb + (1.0 - lb) * _sigmoid(fv)
    lf_s[...] = jnp.log(fg)
    k_s[...] = 1.0 - fg
    v_s[...] = _dot(h, win_ref[:, OFF_V:OFF_V + HG_DIM])
    blk_ones = ones_ref[...]

    def chunk_body(ci, carry):
        r0 = pl.multiple_of(ci * CHUNK, CHUNK)
        for hd in range(HG_HEADS):
            cs = slice(hd * HG_DK, (hd + 1) * HG_DK)
            o, st_new = _hgrn_chunk(q_s[pl.ds(r0, CHUNK), cs], k_s[pl.ds(r0, CHUNK), cs],
                                    v_s[pl.ds(r0, CHUNK), cs], lf_s[pl.ds(r0, CHUNK), cs],
                                    st_ref[hd], blk_ones, bl_s, kc_s)
            st_ref[hd] = st_new
            o_s[pl.ds(r0, CHUNK), cs] = _head_norm(o, hng_ref[:, cs])
        return carry

    lax.fori_loop(0, tb // CHUNK, chunk_body, 0)
    for hd in range(HG_HEADS):
        nhgrn_ref[0, hd] = st_ref[hd].T

    og = _dot(h, win_ref[:, OFF_OG:OFF_OG + HG_DIM])
    y_b = _dot((o_s[...] * _silu(og)).astype(BF), who_ref[...])

    ga = _dot(h, win_ref[:, OFF_GA:OFF_GA + D_MODEL])
    gb = _dot(h, win_ref[:, OFF_GB:OFF_GB + D_MODEL])
    x1, h2, route = _mixer_tail(x, y_a, y_b, ga, gb, bg_ref[...], wout_ref[...], gffn_ref[...], wr_ref[...])
    x1_ref[...] = x1
    h2_ref[...] = h2
    route_ref[...] = route


def _const_spec(shape):
    nd = len(shape)
    return pl.BlockSpec(shape, lambda *_: (0,) * nd, pipeline_mode=pl.Buffered(1))


def _block_ones():
    r = jnp.arange(SUBLANES * LANES) // LANES
    s = jnp.arange(CHUNK) % SUBLANES
    return (r[:, None] == s[None, :]).astype(BF)


def _prompt_mixer(x2, batch, seq, tb, wts):
    nt = seq // tb
    n = batch * seq
    row_spec = lambda w: pl.BlockSpec((tb, w), lambda b, t: (b * nt + t, 0))
    consts = [wts['gmix'], wts['w_in'], wts['b_gates'], wts['conv_w'], wts['conv_b'], wts['ln_g'], wts['ln_b'],
              wts['w_conv_out'], wts['lb_param'], wts['hn_g'], wts['w_hgrn_out'], wts['w_out'], wts['gffn'],
              wts['w_router'], _block_ones()]
    return pl.pallas_call(
        functools.partial(_prompt_mixer_kernel, tb=tb),
        grid=(batch, nt),
        in_specs=[row_spec(D_MODEL)] + [_const_spec(c.shape) for c in consts],
        out_specs=[row_spec(D_MODEL), row_spec(D_MODEL), row_spec(ROUTE_LANES),
                   pl.BlockSpec((1, HIST_ROWS, CONV_DIM), lambda b, t: (b, 0, 0)),
                   pl.BlockSpec((1, HG_HEADS, HG_DK, HG_DV), lambda b, t: (b, 0, 0, 0))],
        out_shape=[jax.ShapeDtypeStruct((n, D_MODEL), F32), jax.ShapeDtypeStruct((n, D_MODEL), F32),
                   jax.ShapeDtypeStruct((n, ROUTE_LANES), F32),
                   jax.ShapeDtypeStruct((batch, HIST_ROWS, CONV_DIM), F32),
                   jax.ShapeDtypeStruct((batch, HG_HEADS, HG_DK, HG_DV), F32)],
        scratch_shapes=[pltpu.VMEM((HIST_ROWS + tb, CONV_DIM), F32),
                        pltpu.VMEM((HG_HEADS, HG_DV, HG_DK), F32)]
                       + [pltpu.VMEM((tb, HG_DIM), F32)] * 5
                       + [pltpu.VMEM((CHUNK, LANES), F32)] * 2,
        compiler_params=pltpu.CompilerParams(dimension_semantics=("arbitrary", "arbitrary"),
                                             vmem_limit_bytes=VMEM_LIMIT),
        name="prompt_mixer",
    )(x2, *consts)


def _sample_proj_kernel(x_ref, gmix_ref, win_ref, proj_ref):
    h = _rms(x_ref[...], gmix_ref[...]).astype(BF)
    proj_ref[...] = _dot(h, win_ref[...])


def _sample_proj(xs, wts):
    n = xs.shape[0]
    return pl.pallas_call(
        _sample_proj_kernel,
        grid=(1,),
        in_specs=[_const_spec(xs.shape), _const_spec(wts['gmix'].shape), _const_spec(wts['w_in'].shape)],
        out_specs=pl.BlockSpec((n, IN_COLS), lambda i: (0, 0)),
        out_shape=jax.ShapeDtypeStruct((n, IN_COLS), F32),
        compiler_params=pltpu.CompilerParams(vmem_limit_bytes=VMEM_LIMIT),
        name="sample_proj",
    )(xs, wts['gmix'], wts['w_in'])


def _sample_state_kernel(q_ref, f_ref, v_ref, lbp_ref, hng_ref, s_ref, snew_ref, o_ref, *, bs):
    lb = _lower_bound(lbp_ref[...])
    qf = _silu(q_ref[...]) * (HG_DK ** -0.5)
    fg = lb + (1.0 - lb) * _sigmoid(f_ref[...])
    kf = 1.0 - fg
    v = v_ref[...]
    rows = lax.broadcasted_iota(jnp.int32, (bs, LANES), 0)
    hi = lax.Precision.HIGHEST
    for hd in range(HG_HEADS):
        cs = slice(hd * HG_DK, (hd + 1) * HG_DK)
        o_h = jnp.zeros((bs, HG_DV), F32)
        for r in range(bs):
            sel = rows == r
            f_col = _dot_tn(fg[:, cs], jnp.where(sel, 1.0, 0.0), precision=hi)
            kv = _dot_tn(kf[:, cs], jnp.where(sel, v[:, cs], 0.0), precision=hi)
            s_new = f_col * s_ref[r, hd] + kv
            snew_ref[r, hd] = s_new
            o_h = o_h + jnp.dot(jnp.where(sel, qf[:, cs], 0.0), s_new, preferred_element_type=F32, precision=hi)
        o_ref[:, cs] = _head_norm(o_h, hng_ref[:, cs])


def _sample_state(proj, state, wts, bs):
    n = proj.shape[0]
    col_spec = lambda off: pl.BlockSpec((bs, HG_DIM), lambda i, off=off: (i, off // HG_DIM))
    st_spec = pl.BlockSpec((bs, HG_HEADS, HG_DK, HG_DV), lambda i: (i, 0, 0, 0))
    return pl.pallas_call(
        functools.partial(_sample_state_kernel, bs=bs),
        grid=(n // bs,),
        in_specs=[col_spec(OFF_Q), col_spec(OFF_F), col_spec(OFF_V),
                  _const_spec(wts['lb_param'].shape), _const_spec(wts['hn_g'].shape), st_spec],
        out_specs=[st_spec, pl.BlockSpec((bs, HG_DIM), lambda i: (i, 0))],
        out_shape=[jax.ShapeDtypeStruct(state.shape, F32), jax.ShapeDtypeStruct((n, HG_DIM), F32)],
        compiler_params=pltpu.CompilerParams(dimension_semantics=("arbitrary",), vmem_limit_bytes=VMEM_LIMIT),
        name="sample_state",
    )(proj, proj, proj, wts['lb_param'], wts['hn_g'], state)


def _sample_tail_kernel(x_ref, proj_ref, o_ref, cst_ref, bg_ref, cw_ref, cb_ref, lng_ref, lnb_ref, wco_ref,
                        who_ref, wout_ref, gffn_ref, wr_ref, x1_ref, h2_ref, route_ref, nconv_ref):
    keep = (CONV_WIDTH - 2) * CONV_DIM
    u = proj_ref[:, OFF_CV:OFF_CV + CONV_DIM] * _sigmoid(proj_ref[:, OFF_CG:OFF_CG + CONV_DIM])
    acc = u * cw_ref[pl.ds(CONV_WIDTH - 1, 1), :]
    for j in range(CONV_WIDTH - 1):
        acc = acc + cst_ref[:, j * CONV_DIM:(j + 1) * CONV_DIM] * cw_ref[pl.ds(j, 1), :]
    nconv_ref[:, :keep] = cst_ref[:, CONV_DIM:]
    nconv_ref[:, keep:] = u
    c = _conv_post(acc, cb_ref[...], lng_ref[...], lnb_ref[...])
    y_a = _dot(c.astype(BF), wco_ref[...])
    og = proj_ref[:, OFF_OG:OFF_OG + HG_DIM]
    y_b = _dot((o_ref[...] * _silu(og)).astype(BF), who_ref[...])
    x1, h2, route = _mixer_tail(x_ref[...], y_a, y_b, proj_ref[:, OFF_GA:OFF_GA + D_MODEL],
                                proj_ref[:, OFF_GB:OFF_GB + D_MODEL], bg_ref[...], wout_ref[...],
                                gffn_ref[...], wr_ref[...])
    x1_ref[...] = x1
    h2_ref[...] = h2
    route_ref[...] = route


def _sample_tail(xs, proj, o, conv_state2, wts):
    n = xs.shape[0]
    ins = [xs, proj, o, conv_state2, wts['b_gates'], wts['conv_w'], wts['conv_b'], wts['ln_g'], wts['ln_b'],
           wts['w_conv_out'], wts['w_hgrn_out'], wts['w_out'], wts['gffn'], wts['w_router']]
    full = lambda shape: pl.BlockSpec(shape, lambda i: (0,) * len(shape))
    return pl.pallas_call(
        _sample_tail_kernel,
        grid=(1,),
        in_specs=[_const_spec(a.shape) for a in ins],
        out_specs=[full((n, D_MODEL)), full((n, D_MODEL)), full((n, ROUTE_LANES)), full(conv_state2.shape)],
        out_shape=[jax.ShapeDtypeStruct((n, D_MODEL), F32), jax.ShapeDtypeStruct((n, D_MODEL), F32),
                   jax.ShapeDtypeStruct((n, ROUTE_LANES), F32), jax.ShapeDtypeStruct(conv_state2.shape, F32)],
        compiler_params=pltpu.CompilerParams(vmem_limit_bytes=VMEM_LIMIT),
        name="sample_tail",
    )(*ins)


def _expert_kernel(te_ref, src_ref, nt_ref, h2_ref, wg_ref, wu_ref, wd_ref, y_ref,
                   xbuf, wg_b, wu_b, wd_b, sem, *, tm):
    i = pl.program_id(0)

    @pl.when(i < nt_ref[0])
    def _():
        def row_copy(r):
            return pltpu.make_async_copy(h2_ref.at[pl.ds(src_ref[i * tm + r], 1), :],
                                         xbuf.at[pl.ds(r, 1), :], sem)

        def issue(r, c):
            row_copy(r).start()
            return c

        lax.fori_loop(0, tm, issue, 0)

        changed = jnp.logical_or(i == 0, te_ref[i] != te_ref[jnp.maximum(i - 1, 0)])

        @pl.when(changed)
        def _():
            wg_b[...] = wg_ref[0].astype(BF)
            wu_b[...] = wu_ref[0].astype(BF)
            wd_b[...] = wd_ref[0].astype(BF)

        def drain(r, c):
            row_copy(r).wait()
            return c

        lax.fori_loop(0, tm, drain, 0)
        xb = xbuf[...].astype(BF)
        a = _silu(_dot(xb, wg_b[...])) * _dot(xb, wu_b[...])
        y_ref[...] = _dot(a.astype(BF), wd_b[...])

    @pl.when(i >= nt_ref[0])
    def _():
        y_ref[...] = jnp.zeros_like(y_ref)


def _experts(h2_all, tile_expert, src_tok, n_tiles, wg, wu, wd, tm, max_tiles):
    grid_spec = pltpu.PrefetchScalarGridSpec(
        num_scalar_prefetch=3,
        grid=(max_tiles,),
        in_specs=[pl.BlockSpec(memory_space=pl.ANY),
                  pl.BlockSpec((1, D_MODEL, EXPERT_FF), lambda i, te, src, nt: (te[i], 0, 0)),
                  pl.BlockSpec((1, D_MODEL, EXPERT_FF), lambda i, te, src, nt: (te[i], 0, 0)),
                  pl.BlockSpec((1, EXPERT_FF, D_MODEL), lambda i, te, src, nt: (te[i], 0, 0))],
        out_specs=pl.BlockSpec((tm, D_MODEL), lambda i, te, src, nt: (i, 0)),
        scratch_shapes=[pltpu.VMEM((tm, D_MODEL), F32),
                        pltpu.VMEM((D_MODEL, EXPERT_FF), BF), pltpu.VMEM((D_MODEL, EXPERT_FF), BF),
                        pltpu.VMEM((EXPERT_FF, D_MODEL), BF),
                        pltpu.SemaphoreType.DMA(())])
    return pl.pallas_call(
        functools.partial(_expert_kernel, tm=tm),
        grid_spec=grid_spec,
        out_shape=jax.ShapeDtypeStruct((max_tiles * tm, D_MODEL), F32),
        compiler_params=pltpu.CompilerParams(dimension_semantics=("arbitrary",), vmem_limit_bytes=VMEM_LIMIT),
        name="experts",
    )(tile_expert, src_tok, n_tiles, h2_all, wg, wu, wd)


def _combine_kernel(pos_ref, x1_ref, route_ref, gfin_ref, ys_ref, y_ref, buf0, buf1, sem, *, tb):
    i = pl.program_id(0)

    def copies(r):
        base = 2 * (i * tb + r)
        return (pltpu.make_async_copy(ys_ref.at[pl.ds(pos_ref[base], 1), :], buf0.at[pl.ds(r, 1), :], sem),
                pltpu.make_async_copy(ys_ref.at[pl.ds(pos_ref[base + 1], 1), :], buf1.at[pl.ds(r, 1), :], sem))

    def issue(r, c):
        c0, c1 = copies(r)
        c0.start()
        c1.start()
        return c

    def drain(r, c):
        c0, c1 = copies(r)
        c0.wait()
        c1.wait()
        return c

    lax.fori_loop(0, tb, issue, 0)
    lax.fori_loop(0, tb, drain, 0)
    route = route_ref[...]
    out = x1_ref[...] + (route[:, 2:3] * buf0[...] + route[:, 3:4] * buf1[...])
    y_ref[...] = _rms(out, gfin_ref[...])


def _combine(pos, x1, route, gfin, ys, tb):
    n = x1.shape[0]
    grid_spec = pltpu.PrefetchScalarGridSpec(
        num_scalar_prefetch=1,
        grid=(n // tb,),
        in_specs=[pl.BlockSpec((tb, D_MODEL), lambda i, p: (i, 0)),
                  pl.BlockSpec((tb, ROUTE_LANES), lambda i, p: (i, 0)),
                  pl.BlockSpec((1, D_MODEL), lambda i, p: (0, 0)),
                  pl.BlockSpec(memory_space=pl.ANY)],
        out_specs=pl.BlockSpec((tb, D_MODEL), lambda i, p: (i, 0)),
        scratch_shapes=[pltpu.VMEM((tb, D_MODEL), F32), pltpu.VMEM((tb, D_MODEL), F32),
                        pltpu.SemaphoreType.DMA(())])
    return pl.pallas_call(
        functools.partial(_combine_kernel, tb=tb),
        grid_spec=grid_spec,
        out_shape=jax.ShapeDtypeStruct((n, D_MODEL), F32),
        compiler_params=pltpu.CompilerParams(dimension_semantics=("arbitrary",), vmem_limit_bytes=VMEM_LIMIT),
        name="combine",
    )(pos, x1, route, gfin, ys)


def _dispatch_plan(eid, tm, max_tiles):
    flat = eid.reshape(-1)
    onehot = (flat[:, None] == jnp.arange(N_EXPERTS, dtype=jnp.int32)[None, :]).astype(jnp.int32)
    csum = jnp.cumsum(onehot, axis=0)
    counts = csum[-1]
    rank = jnp.take_along_axis(csum, flat[:, None], axis=1)[:, 0] - 1
    tiles_per = (counts + tm - 1) // tm
    tile_end = jnp.cumsum(tiles_per)
    starts = (tile_end - tiles_per) * tm
    pos = starts[flat] + rank
    n_tiles = tile_end[-1:]
    tile_expert = jnp.minimum(jnp.searchsorted(tile_end, jnp.arange(max_tiles, dtype=jnp.int32), side='right'),
                              N_EXPERTS - 1).astype(jnp.int32)
    src_tok = jnp.zeros((max_tiles * tm,), jnp.int32).at[pos].set(jnp.arange(flat.shape[0], dtype=jnp.int32) // 2)
    return pos.astype(jnp.int32), tile_expert, src_tok, n_tiles.astype(jnp.int32)


def kernel(x_prompt, x_sample, state_conv, state_hgrn, norm_mix_g, w_in, b_gates, conv_dw_w, conv_dw_b,
           conv_ln_g, conv_ln_b, w_conv_out, hgrn_lb_param, hgrn_norm_g, w_hgrn_out, w_out, norm_ffn_g,
           w_router_group, w_router_expert, w_expert_gate, w_expert_up, w_expert_down, norm_final_g):
    batch, seq, _ = x_prompt.shape
    dec_batch = x_sample.shape[0]
    assert x_sample.shape[1] == 1 and w_in.shape[0] == 1
    tb = min(256, seq)
    tm = 256
    bs = min(8, dec_batch)
    assert seq % tb == 0 and tb % CHUNK == 0 and dec_batch % bs == 0

    w_router = jnp.concatenate(
        [w_router_group[0], w_router_expert[0],
         jnp.zeros((D_MODEL, ROUTE_LANES - N_GROUPS - N_EXPERTS), F32)], axis=1).astype(BF)
    wts = dict(gmix=norm_mix_g, w_in=w_in[0].astype(BF), b_gates=b_gates, conv_w=conv_dw_w[0], conv_b=conv_dw_b,
               ln_g=conv_ln_g, ln_b=conv_ln_b, w_conv_out=w_conv_out[0].astype(BF), lb_param=hgrn_lb_param,
               hn_g=hgrn_norm_g, w_hgrn_out=w_hgrn_out[0].astype(BF), w_out=w_out[0].astype(BF),
               gffn=norm_ffn_g, w_router=w_router)

    n_p = batch * seq
    x1_p, h2_p, route_p, nconv_p, nhgrn_p = _prompt_mixer(x_prompt.reshape(n_p, D_MODEL), batch, seq, tb, wts)

    xs = x_sample.reshape(dec_batch, D_MODEL)
    proj_s = _sample_proj(xs, wts)
    nhgrn_s, o_s = _sample_state(proj_s, state_hgrn[0], wts, bs)
    conv2 = state_conv[0].reshape(dec_batch, (CONV_WIDTH - 1) * CONV_DIM)
    x1_s, h2_s, route_s, nconv_s = _sample_tail(xs, proj_s, o_s, conv2, wts)

    h2_all = jnp.concatenate([h2_p, h2_s], axis=0)
    route_all = jnp.concatenate([route_p, route_s], axis=0)
    n_all = n_p + dec_batch
    eid = route_all[:, 0:2].astype(jnp.int32)
    max_tiles = (2 * n_all) // tm + N_EXPERTS
    pos, tile_expert, src_tok, n_tiles = _dispatch_plan(eid, tm, max_tiles)
    ys = _experts(h2_all, tile_expert, src_tok, n_tiles, w_expert_gate[0], w_expert_up[0], w_expert_down[0],
                  tm, max_tiles)

    gfin = norm_final_g.reshape(1, D_MODEL)
    y_p = _combine(pos[:2 * n_p], x1_p, route_p, gfin, ys, tb)
    y_s = _combine(pos[2 * n_p:], x1_s, route_s, gfin, ys, dec_batch)

    return (y_p.reshape(batch, seq, D_MODEL), y_s.reshape(dec_batch, 1, D_MODEL),
            nconv_p[None, :, HIST_PAD:, :], nhgrn_p[None],
            nconv_s.reshape(1, dec_batch, CONV_WIDTH - 1, CONV_DIM), nhgrn_s[None])
```

```python
import functools

import jax
import jax.numpy as jnp
from jax import lax
from jax.experimental import pallas as pl
from jax.experimental.pallas import tpu as pltpu

D_MODEL = 1024
CONV_DIM = D_MODEL // 2
CONV_WIDTH = 31
HG_HEADS = 8
HG_DK = 128
HG_DV = 128
HG_DIM = HG_HEADS * HG_DK
N_GROUPS = 4
EXPERTS_PER_GROUP = 8
N_EXPERTS = N_GROUPS * EXPERTS_PER_GROUP
EXPERT_FF = D_MODEL // 2
EPS = 1e-6
IN_COLS = 2 * CONV_DIM + 4 * HG_DIM + 2 * D_MODEL

OFF_CV, OFF_CG = 0, CONV_DIM
OFF_Q = 2 * CONV_DIM
OFF_F = OFF_Q + HG_DIM
OFF_V = OFF_F + HG_DIM
OFF_OG = OFF_V + HG_DIM
OFF_GA = OFF_OG + HG_DIM
OFF_GB = OFF_GA + D_MODEL

SUBLANES = 8
LANES = 128
HIST_ROWS = 32
HIST_PAD = HIST_ROWS - (CONV_WIDTH - 1)
CHUNK = 64
NBLK = CHUNK // SUBLANES
ROUTE_LANES = 128
ROW_W = D_MODEL
VMEM_LIMIT = 56 * 1024 * 1024

BF = jnp.bfloat16
F32 = jnp.float32


def _dot(a, b):
    return jnp.dot(a, b, preferred_element_type=F32)


def _dot_nt(a, b):
    return lax.dot_general(a, b, (((1,), (1,)), ((), ())), preferred_element_type=F32)


def _dot_tn(a, b, precision=None):
    return lax.dot_general(a, b, (((0,), (0,)), ((), ())), preferred_element_type=F32,
                           precision=precision)


def _sigmoid(x):
    return 1.0 / (1.0 + jnp.exp(-x))


def _silu(x):
    return x * _sigmoid(x)


def _rms(xf, g):
    return xf * lax.rsqrt(jnp.mean(xf * xf, axis=-1, keepdims=True) + EPS) * g


def _lower_bound(lb_param):
    m = jnp.max(lb_param, axis=0, keepdims=True)
    e = jnp.exp(lb_param - m)
    return e[0:1] / jnp.sum(e, axis=0, keepdims=True)


def _conv_post(c, conv_b, ln_g, ln_b):
    c = c + conv_b
    mu = jnp.mean(c, axis=-1, keepdims=True)
    d = c - mu
    var = jnp.mean(d * d, axis=-1, keepdims=True)
    return _silu(d * lax.rsqrt(var + EPS) * ln_g + ln_b)


def _head_norm(o, g):
    return o * lax.rsqrt(jnp.mean(o * o, axis=-1, keepdims=True) + EPS) * g


def _route(logits):
    col = lax.broadcasted_iota(jnp.int32, logits.shape, 1)
    big = jnp.int32(1 << 20)
    neg = jnp.float32(-jnp.inf)
    gmask = col < N_GROUPS
    lg = jnp.where(gmask, logits, neg)
    gmax = jnp.max(lg, axis=-1, keepdims=True)
    gsum = jnp.sum(jnp.where(gmask, jnp.exp(lg - gmax), 0.0), axis=-1, keepdims=True)
    gval = 1.0 / gsum
    gidx = jnp.min(jnp.where(lg == gmax, col, big), axis=-1, keepdims=True)
    lo = N_GROUPS + EXPERTS_PER_GROUP * gidx
    emask = (col >= lo) & (col < lo + EXPERTS_PER_GROUP)
    el = jnp.where(emask, logits, neg)
    m1 = jnp.max(el, axis=-1, keepdims=True)
    i1 = jnp.min(jnp.where(el == m1, col, big), axis=-1, keepdims=True)
    el2 = jnp.where(col == i1, neg, el)
    m2 = jnp.max(el2, axis=-1, keepdims=True)
    i2 = jnp.min(jnp.where(el2 == m2, col, big), axis=-1, keepdims=True)
    r = jnp.exp(m2 - m1)
    w1 = gval / (1.0 + r)
    w2 = gval * r / (1.0 + r)
    e1 = (i1 - N_GROUPS).astype(F32)
    e2 = (i2 - N_GROUPS).astype(F32)
    return jnp.where(col == 0, e1, jnp.where(col == 1, e2, jnp.where(col == 2, w1, jnp.where(col == 3, w2, 0.0))))


def _bcast_blocks(g):
    return jnp.concatenate(
        [jnp.broadcast_to(g[j:j + 1, :], (SUBLANES, g.shape[1])) for j in range(g.shape[0])], axis=0)


def _hgrn_chunk(q, k, v, lf, st, blk_ones, bl_ref, k_ref):
    row = lax.broadcasted_iota(jnp.int32, (CHUNK, LANES), 0)
    rin = row % SUBLANES
    bl = lf
    for s in (1, 2, 4):
        bl = bl + jnp.where(rin >= s, pltpu.roll(bl, s, axis=0), 0.0)
    bl_ref[...] = bl
    k_ref[...] = k
    tot = bl_ref[pl.ds(SUBLANES - 1, NBLK, stride=SUBLANES), :]
    brow = lax.broadcasted_iota(jnp.int32, (NBLK, LANES), 0)
    rb = tot
    for s in (1, 2, 4):
        rb = rb + jnp.where(brow >= s, pltpu.roll(rb, s, axis=0), 0.0)
    rb_prev = rb - tot
    total = rb[NBLK - 1:NBLK, :]

    p_in = jnp.exp(bl)
    k_out = jnp.exp(_bcast_blocks(tot) - bl)
    qp = q * p_in
    kp = k * k_out

    qe = qp * _bcast_blocks(jnp.exp(rb_prev))
    kdec = kp * _bcast_blocks(jnp.exp(total - rb))
    st_b = st.astype(BF)
    o = _dot_nt(qe.astype(BF), st_b)
    v_b = v.astype(BF)
    st_new = st * jnp.exp(total) + _dot_tn(v_b, kdec.astype(BF))

    ti = lax.broadcasted_iota(jnp.int32, (CHUNK, CHUNK), 0)
    si = lax.broadcasted_iota(jnp.int32, (CHUNK, CHUNK), 1)
    scores = jnp.zeros((CHUNK, CHUNK), F32)
    cb = NBLK // 2
    while cb >= 1:
        edges = [(j // (2 * cb)) * (2 * cb) + cb - 1 for j in range(NBLK)]
        rb_edge = jnp.concatenate([rb[e:e + 1, :] for e in edges], axis=0)
        late = (brow // cb) % 2 == 1
        gq = jnp.where(late, jnp.exp(jnp.minimum(rb_prev - rb_edge, 0.0)), 0.0)
        gk = jnp.where(late, 0.0, jnp.exp(jnp.minimum(rb_edge - rb, 0.0)))
        qc = (qp * _bcast_blocks(gq)).astype(BF)
        kc = (kp * _bcast_blocks(gk)).astype(BF)
        span = SUBLANES * 2 * cb
        scores = scores + jnp.where(ti // span == si // span, _dot_nt(qc, kc), 0.0)
        cb //= 2

    parts = []
    for c in range(SUBLANES):
        blc = _bcast_rows_of_block(bl_ref, c)
        kc = _bcast_rows_of_block(k_ref, c)
        parts.append((q * kc * jnp.exp(jnp.minimum(bl - blc, 0.0))).astype(BF))
    diag = _dot(jnp.concatenate(parts, axis=1), blk_ones)
    scores = scores + jnp.where((ti // SUBLANES == si // SUBLANES) & (si <= ti), diag, 0.0)

    o = o + _dot(scores.astype(BF), v_b)
    return o, st_new


def _bcast_rows_of_block(ref, c):
    return jnp.concatenate(
        [jnp.broadcast_to(ref[pl.ds(j * SUBLANES + c, 1), :], (SUBLANES, ref.shape[1])) for j in range(NBLK)], axis=0)


def _mixer_tail(x, y_a, y_b, ga, gb, b_gates, w_out, g_ffn, w_router):
    m = _sigmoid(ga + b_gates[:, :D_MODEL]) * y_a + _sigmoid(gb + b_gates[:, D_MODEL:]) * y_b
    x1 = x + _dot(m.astype(BF), w_out)
    h2 = _rms(x1, g_ffn).astype(BF)
    route = _route(_dot(h2, w_router))
    return x1, h2.astype(F32), route


def _prompt_mixer_kernel(x_ref, gmix_ref, win_ref, bg_ref, cw_ref, cb_ref, lng_ref, lnb_ref, wco_ref,
                         lbp_ref, hng_ref, who_ref, wout_ref, gffn_ref, wr_ref, ones_ref,
                         x1_ref, h2_ref, route_ref, nconv_ref, nhgrn_ref,
                         hist_ref, st_ref, q_s, k_s, v_s, lf_s, o_s, bl_s, kc_s, *, tb):
    t = pl.program_id(1)

    @pl.when(t == 0)
    def _():
        hist_ref[pl.ds(0, HIST_ROWS), :] = jnp.zeros((HIST_ROWS, CONV_DIM), F32)
        st_ref[...] = jnp.zeros_like(st_ref)

    x = x_ref[...]
    h = _rms(x, gmix_ref[...]).astype(BF)

    cv = _dot(h, win_ref[:, OFF_CV:OFF_CV + CONV_DIM])
    cg = _dot(h, win_ref[:, OFF_CG:OFF_CG + CONV_DIM])
    hist_ref[pl.ds(HIST_ROWS, tb), :] = cv * _sigmoid(cg)
    acc = jnp.zeros((tb, CONV_DIM), F32)
    for r in range(SUBLANES):
        taps = range(r, CONV_WIDTH, SUBLANES)
        g = hist_ref[pl.ds(HIST_PAD + r, tb + SUBLANES * (len(taps) - 1)), :]
        for a, j in enumerate(taps):
            acc = acc + g[SUBLANES * a:SUBLANES * a + tb, :] * cw_ref[pl.ds(j, 1), :]
    tail = hist_ref[pl.ds(tb, HIST_ROWS), :]
    hist_ref[pl.ds(0, HIST_ROWS), :] = tail
    nconv_ref[0] = tail
    c = _conv_post(acc, cb_ref[...], lng_ref[...], lnb_ref[...])
    y_a = _dot(c.astype(BF), wco_ref[...])

    lb = _lower_bound(lbp_ref[...])
    qv = _dot(h, win_ref[:, OFF_Q:OFF_Q + HG_DIM])
    q_s[...] = _silu(qv) * (HG_DK ** -0.5)
    fv = _dot(h, win_ref[:, OFF_F:OFF_F + HG_DIM])
    fg = lb + (1.0 - lb) * _sigmoid(fv)
    lf_s[...] = jnp.log(fg)
    k_s[...] = 1.0 - fg
    v_s[...] = _dot(h, win_ref[:, OFF_V:OFF_V + HG_DIM])
    blk_ones = ones_ref[...]

    def chunk_body(ci, carry):
        r0 = pl.multiple_of(ci * CHUNK, CHUNK)
        for hd in range(HG_HEADS):
            cs = slice(hd * HG_DK, (hd + 1) * HG_DK)
            o, st_new = _hgrn_chunk(q_s[pl.ds(r0, CHUNK), cs], k_s[pl.ds(r0, CHUNK), cs],
                                    v_s[pl.ds(r0, CHUNK), cs], lf_s[pl.ds(r0, CHUNK), cs],
                                    st_ref[hd], blk_ones, bl_s, kc_s)
            st_ref[hd] = st_new
            o_s[pl.ds(r0, CHUNK), cs] = _head_norm(o, hng_ref[:, cs])
        return carry

    lax.fori_loop(0, tb // CHUNK, chunk_body, 0)
    for hd in range(HG_HEADS):
        nhgrn_ref[0, hd] = st_ref[hd].T

    og = _dot(h, win_ref[:, OFF_OG:OFF_OG + HG_DIM])
    y_b = _dot((o_s[...] * _silu(og)).astype(BF), who_ref[...])

    ga = _dot(h, win_ref[:, OFF_GA:OFF_GA + D_MODEL])
    gb = _dot(h, win_ref[:, OFF_GB:OFF_GB + D_MODEL])
    x1, h2, route = _mixer_tail(x, y_a, y_b, ga, gb, bg_ref[...], wout_ref[...], gffn_ref[...], wr_ref[...])
    x1_ref[...] = x1
    h2_ref[...] = h2
    route_ref[...] = route


def _const_spec(shape):
    nd = len(shape)
    return pl.BlockSpec(shape, lambda *_: (0,) * nd, pipeline_mode=pl.Buffered(1))


def _block_ones():
    r = jnp.arange(SUBLANES * LANES) // LANES
    s = jnp.arange(CHUNK) % SUBLANES
    return (r[:, None] == s[None, :]).astype(BF)


def _prompt_mixer(x2, batch, seq, tb, wts):
    nt = seq // tb
    n = batch * seq
    row_spec = lambda w: pl.BlockSpec((tb, w), lambda b, t: (b * nt + t, 0))
    consts = [wts['gmix'], wts['w_in'], wts['b_gates'], wts['conv_w'], wts['conv_b'], wts['ln_g'], wts['ln_b'],
              wts['w_conv_out'], wts['lb_param'], wts['hn_g'], wts['w_hgrn_out'], wts['w_out'], wts['gffn'],
              wts['w_router'], _block_ones()]
    return pl.pallas_call(
        functools.partial(_prompt_mixer_kernel, tb=tb),
        grid=(batch, nt),
        in_specs=[row_spec(D_MODEL)] + [_const_spec(c.shape) for c in consts],
        out_specs=[row_spec(D_MODEL), row_spec(ROW_W), row_spec(ROUTE_LANES),
                   pl.BlockSpec((1, HIST_ROWS, CONV_DIM), lambda b, t: (b, 0, 0)),
                   pl.BlockSpec((1, HG_HEADS, HG_DK, HG_DV), lambda b, t: (b, 0, 0, 0))],
        out_shape=[jax.ShapeDtypeStruct((n, D_MODEL), F32), jax.ShapeDtypeStruct((n, ROW_W), F32),
                   jax.ShapeDtypeStruct((n, ROUTE_LANES), F32),
                   jax.ShapeDtypeStruct((batch, HIST_ROWS, CONV_DIM), F32),
                   jax.ShapeDtypeStruct((batch, HG_HEADS, HG_DK, HG_DV), F32)],
        scratch_shapes=[pltpu.VMEM((HIST_ROWS + tb, CONV_DIM), F32),
                        pltpu.VMEM((HG_HEADS, HG_DV, HG_DK), F32)]
                       + [pltpu.VMEM((tb, HG_DIM), F32)] * 5
                       + [pltpu.VMEM((CHUNK, LANES), F32)] * 2,
        compiler_params=pltpu.CompilerParams(dimension_semantics=("arbitrary", "arbitrary"),
                                             vmem_limit_bytes=VMEM_LIMIT),
        name="prompt_mixer",
    )(x2, *consts)


def _sample_proj_kernel(x_ref, gmix_ref, win_ref, proj_ref):
    h = _rms(x_ref[...], gmix_ref[...]).astype(BF)
    proj_ref[...] = _dot(h, win_ref[...])


def _sample_proj(xs, wts):
    n = xs.shape[0]
    return pl.pallas_call(
        _sample_proj_kernel,
        grid=(1,),
        in_specs=[_const_spec(xs.shape), _const_spec(wts['gmix'].shape), _const_spec(wts['w_in'].shape)],
        out_specs=pl.BlockSpec((n, IN_COLS), lambda i: (0, 0)),
        out_shape=jax.ShapeDtypeStruct((n, IN_COLS), F32),
        compiler_params=pltpu.CompilerParams(vmem_limit_bytes=VMEM_LIMIT),
        name="sample_proj",
    )(xs, wts['gmix'], wts['w_in'])


def _sample_state_kernel(q_ref, f_ref, v_ref, lbp_ref, hng_ref, s_ref, snew_ref, o_ref, *, bs):
    lb = _lower_bound(lbp_ref[...])
    qf = _silu(q_ref[...]) * (HG_DK ** -0.5)
    fg = lb + (1.0 - lb) * _sigmoid(f_ref[...])
    kf = 1.0 - fg
    v = v_ref[...]
    rows = lax.broadcasted_iota(jnp.int32, (bs, LANES), 0)
    hi = lax.Precision.HIGHEST
    for hd in range(HG_HEADS):
        cs = slice(hd * HG_DK, (hd + 1) * HG_DK)
        o_h = jnp.zeros((bs, HG_DV), F32)
        for r in range(bs):
            sel = rows == r
            f_col = _dot_tn(fg[:, cs], jnp.where(sel, 1.0, 0.0), precision=hi)
            kv = _dot_tn(kf[:, cs], jnp.where(sel, v[:, cs], 0.0), precision=hi)
            s_new = f_col * s_ref[r, hd] + kv
            snew_ref[r, hd] = s_new
            o_h = o_h + jnp.dot(jnp.where(sel, qf[:, cs], 0.0), s_new, preferred_element_type=F32, precision=hi)
        o_ref[:, cs] = _head_norm(o_h, hng_ref[:, cs])


def _sample_state(proj, state, wts, bs):
    n = proj.shape[0]
    col_spec = lambda off: pl.BlockSpec((bs, HG_DIM), lambda i, off=off: (i, off // HG_DIM))
    st_spec = pl.BlockSpec((bs, HG_HEADS, HG_DK, HG_DV), lambda i: (i, 0, 0, 0))
    return pl.pallas_call(
        functools.partial(_sample_state_kernel, bs=bs),
        grid=(n // bs,),
        in_specs=[col_spec(OFF_Q), col_spec(OFF_F), col_spec(OFF_V),
                  _const_spec(wts['lb_param'].shape), _const_spec(wts['hn_g'].shape), st_spec],
        out_specs=[st_spec, pl.BlockSpec((bs, HG_DIM), lambda i: (i, 0))],
        out_shape=[jax.ShapeDtypeStruct(state.shape, F32), jax.ShapeDtypeStruct((n, HG_DIM), F32)],
        compiler_params=pltpu.CompilerParams(dimension_semantics=("arbitrary",), vmem_limit_bytes=VMEM_LIMIT),
        name="sample_state",
    )(proj, proj, proj, wts['lb_param'], wts['hn_g'], state)


def _sample_tail_kernel(x_ref, proj_ref, o_ref, cst_ref, bg_ref, cw_ref, cb_ref, lng_ref, lnb_ref, wco_ref,
                        who_ref, wout_ref, gffn_ref, wr_ref, x1_ref, h2_ref, route_ref, nconv_ref):
    keep = (CONV_WIDTH - 2) * CONV_DIM
    u = proj_ref[:, OFF_CV:OFF_CV + CONV_DIM] * _sigmoid(proj_ref[:, OFF_CG:OFF_CG + CONV_DIM])
    acc = u * cw_ref[pl.ds(CONV_WIDTH - 1, 1), :]
    for j in range(CONV_WIDTH - 1):
        acc = acc + cst_ref[:, j * CONV_DIM:(j + 1) * CONV_DIM] * cw_ref[pl.ds(j, 1), :]
    nconv_ref[:, :keep] = cst_ref[:, CONV_DIM:]
    nconv_ref[:, keep:] = u
    c = _conv_post(acc, cb_ref[...], lng_ref[...], lnb_ref[...])
    y_a = _dot(c.astype(BF), wco_ref[...])
    og = proj_ref[:, OFF_OG:OFF_OG + HG_DIM]
    y_b = _dot((o_ref[...] * _silu(og)).astype(BF), who_ref[...])
    x1, h2, route = _mixer_tail(x_ref[...], y_a, y_b, proj_ref[:, OFF_GA:OFF_GA + D_MODEL],
                                proj_ref[:, OFF_GB:OFF_GB + D_MODEL], bg_ref[...], wout_ref[...],
                                gffn_ref[...], wr_ref[...])
    x1_ref[...] = x1
    h2_ref[...] = h2
    route_ref[...] = route


def _sample_tail(xs, proj, o, conv_state2, wts):
    n = xs.shape[0]
    ins = [xs, proj, o, conv_state2, wts['b_gates'], wts['conv_w'], wts['conv_b'], wts['ln_g'], wts['ln_b'],
           wts['w_conv_out'], wts['w_hgrn_out'], wts['w_out'], wts['gffn'], wts['w_router']]
    full = lambda shape: pl.BlockSpec(shape, lambda i: (0,) * len(shape))
    return pl.pallas_call(
        _sample_tail_kernel,
        grid=(1,),
        in_specs=[_const_spec(a.shape) for a in ins],
        out_specs=[full((n, D_MODEL)), full((n, ROW_W)), full((n, ROUTE_LANES)), full(conv_state2.shape)],
        out_shape=[jax.ShapeDtypeStruct((n, D_MODEL), F32), jax.ShapeDtypeStruct((n, ROW_W), F32),
                   jax.ShapeDtypeStruct((n, ROUTE_LANES), F32), jax.ShapeDtypeStruct(conv_state2.shape, F32)],
        compiler_params=pltpu.CompilerParams(vmem_limit_bytes=VMEM_LIMIT),
        name="sample_tail",
    )(*ins)


def _plan_kernel(slab_p_ref, slab_s_ref, pos_ref, tinfo_ref, e_ref, pre_ref, *, tm, n_tok, max_tiles):
    n_p = slab_p_ref.shape[0]
    nblk = n_tok // LANES
    blk_p = n_p // LANES
    eidx = lax.broadcasted_iota(jnp.int32, (N_EXPERTS, LANES), 0).astype(F32)
    ti = lax.broadcasted_iota(jnp.int32, (LANES, LANES), 0)
    si = lax.broadcasted_iota(jnp.int32, (LANES, LANES), 1)
    before = (ti < si).astype(BF)

    def onehots(b):
        c0 = pl.multiple_of(b * LANES, LANES)
        e1 = e_ref[0:1, pl.ds(c0, LANES)]
        e2 = e_ref[1:2, pl.ds(c0, LANES)]
        return (e1 == eidx).astype(F32), (e2 == eidx).astype(F32), c0

    def transpose_block(slab_ref, b_local, b_global):
        r0 = pl.multiple_of(b_local * LANES, LANES)
        c0 = pl.multiple_of(b_global * LANES, LANES)
        e_ref[:, pl.ds(c0, LANES)] = slab_ref[pl.ds(r0, LANES), :].T[0:SUBLANES, :]

    def load_p(b, c):
        transpose_block(slab_p_ref, b, b)
        return c

    lax.fori_loop(0, blk_p, load_p, 0)

    def load_s(b, c):
        transpose_block(slab_s_ref, b, b + blk_p)
        return c

    lax.fori_loop(0, nblk - blk_p, load_s, 0)

    def count(b, carry):
        h1, h2, c0 = onehots(b)
        h = h1 + h2
        pre_ref[:, pl.ds(c0, LANES)] = _dot(h.astype(BF), before) + carry
        return carry + jnp.sum(h, axis=1, keepdims=True)

    counts = lax.fori_loop(0, nblk, count, jnp.zeros((N_EXPERTS, 1), F32))
    tiles_per = jnp.floor((counts + (tm - 1)) * (1.0 / tm))
    ei = lax.broadcasted_iota(jnp.int32, (N_EXPERTS, N_EXPERTS), 0)
    ej = lax.broadcasted_iota(jnp.int32, (N_EXPERTS, N_EXPERTS), 1)
    upto = (ej <= ei).astype(BF)
    tile_end = _dot(upto, jnp.broadcast_to(tiles_per, (N_EXPERTS, LANES)).astype(BF))[:, 0:1]
    starts = (tile_end - tiles_per) * tm

    def place(b, c):
        h1, h2, c0 = onehots(b)
        dest = pre_ref[:, pl.ds(c0, LANES)] + starts
        pos_ref[0:1, pl.ds(c0, LANES)] = jnp.sum(h1 * dest, axis=0, keepdims=True).astype(jnp.int32)
        pos_ref[1:2, pl.ds(c0, LANES)] = jnp.sum(h2 * dest, axis=0, keepdims=True).astype(jnp.int32)
        return c

    pos_ref[...] = jnp.zeros_like(pos_ref)
    lax.fori_loop(0, nblk, place, 0)

    tile = lax.broadcasted_iota(jnp.int32, (N_EXPERTS, max_tiles), 1).astype(F32)
    t_exp = jnp.sum((tile_end <= tile).astype(F32), axis=0, keepdims=True)
    t_exp = jnp.minimum(t_exp, N_EXPERTS - 1.0).astype(jnp.int32)
    n_used = jnp.broadcast_to(tile_end[N_EXPERTS - 1:N_EXPERTS, :], (1, max_tiles)).astype(jnp.int32)
    row = lax.broadcasted_iota(jnp.int32, (SUBLANES, max_tiles), 0)
    tinfo_ref[...] = jnp.where(row == 0, t_exp, jnp.where(row == 1, n_used, 0))


def _plan(slab_p, slab_s, tm, max_tiles):
    n_tok = slab_p.shape[0] + slab_s.shape[0]
    assert slab_p.shape[0] % LANES == 0 and slab_s.shape[0] % LANES == 0
    vm = pl.BlockSpec(memory_space=pltpu.VMEM)
    return pl.pallas_call(
        functools.partial(_plan_kernel, tm=tm, n_tok=n_tok, max_tiles=max_tiles),
        in_specs=[vm, vm],
        out_specs=[vm, vm],
        out_shape=[jax.ShapeDtypeStruct((SUBLANES, n_tok), jnp.int32),
                   jax.ShapeDtypeStruct((SUBLANES, max_tiles), jnp.int32)],
        scratch_shapes=[pltpu.VMEM((SUBLANES, n_tok), F32), pltpu.VMEM((N_EXPERTS, n_tok), F32)],
        compiler_params=pltpu.CompilerParams(vmem_limit_bytes=VMEM_LIMIT),
        name="route_plan",
    )(slab_p, slab_s)


def _dispatch_kernel(pos_ref, h_ref, xs_in_ref, xs_ref, sem, *, tb, tok0):
    del xs_in_ref
    i = pl.program_id(0)

    def copies(r):
        base = 2 * (tok0 + i * tb + r)
        src = h_ref.at[pl.ds(r, 1), :]
        return (pltpu.make_async_copy(src, xs_ref.at[pl.ds(pos_ref[base], 1), :], sem),
                pltpu.make_async_copy(src, xs_ref.at[pl.ds(pos_ref[base + 1], 1), :], sem))

    def issue(r, c):
        c0, c1 = copies(r)
        c0.start()
        c1.start()
        return c

    def drain(r, c):
        c0, c1 = copies(r)
        c0.wait()
        c1.wait()
        return c

    lax.fori_loop(0, tb, issue, 0, unroll=8)
    lax.fori_loop(0, tb, drain, 0, unroll=8)


def _dispatch(pos_flat, hpk, xs, tb, tok0):
    n = hpk.shape[0]
    grid_spec = pltpu.PrefetchScalarGridSpec(
        num_scalar_prefetch=1,
        grid=(n // tb,),
        in_specs=[pl.BlockSpec((tb, ROW_W), lambda i, p: (i, 0)),
                  pl.BlockSpec(memory_space=pl.ANY)],
        out_specs=pl.BlockSpec(memory_space=pl.ANY),
        scratch_shapes=[pltpu.SemaphoreType.DMA(())])
    return pl.pallas_call(
        functools.partial(_dispatch_kernel, tb=tb, tok0=tok0),
        grid_spec=grid_spec,
        out_shape=jax.ShapeDtypeStruct(xs.shape, xs.dtype),
        input_output_aliases={2: 0},
        compiler_params=pltpu.CompilerParams(dimension_semantics=("arbitrary",), vmem_limit_bytes=VMEM_LIMIT),
        name="dispatch",
    )(pos_flat, hpk, xs)


def _expert_kernel(te_ref, nt_ref, xs_ref, wg_ref, wu_ref, wd_ref, y_ref, wg_b, wu_b, wd_b):
    i = pl.program_id(0)

    @pl.when(i < nt_ref[0])
    def _():
        changed = jnp.logical_or(i == 0, te_ref[i] != te_ref[jnp.maximum(i - 1, 0)])

        @pl.when(changed)
        def _():
            wg_b[...] = wg_ref[0].astype(BF)
            wu_b[...] = wu_ref[0].astype(BF)
            wd_b[...] = wd_ref[0].astype(BF)

        xb = xs_ref[...].astype(BF)
        gate = _dot(xb, wg_b[...])
        up = _dot(xb, wu_b[...])
        y_ref[...] = _dot((_silu(gate) * up).astype(BF), wd_b[...])

    @pl.when(i >= nt_ref[0])
    def _():
        y_ref[...] = jnp.zeros_like(y_ref)


def _experts(xs, tile_expert, n_tiles, wg, wu, wd, tm, max_tiles):
    grid_spec = pltpu.PrefetchScalarGridSpec(
        num_scalar_prefetch=2,
        grid=(max_tiles,),
        in_specs=[pl.BlockSpec((tm, ROW_W), lambda i, te, nt: (i, 0)),
                  pl.BlockSpec((1, D_MODEL, EXPERT_FF), lambda i, te, nt: (te[i], 0, 0)),
                  pl.BlockSpec((1, D_MODEL, EXPERT_FF), lambda i, te, nt: (te[i], 0, 0)),
                  pl.BlockSpec((1, EXPERT_FF, D_MODEL), lambda i, te, nt: (te[i], 0, 0))],
        out_specs=pl.BlockSpec((tm, D_MODEL), lambda i, te, nt: (i, 0)),
        scratch_shapes=[pltpu.VMEM((D_MODEL, EXPERT_FF), BF), pltpu.VMEM((D_MODEL, EXPERT_FF), BF),
                        pltpu.VMEM((EXPERT_FF, D_MODEL), BF)])
    return pl.pallas_call(
        _expert_kernel,
        grid_spec=grid_spec,
        out_shape=jax.ShapeDtypeStruct((max_tiles * tm, D_MODEL), F32),
        compiler_params=pltpu.CompilerParams(dimension_semantics=("arbitrary",), vmem_limit_bytes=VMEM_LIMIT),
        name="experts",
    )(tile_expert, n_tiles, xs, wg, wu, wd)


def _combine_kernel(pos_ref, x1_ref, route_ref, gfin_ref, ys_ref, y_ref, buf0, buf1, sem, *, tb, tok0):
    i = pl.program_id(0)

    def copies(r):
        base = 2 * (tok0 + i * tb + r)
        return (pltpu.make_async_copy(ys_ref.at[pl.ds(pos_ref[base], 1), :], buf0.at[pl.ds(r, 1), :], sem),
                pltpu.make_async_copy(ys_ref.at[pl.ds(pos_ref[base + 1], 1), :], buf1.at[pl.ds(r, 1), :], sem))

    def issue(r, c):
        c0, c1 = copies(r)
        c0.start()
        c1.start()
        return c

    def drain(r, c):
        c0, c1 = copies(r)
        c0.wait()
        c1.wait()
        return c

    lax.fori_loop(0, tb, issue, 0, unroll=8)
    lax.fori_loop(0, tb, drain, 0, unroll=8)
    route = route_ref[...]
    out = x1_ref[...] + (route[:, 2:3] * buf0[...] + route[:, 3:4] * buf1[...])
    y_ref[...] = _rms(out, gfin_ref[...])


def _combine(pos_flat, x1, route, gfin, ys, tb, tok0):
    n = x1.shape[0]
    grid_spec = pltpu.PrefetchScalarGridSpec(
        num_scalar_prefetch=1,
        grid=(n // tb,),
        in_specs=[pl.BlockSpec((tb, D_MODEL), lambda i, p: (i, 0)),
                  pl.BlockSpec((tb, ROUTE_LANES), lambda i, p: (i, 0)),
                  pl.BlockSpec((1, D_MODEL), lambda i, p: (0, 0)),
                  pl.BlockSpec(memory_space=pl.ANY)],
        out_specs=pl.BlockSpec((tb, D_MODEL), lambda i, p: (i, 0)),
        scratch_shapes=[pltpu.VMEM((tb, D_MODEL), F32), pltpu.VMEM((tb, D_MODEL), F32),
                        pltpu.SemaphoreType.DMA(())])
    return pl.pallas_call(
        functools.partial(_combine_kernel, tb=tb, tok0=tok0),
        grid_spec=grid_spec,
        out_shape=jax.ShapeDtypeStruct((n, D_MODEL), F32),
        compiler_params=pltpu.CompilerParams(dimension_semantics=("arbitrary",), vmem_limit_bytes=VMEM_LIMIT),
        name="combine",
    )(pos_flat, x1, route, gfin, ys)


def kernel(x_prompt, x_sample, state_conv, state_hgrn, norm_mix_g, w_in, b_gates, conv_dw_w, conv_dw_b,
           conv_ln_g, conv_ln_b, w_conv_out, hgrn_lb_param, hgrn_norm_g, w_hgrn_out, w_out, norm_ffn_g,
           w_router_group, w_router_expert, w_expert_gate, w_expert_up, w_expert_down, norm_final_g):
    batch, seq, _ = x_prompt.shape
    dec_batch = x_sample.shape[0]
    assert x_sample.shape[1] == 1 and w_in.shape[0] == 1
    tb = min(256, seq)
    tm = 256
    bs = min(8, dec_batch)
    n_p = batch * seq
    n_all = n_p + dec_batch
    tbd = min(1024, n_p)
    tbc = min(512, n_p)
    assert seq % tb == 0 and tb % CHUNK == 0 and dec_batch % bs == 0 and n_p % tbd == 0 and n_p % tbc == 0

    w_router = jnp.concatenate(
        [w_router_group[0], w_router_expert[0],
         jnp.zeros((D_MODEL, ROUTE_LANES - N_GROUPS - N_EXPERTS), F32)], axis=1).astype(BF)
    wts = dict(gmix=norm_mix_g, w_in=w_in[0].astype(BF), b_gates=b_gates, conv_w=conv_dw_w[0], conv_b=conv_dw_b,
               ln_g=conv_ln_g, ln_b=conv_ln_b, w_conv_out=w_conv_out[0].astype(BF), lb_param=hgrn_lb_param,
               hn_g=hgrn_norm_g, w_hgrn_out=w_hgrn_out[0].astype(BF), w_out=w_out[0].astype(BF),
               gffn=norm_ffn_g, w_router=w_router)

    x1_p, hpk_p, route_p, nconv_p, nhgrn_p = _prompt_mixer(x_prompt.reshape(n_p, D_MODEL), batch, seq, tb, wts)

    xs_tok = x_sample.reshape(dec_batch, D_MODEL)
    proj_s = _sample_proj(xs_tok, wts)
    nhgrn_s, o_s = _sample_state(proj_s, state_hgrn[0], wts, bs)
    conv2 = state_conv[0].reshape(dec_batch, (CONV_WIDTH - 1) * CONV_DIM)
    x1_s, hpk_s, route_s, nconv_s = _sample_tail(xs_tok, proj_s, o_s, conv2, wts)

    max_tiles = -(-((2 * n_all) // tm + N_EXPERTS) // SUBLANES) * SUBLANES
    pos, tinfo = _plan(route_p, route_s, tm, max_tiles)
    pos_flat = pos[0:2].T.reshape(-1)
    tile_expert, n_tiles = tinfo[0], tinfo[1, 0:1]

    xs = jnp.zeros((max_tiles * tm, ROW_W), F32)
    xs = _dispatch(pos_flat, hpk_p, xs, tbd, 0)
    xs = _dispatch(pos_flat, hpk_s, xs, dec_batch, n_p)
    ys = _experts(xs, tile_expert, n_tiles, w_expert_gate[0], w_expert_up[0], w_expert_down[0], tm, max_tiles)

    gfin = norm_final_g.reshape(1, D_MODEL)
    y_p = _combine(pos_flat, x1_p, route_p, gfin, ys, tbc, 0)
    y_s = _combine(pos_flat, x1_s, route_s, gfin, ys, dec_batch, n_p)

    return (y_p.reshape(batch, seq, D_MODEL), y_s.reshape(dec_batch, 1, D_MODEL),
            nconv_p[None, :, HIST_PAD:, :], nhgrn_p[None],
            nconv_s.reshape(1, dec_batch, CONV_WIDTH - 1, CONV_DIM), nhgrn_s[None])
```

```python
import functools

import jax
import jax.numpy as jnp
from jax import lax
from jax.experimental import pallas as pl
from jax.experimental.pallas import tpu as pltpu

D_MODEL = 1024
CONV_DIM = D_MODEL // 2
CONV_WIDTH = 31
HG_HEADS = 8
HG_DK = 128
HG_DV = 128
HG_DIM = HG_HEADS * HG_DK
N_GROUPS = 4
EXPERTS_PER_GROUP = 8
N_EXPERTS = N_GROUPS * EXPERTS_PER_GROUP
EXPERT_FF = D_MODEL // 2
EPS = 1e-6
IN_COLS = 2 * CONV_DIM + 4 * HG_DIM + 2 * D_MODEL

OFF_CV, OFF_CG = 0, CONV_DIM
OFF_Q = 2 * CONV_DIM
OFF_F = OFF_Q + HG_DIM
OFF_V = OFF_F + HG_DIM
OFF_OG = OFF_V + HG_DIM
OFF_GA = OFF_OG + HG_DIM
OFF_GB = OFF_GA + D_MODEL

SUBLANES = 8
LANES = 128
HIST_ROWS = 32
HIST_PAD = HIST_ROWS - (CONV_WIDTH - 1)
CHUNK = 64
NBLK = CHUNK // SUBLANES
CONV_ROWS = 32
ROUTE_LANES = 128
ROW_W = D_MODEL
VMEM_LIMIT = 56 * 1024 * 1024

BF = jnp.bfloat16
F32 = jnp.float32


def _dot(a, b):
    return jnp.dot(a, b, preferred_element_type=F32)


def _dot_nt(a, b):
    return lax.dot_general(a, b, (((1,), (1,)), ((), ())), preferred_element_type=F32)


def _dot_tn(a, b, precision=None):
    return lax.dot_general(a, b, (((0,), (0,)), ((), ())), preferred_element_type=F32,
                           precision=precision)


NEG_LOG2E = -1.4426950408889634


def _sigmoid(x):
    return 1.0 / (1.0 + jnp.exp2(x * NEG_LOG2E))


def _silu(x):
    return x * _sigmoid(x)


def _rms(xf, g):
    return xf * lax.rsqrt(jnp.mean(xf * xf, axis=-1, keepdims=True) + EPS) * g


def _lower_bound(lb_param):
    m = jnp.max(lb_param, axis=0, keepdims=True)
    e = jnp.exp(lb_param - m)
    return e[0:1] / jnp.sum(e, axis=0, keepdims=True)


def _conv_post(c, conv_b, ln_g, ln_b):
    c = c + conv_b
    mu = jnp.mean(c, axis=-1, keepdims=True)
    d = c - mu
    var = jnp.mean(d * d, axis=-1, keepdims=True)
    return _silu(d * lax.rsqrt(var + EPS) * ln_g + ln_b)


def _head_norm(o, g):
    return o * lax.rsqrt(jnp.mean(o * o, axis=-1, keepdims=True) + EPS) * g


def _route(logits):
    col = lax.broadcasted_iota(jnp.int32, logits.shape, 1)
    big = jnp.int32(1 << 20)
    neg = jnp.float32(-jnp.inf)
    gmask = col < N_GROUPS
    lg = jnp.where(gmask, logits, neg)
    gmax = jnp.max(lg, axis=-1, keepdims=True)
    gsum = jnp.sum(jnp.where(gmask, jnp.exp(lg - gmax), 0.0), axis=-1, keepdims=True)
    gval = 1.0 / gsum
    gidx = jnp.min(jnp.where(lg == gmax, col, big), axis=-1, keepdims=True)
    lo = N_GROUPS + EXPERTS_PER_GROUP * gidx
    emask = (col >= lo) & (col < lo + EXPERTS_PER_GROUP)
    el = jnp.where(emask, logits, neg)
    m1 = jnp.max(el, axis=-1, keepdims=True)
    i1 = jnp.min(jnp.where(el == m1, col, big), axis=-1, keepdims=True)
    el2 = jnp.where(col == i1, neg, el)
    m2 = jnp.max(el2, axis=-1, keepdims=True)
    i2 = jnp.min(jnp.where(el2 == m2, col, big), axis=-1, keepdims=True)
    r = jnp.exp(m2 - m1)
    w1 = gval / (1.0 + r)
    w2 = gval * r / (1.0 + r)
    e1 = (i1 - N_GROUPS).astype(F32)
    e2 = (i2 - N_GROUPS).astype(F32)
    return jnp.where(col == 0, e1, jnp.where(col == 1, e2, jnp.where(col == 2, w1, jnp.where(col == 3, w2, 0.0))))


LEVELS = (NBLK // 2, NBLK // 4, NBLK // 8)
ROW_TOT, ROW_QE, ROW_KD, ROW_LEVEL = 0, NBLK, 2 * NBLK, 3 * NBLK
TABLE_ROWS = ROW_LEVEL + 2 * NBLK * len(LEVELS)


def _row_bcast(ref, row):
    return jnp.broadcast_to(ref[pl.ds(row, 1), :], (SUBLANES, LANES))


def _per_block(ref, base):
    return jnp.concatenate([_row_bcast(ref, base + j) for j in range(NBLK)], axis=0)


def _block_rows(ref, c):
    return jnp.concatenate([_row_bcast(ref, j * SUBLANES + c) for j in range(NBLK)], axis=0)


def _sparse_tile(x, ref, base, blocks):
    groups = []
    for g in range(0, NBLK, 2):
        if g not in blocks and g + 1 not in blocks:
            groups.append(jnp.zeros((2 * SUBLANES, LANES), BF))
            continue
        halves = [x[j * SUBLANES:(j + 1) * SUBLANES, :] * _row_bcast(ref, base + j) if j in blocks
                  else jnp.zeros((SUBLANES, LANES), F32) for j in (g, g + 1)]
        groups.append(jnp.concatenate(halves, axis=0).astype(BF))
    return jnp.concatenate(groups, axis=0)


def _chunk_tables(l2, bl_ref, tab_ref):
    rin = lax.broadcasted_iota(jnp.int32, (CHUNK, LANES), 0) % SUBLANES
    bl = l2
    for s in (1, 2, 4):
        bl = bl + jnp.where(rin >= s, pltpu.roll(bl, s, axis=0), 0.0)
    bl_ref[...] = bl
    tot = bl_ref[pl.ds(SUBLANES - 1, NBLK, stride=SUBLANES), :]
    brow = lax.broadcasted_iota(jnp.int32, (NBLK, LANES), 0)
    rb = tot
    for s in (1, 2, 4):
        rb = rb + jnp.where(brow >= s, pltpu.roll(rb, s, axis=0), 0.0)
    rb_prev = rb - tot
    total = rb[NBLK - 1:NBLK, :]

    tab_ref[pl.ds(ROW_TOT, NBLK), :] = tot
    tab_ref[pl.ds(ROW_QE, NBLK), :] = jnp.exp2(rb_prev)
    tab_ref[pl.ds(ROW_KD, NBLK), :] = jnp.exp2(total - rb)
    for lv, cb in enumerate(LEVELS):
        edges = [(j // (2 * cb)) * (2 * cb) + cb - 1 for j in range(NBLK)]
        rb_edge = jnp.concatenate([rb[e:e + 1, :] for e in edges], axis=0)
        base = ROW_LEVEL + 2 * NBLK * lv
        tab_ref[pl.ds(base, NBLK), :] = jnp.exp2(jnp.minimum(rb_prev - rb_edge, 0.0))
        tab_ref[pl.ds(base + NBLK, NBLK), :] = jnp.exp2(jnp.minimum(rb_edge - rb, 0.0))
    return bl, total


def _chunk_operands(q, k, v, st, bl, row_masks, bl_ref, tab_ref):
    qp = q * jnp.exp2(bl)
    kp = k * jnp.exp2(_per_block(tab_ref, ROW_TOT) - bl)
    qe = (qp * _per_block(tab_ref, ROW_QE)).astype(BF)
    kdec = (kp * _per_block(tab_ref, ROW_KD)).astype(BF)

    q_tiles, k_tiles = [], []
    for lv, cb in enumerate(LEVELS):
        base = ROW_LEVEL + 2 * NBLK * lv
        for p0 in range(0, NBLK, 2 * cb):
            q_tiles.append(_sparse_tile(qp, tab_ref, base, range(p0 + cb, p0 + 2 * cb)))
            k_tiles.append(_sparse_tile(kp, tab_ref, base + NBLK, range(p0, p0 + cb)))

    k_b = k.astype(BF)
    lhs = [(q * jnp.exp2(jnp.minimum(bl - _block_rows(bl_ref, c), 0.0))).astype(BF) for c in range(SUBLANES)]
    rhs = [k_b * row_masks[c] for c in range(SUBLANES)]
    return dict(qe=qe, kdec=kdec, st=st.astype(BF), v=v.astype(BF),
                off_l=jnp.concatenate(q_tiles, axis=1), off_r=jnp.concatenate(k_tiles, axis=1),
                diag_l=jnp.concatenate(lhs, axis=1), diag_r=jnp.concatenate(rhs, axis=1))


def _chunk_products(ops):
    return (_dot_nt(ops['qe'], ops['st']), _dot_tn(ops['v'], ops['kdec']),
            _dot_nt(ops['off_l'], ops['off_r']), _dot_nt(ops['diag_l'], ops['diag_r']))


def _chunk_output(o_inter, off_diag, diag, v_b):
    ti = lax.broadcasted_iota(jnp.int32, (CHUNK, CHUNK), 0)
    si = lax.broadcasted_iota(jnp.int32, (CHUNK, CHUNK), 1)
    scores = jnp.where((ti // SUBLANES == si // SUBLANES) & (si <= ti), diag, off_diag)
    return o_inter + _dot(scores.astype(BF), v_b)


def _mixer_tail(x, y_a, y_b, ga, gb, b_gates, w_out, g_ffn, w_router):
    m = _sigmoid(ga + b_gates[:, :D_MODEL]) * y_a + _sigmoid(gb + b_gates[:, D_MODEL:]) * y_b
    x1 = x + _dot(m.astype(BF), w_out)
    h2 = _rms(x1, g_ffn).astype(BF)
    route = _route(_dot(h2, w_router))
    return x1, h2.astype(F32), route


def _prompt_mixer_kernel(x_ref, gmix_ref, win_ref, bg_ref, cw_ref, cb_ref, lng_ref, lnb_ref, wco_ref,
                         lbp_ref, hng_ref, who_ref, wout_ref, gffn_ref, wr_ref, masks_ref,
                         x1_ref, h2_ref, route_ref, nconv_ref, nhgrn_ref,
                         hist_ref, phase_ref, conv_ref, st_ref, q_s, k_s, v_s, lf_s, o_s, *head_scr, tb):
    t = pl.program_id(1)

    @pl.when(t == 0)
    def _():
        hist_ref[pl.ds(0, HIST_ROWS), :] = jnp.zeros((HIST_ROWS, CONV_DIM), F32)
        st_ref[...] = jnp.zeros_like(st_ref)

    x = x_ref[...]
    h = _rms(x, gmix_ref[...]).astype(BF)

    cv = _dot(h, win_ref[:, OFF_CV:OFF_CV + CONV_DIM])
    cg = _dot(h, win_ref[:, OFF_CG:OFF_CG + CONV_DIM])
    hist_ref[pl.ds(HIST_ROWS, tb), :] = cv * _sigmoid(cg)
    span = tb + SUBLANES * (-(-CONV_WIDTH // SUBLANES) - 1)
    for r in range(SUBLANES):
        n = min(span, HIST_ROWS + tb - HIST_PAD - r)
        phase_ref[r, pl.ds(0, n), :] = hist_ref[pl.ds(HIST_PAD + r, n), :]
    for r0 in range(0, tb, CONV_ROWS):
        acc = jnp.zeros((CONV_ROWS, CONV_DIM), F32)
        for j in range(CONV_WIDTH):
            acc = acc + (phase_ref[j % SUBLANES, pl.ds(r0 + j - j % SUBLANES, CONV_ROWS), :]
                         * cw_ref[pl.ds(j, 1), :])
        conv_ref[pl.ds(r0, CONV_ROWS), :] = acc
    tail = hist_ref[pl.ds(tb, HIST_ROWS), :]
    hist_ref[pl.ds(0, HIST_ROWS), :] = tail
    nconv_ref[0] = tail
    c = _conv_post(conv_ref[...], cb_ref[...], lng_ref[...], lnb_ref[...])
    y_a = _dot(c.astype(BF), wco_ref[...])

    lb = _lower_bound(lbp_ref[...])
    qv = _dot(h, win_ref[:, OFF_Q:OFF_Q + HG_DIM])
    q_s[...] = _silu(qv) * (HG_DK ** -0.5)
    fv = _dot(h, win_ref[:, OFF_F:OFF_F + HG_DIM])
    fg = lb + (1.0 - lb) * _sigmoid(fv)
    lf_s[...] = jnp.log2(fg)
    k_s[...] = 1.0 - fg
    v_s[...] = _dot(h, win_ref[:, OFF_V:OFF_V + HG_DIM])
    row_masks = masks_ref[...]

    def chunk_body(ci, carry):
        rows = pl.ds(pl.multiple_of(ci * CHUNK, CHUNK), CHUNK)
        cols = [slice(hd * HG_DK, (hd + 1) * HG_DK) for hd in range(HG_HEADS)]
        bl_refs, tab_refs = head_scr[:HG_HEADS], head_scr[HG_HEADS:]
        tabs = [_chunk_tables(lf_s[rows, cols[hd]], bl_refs[hd], tab_refs[hd]) for hd in range(HG_HEADS)]
        ops = [_chunk_operands(q_s[rows, cols[hd]], k_s[rows, cols[hd]], v_s[rows, cols[hd]], st_ref[hd],
                               tabs[hd][0], row_masks, bl_refs[hd], tab_refs[hd]) for hd in range(HG_HEADS)]
        prods = [_chunk_products(op) for op in ops]
        for hd in range(HG_HEADS):
            o_inter, update, off_diag, diag = prods[hd]
            st_ref[hd] = st_ref[hd] * jnp.exp2(tabs[hd][1]) + update
            o = _chunk_output(o_inter, off_diag, diag, ops[hd]['v'])
            o_s[rows, cols[hd]] = _head_norm(o, hng_ref[:, cols[hd]])
        return carry

    lax.fori_loop(0, tb // CHUNK, chunk_body, 0)
    for hd in range(HG_HEADS):
        nhgrn_ref[0, hd] = st_ref[hd].T

    og = _dot(h, win_ref[:, OFF_OG:OFF_OG + HG_DIM])
    y_b = _dot((o_s[...] * _silu(og)).astype(BF), who_ref[...])

    ga = _dot(h, win_ref[:, OFF_GA:OFF_GA + D_MODEL])
    gb = _dot(h, win_ref[:, OFF_GB:OFF_GB + D_MODEL])
    x1, h2, route = _mixer_tail(x, y_a, y_b, ga, gb, bg_ref[...], wout_ref[...], gffn_ref[...], wr_ref[...])
    x1_ref[...] = x1
    h2_ref[...] = h2
    route_ref[...] = route


def _const_spec(shape):
    nd = len(shape)
    return pl.BlockSpec(shape, lambda *_: (0,) * nd, pipeline_mode=pl.Buffered(1))


def _row_masks():
    c = jnp.arange(SUBLANES)[:, None, None]
    r = jnp.arange(CHUNK)[None, :, None] % SUBLANES
    return jnp.broadcast_to(r == c, (SUBLANES, CHUNK, LANES)).astype(BF)


def _prompt_mixer(x2, batch, seq, tb, wts):
    nt = seq // tb
    n = batch * seq
    row_spec = lambda w: pl.BlockSpec((tb, w), lambda b, t: (b * nt + t, 0))
    consts = [wts['gmix'], wts['w_in'], wts['b_gates'], wts['conv_w'], wts['conv_b'], wts['ln_g'], wts['ln_b'],
              wts['w_conv_out'], wts['lb_param'], wts['hn_g'], wts['w_hgrn_out'], wts['w_out'], wts['gffn'],
              wts['w_router'], _row_masks()]
    span = tb + SUBLANES * (-(-CONV_WIDTH // SUBLANES) - 1)
    return pl.pallas_call(
        functools.partial(_prompt_mixer_kernel, tb=tb),
        grid=(batch, nt),
        in_specs=[row_spec(D_MODEL)] + [_const_spec(c.shape) for c in consts],
        out_specs=[row_spec(D_MODEL), row_spec(ROW_W), row_spec(ROUTE_LANES),
                   pl.BlockSpec((1, HIST_ROWS, CONV_DIM), lambda b, t: (b, 0, 0)),
                   pl.BlockSpec((1, HG_HEADS, HG_DK, HG_DV), lambda b, t: (b, 0, 0, 0))],
        out_shape=[jax.ShapeDtypeStruct((n, D_MODEL), F32), jax.ShapeDtypeStruct((n, ROW_W), F32),
                   jax.ShapeDtypeStruct((n, ROUTE_LANES), F32),
                   jax.ShapeDtypeStruct((batch, HIST_ROWS, CONV_DIM), F32),
                   jax.ShapeDtypeStruct((batch, HG_HEADS, HG_DK, HG_DV), F32)],
        scratch_shapes=[pltpu.VMEM((HIST_ROWS + tb, CONV_DIM), F32),
                        pltpu.VMEM((SUBLANES, span, CONV_DIM), F32),
                        pltpu.VMEM((tb, CONV_DIM), F32),
                        pltpu.VMEM((HG_HEADS, HG_DV, HG_DK), F32)]
                       + [pltpu.VMEM((tb, HG_DIM), F32)] * 5
                       + [pltpu.VMEM((CHUNK, LANES), F32)] * HG_HEADS
                       + [pltpu.VMEM((TABLE_ROWS, LANES), F32)] * HG_HEADS,
        compiler_params=pltpu.CompilerParams(dimension_semantics=("arbitrary", "arbitrary"),
                                             vmem_limit_bytes=VMEM_LIMIT),
        name="prompt_mixer",
    )(x2, *consts)


def _sample_proj_kernel(x_ref, gmix_ref, win_ref, proj_ref):
    h = _rms(x_ref[...], gmix_ref[...]).astype(BF)
    proj_ref[...] = _dot(h, win_ref[...])


def _sample_proj(xs, wts):
    n = xs.shape[0]
    return pl.pallas_call(
        _sample_proj_kernel,
        grid=(1,),
        in_specs=[_const_spec(xs.shape), _const_spec(wts['gmix'].shape), _const_spec(wts['w_in'].shape)],
        out_specs=pl.BlockSpec((n, IN_COLS), lambda i: (0, 0)),
        out_shape=jax.ShapeDtypeStruct((n, IN_COLS), F32),
        compiler_params=pltpu.CompilerParams(vmem_limit_bytes=VMEM_LIMIT),
        name="sample_proj",
    )(xs, wts['gmix'], wts['w_in'])


def _sample_state_kernel(q_ref, f_ref, v_ref, lbp_ref, hng_ref, s_ref, snew_ref, o_ref, *, bs):
    lb = _lower_bound(lbp_ref[...])
    qf = _silu(q_ref[...]) * (HG_DK ** -0.5)
    fg = lb + (1.0 - lb) * _sigmoid(f_ref[...])
    kf = 1.0 - fg
    v = v_ref[...]
    rows = lax.broadcasted_iota(jnp.int32, (bs, LANES), 0)
    hi = lax.Precision.HIGHEST
    for hd in range(HG_HEADS):
        cs = slice(hd * HG_DK, (hd + 1) * HG_DK)
        o_h = jnp.zeros((bs, HG_DV), F32)
        for r in range(bs):
            sel = rows == r
            f_col = _dot_tn(fg[:, cs], jnp.where(sel, 1.0, 0.0), precision=hi)
            kv = _dot_tn(kf[:, cs], jnp.where(sel, v[:, cs], 0.0), precision=hi)
            s_new = f_col * s_ref[r, hd] + kv
            snew_ref[r, hd] = s_new
            o_h = o_h + jnp.dot(jnp.where(sel, qf[:, cs], 0.0), s_new, preferred_element_type=F32, precision=hi)
        o_ref[:, cs] = _head_norm(o_h, hng_ref[:, cs])


def _sample_state(proj, state, wts, bs):
    n = proj.shape[0]
    col_spec = lambda off: pl.BlockSpec((bs, HG_DIM), lambda i, off=off: (i, off // HG_DIM))
    st_spec = pl.BlockSpec((bs, HG_HEADS, HG_DK, HG_DV), lambda i: (i, 0, 0, 0))
    return pl.pallas_call(
        functools.partial(_sample_state_kernel, bs=bs),
        grid=(n // bs,),
        in_specs=[col_spec(OFF_Q), col_spec(OFF_F), col_spec(OFF_V),
                  _const_spec(wts['lb_param'].shape), _const_spec(wts['hn_g'].shape), st_spec],
        out_specs=[st_spec, pl.BlockSpec((bs, HG_DIM), lambda i: (i, 0))],
        out_shape=[jax.ShapeDtypeStruct(state.shape, F32), jax.ShapeDtypeStruct((n, HG_DIM), F32)],
        compiler_params=pltpu.CompilerParams(dimension_semantics=("arbitrary",), vmem_limit_bytes=VMEM_LIMIT),
        name="sample_state",
    )(proj, proj, proj, wts['lb_param'], wts['hn_g'], state)


def _sample_tail_kernel(x_ref, proj_ref, o_ref, cst_ref, bg_ref, cw_ref, cb_ref, lng_ref, lnb_ref, wco_ref,
                        who_ref, wout_ref, gffn_ref, wr_ref, x1_ref, h2_ref, route_ref, nconv_ref):
    keep = (CONV_WIDTH - 2) * CONV_DIM
    u = proj_ref[:, OFF_CV:OFF_CV + CONV_DIM] * _sigmoid(proj_ref[:, OFF_CG:OFF_CG + CONV_DIM])
    acc = u * cw_ref[pl.ds(CONV_WIDTH - 1, 1), :]
    for j in range(CONV_WIDTH - 1):
        acc = acc + cst_ref[:, j * CONV_DIM:(j + 1) * CONV_DIM] * cw_ref[pl.ds(j, 1), :]
    nconv_ref[:, :keep] = cst_ref[:, CONV_DIM:]
    nconv_ref[:, keep:] = u
    c = _conv_post(acc, cb_ref[...], lng_ref[...], lnb_ref[...])
    y_a = _dot(c.astype(BF), wco_ref[...])
    og = proj_ref[:, OFF_OG:OFF_OG + HG_DIM]
    y_b = _dot((o_ref[...] * _silu(og)).astype(BF), who_ref[...])
    x1, h2, route = _mixer_tail(x_ref[...], y_a, y_b, proj_ref[:, OFF_GA:OFF_GA + D_MODEL],
                                proj_ref[:, OFF_GB:OFF_GB + D_MODEL], bg_ref[...], wout_ref[...],
                                gffn_ref[...], wr_ref[...])
    x1_ref[...] = x1
    h2_ref[...] = h2
    route_ref[...] = route


def _sample_tail(xs, proj, o, conv_state2, wts):
    n = xs.shape[0]
    ins = [xs, proj, o, conv_state2, wts['b_gates'], wts['conv_w'], wts['conv_b'], wts['ln_g'], wts['ln_b'],
           wts['w_conv_out'], wts['w_hgrn_out'], wts['w_out'], wts['gffn'], wts['w_router']]
    full = lambda shape: pl.BlockSpec(shape, lambda i: (0,) * len(shape))
    return pl.pallas_call(
        _sample_tail_kernel,
        grid=(1,),
        in_specs=[_const_spec(a.shape) for a in ins],
        out_specs=[full((n, D_MODEL)), full((n, ROW_W)), full((n, ROUTE_LANES)), full(conv_state2.shape)],
        out_shape=[jax.ShapeDtypeStruct((n, D_MODEL), F32), jax.ShapeDtypeStruct((n, ROW_W), F32),
                   jax.ShapeDtypeStruct((n, ROUTE_LANES), F32), jax.ShapeDtypeStruct(conv_state2.shape, F32)],
        compiler_params=pltpu.CompilerParams(vmem_limit_bytes=VMEM_LIMIT),
        name="sample_tail",
    )(*ins)


def _plan_kernel(slab_p_ref, slab_s_ref, pos_ref, tinfo_ref, e_ref, pre_ref, *, tm, n_tok, max_tiles):
    n_p = slab_p_ref.shape[0]
    nblk = n_tok // LANES
    blk_p = n_p // LANES
    eidx = lax.broadcasted_iota(jnp.int32, (N_EXPERTS, LANES), 0).astype(F32)
    ti = lax.broadcasted_iota(jnp.int32, (LANES, LANES), 0)
    si = lax.broadcasted_iota(jnp.int32, (LANES, LANES), 1)
    before = (ti < si).astype(BF)

    def onehots(b):
        c0 = pl.multiple_of(b * LANES, LANES)
        e1 = e_ref[0:1, pl.ds(c0, LANES)]
        e2 = e_ref[1:2, pl.ds(c0, LANES)]
        return (e1 == eidx).astype(F32), (e2 == eidx).astype(F32), c0

    def transpose_block(slab_ref, b_local, b_global):
        r0 = pl.multiple_of(b_local * LANES, LANES)
        c0 = pl.multiple_of(b_global * LANES, LANES)
        e_ref[:, pl.ds(c0, LANES)] = slab_ref[pl.ds(r0, LANES), :].T[0:SUBLANES, :]

    def load_p(b, c):
        transpose_block(slab_p_ref, b, b)
        return c

    lax.fori_loop(0, blk_p, load_p, 0)

    def load_s(b, c):
        transpose_block(slab_s_ref, b, b + blk_p)
        return c

    lax.fori_loop(0, nblk - blk_p, load_s, 0)

    def count(b, carry):
        h1, h2, c0 = onehots(b)
        h = h1 + h2
        pre_ref[:, pl.ds(c0, LANES)] = _dot(h.astype(BF), before) + carry
        return carry + jnp.sum(h, axis=1, keepdims=True)

    counts = lax.fori_loop(0, nblk, count, jnp.zeros((N_EXPERTS, 1), F32))
    tiles_per = jnp.floor((counts + (tm - 1)) * (1.0 / tm))
    ei = lax.broadcasted_iota(jnp.int32, (N_EXPERTS, N_EXPERTS), 0)
    ej = lax.broadcasted_iota(jnp.int32, (N_EXPERTS, N_EXPERTS), 1)
    upto = (ej <= ei).astype(BF)
    tile_end = _dot(upto, jnp.broadcast_to(tiles_per, (N_EXPERTS, LANES)).astype(BF))[:, 0:1]
    starts = (tile_end - tiles_per) * tm

    def place(b, c):
        h1, h2, c0 = onehots(b)
        dest = pre_ref[:, pl.ds(c0, LANES)] + starts
        pos_ref[0:1, pl.ds(c0, LANES)] = jnp.sum(h1 * dest, axis=0, keepdims=True).astype(jnp.int32)
        pos_ref[1:2, pl.ds(c0, LANES)] = jnp.sum(h2 * dest, axis=0, keepdims=True).astype(jnp.int32)
        return c

    pos_ref[...] = jnp.zeros_like(pos_ref)
    lax.fori_loop(0, nblk, place, 0)

    tile = lax.broadcasted_iota(jnp.int32, (N_EXPERTS, max_tiles), 1).astype(F32)
    t_exp = jnp.sum((tile_end <= tile).astype(F32), axis=0, keepdims=True)
    t_exp = jnp.minimum(t_exp, N_EXPERTS - 1.0).astype(jnp.int32)
    n_used = jnp.broadcast_to(tile_end[N_EXPERTS - 1:N_EXPERTS, :], (1, max_tiles)).astype(jnp.int32)
    row = lax.broadcasted_iota(jnp.int32, (SUBLANES, max_tiles), 0)
    tinfo_ref[...] = jnp.where(row == 0, t_exp, jnp.where(row == 1, n_used, 0))


def _plan(slab_p, slab_s, tm, max_tiles):
    n_tok = slab_p.shape[0] + slab_s.shape[0]
    assert slab_p.shape[0] % LANES == 0 and slab_s.shape[0] % LANES == 0
    vm = pl.BlockSpec(memory_space=pltpu.VMEM)
    return pl.pallas_call(
        functools.partial(_plan_kernel, tm=tm, n_tok=n_tok, max_tiles=max_tiles),
        in_specs=[vm, vm],
        out_specs=[vm, vm],
        out_shape=[jax.ShapeDtypeStruct((SUBLANES, n_tok), jnp.int32),
                   jax.ShapeDtypeStruct((SUBLANES, max_tiles), jnp.int32)],
        scratch_shapes=[pltpu.VMEM((SUBLANES, n_tok), F32), pltpu.VMEM((N_EXPERTS, n_tok), F32)],
        compiler_params=pltpu.CompilerParams(vmem_limit_bytes=VMEM_LIMIT),
        name="route_plan",
    )(slab_p, slab_s)


def _dispatch_kernel(pos_ref, h_ref, xs_in_ref, xs_ref, sem, *, tb, tok0):
    del xs_in_ref
    i = pl.program_id(0)

    def copies(r):
        base = 2 * (tok0 + i * tb + r)
        src = h_ref.at[pl.ds(r, 1), :]
        return (pltpu.make_async_copy(src, xs_ref.at[pl.ds(pos_ref[base], 1), :], sem),
                pltpu.make_async_copy(src, xs_ref.at[pl.ds(pos_ref[base + 1], 1), :], sem))

    def issue(r, c):
        c0, c1 = copies(r)
        c0.start()
        c1.start()
        return c

    def drain(r, c):
        c0, c1 = copies(r)
        c0.wait()
        c1.wait()
        return c

    lax.fori_loop(0, tb, issue, 0, unroll=8)
    lax.fori_loop(0, tb, drain, 0, unroll=8)


def _dispatch(pos_flat, hpk, xs, tb, tok0):
    n = hpk.shape[0]
    grid_spec = pltpu.PrefetchScalarGridSpec(
        num_scalar_prefetch=1,
        grid=(n // tb,),
        in_specs=[pl.BlockSpec((tb, ROW_W), lambda i, p: (i, 0)),
                  pl.BlockSpec(memory_space=pl.ANY)],
        out_specs=pl.BlockSpec(memory_space=pl.ANY),
        scratch_shapes=[pltpu.SemaphoreType.DMA(())])
    return pl.pallas_call(
        functools.partial(_dispatch_kernel, tb=tb, tok0=tok0),
        grid_spec=grid_spec,
        out_shape=jax.ShapeDtypeStruct(xs.shape, xs.dtype),
        input_output_aliases={2: 0},
        compiler_params=pltpu.CompilerParams(dimension_semantics=("arbitrary",), vmem_limit_bytes=VMEM_LIMIT),
        name="dispatch",
    )(pos_flat, hpk, xs)


def _expert_kernel(te_ref, nt_ref, xs_ref, wg_ref, wu_ref, wd_ref, y_ref, wg_b, wu_b, wd_b):
    i = pl.program_id(0)

    @pl.when(i < nt_ref[0])
    def _():
        changed = jnp.logical_or(i == 0, te_ref[i] != te_ref[jnp.maximum(i - 1, 0)])

        @pl.when(changed)
        def _():
            wg_b[...] = wg_ref[0].astype(BF)
            wu_b[...] = wu_ref[0].astype(BF)
            wd_b[...] = wd_ref[0].astype(BF)

        xb = xs_ref[...].astype(BF)
        gate = _dot(xb, wg_b[...])
        up = _dot(xb, wu_b[...])
        y_ref[...] = _dot((_silu(gate) * up).astype(BF), wd_b[...])

    @pl.when(i >= nt_ref[0])
    def _():
        y_ref[...] = jnp.zeros_like(y_ref)


def _experts(xs, tile_expert, n_tiles, wg, wu, wd, tm, max_tiles):
    grid_spec = pltpu.PrefetchScalarGridSpec(
        num_scalar_prefetch=2,
        grid=(max_tiles,),
        in_specs=[pl.BlockSpec((tm, ROW_W), lambda i, te, nt: (i, 0)),
                  pl.BlockSpec((1, D_MODEL, EXPERT_FF), lambda i, te, nt: (te[i], 0, 0)),
                  pl.BlockSpec((1, D_MODEL, EXPERT_FF), lambda i, te, nt: (te[i], 0, 0)),
                  pl.BlockSpec((1, EXPERT_FF, D_MODEL), lambda i, te, nt: (te[i], 0, 0))],
        out_specs=pl.BlockSpec((tm, D_MODEL), lambda i, te, nt: (i, 0)),
        scratch_shapes=[pltpu.VMEM((D_MODEL, EXPERT_FF), BF), pltpu.VMEM((D_MODEL, EXPERT_FF), BF),
                        pltpu.VMEM((EXPERT_FF, D_MODEL), BF)])
    return pl.pallas_call(
        _expert_kernel,
        grid_spec=grid_spec,
        out_shape=jax.ShapeDtypeStruct((max_tiles * tm, D_MODEL), F32),
        compiler_params=pltpu.CompilerParams(dimension_semantics=("arbitrary",), vmem_limit_bytes=VMEM_LIMIT),
        name="experts",
    )(tile_expert, n_tiles, xs, wg, wu, wd)


def _combine_kernel(pos_ref, x1_ref, route_ref, gfin_ref, ys_ref, y_ref, buf0, buf1, sem, *, tb, tok0):
    i = pl.program_id(0)

    def copies(r):
        base = 2 * (tok0 + i * tb + r)
        return (pltpu.make_async_copy(ys_ref.at[pl.ds(pos_ref[base], 1), :], buf0.at[pl.ds(r, 1), :], sem),
                pltpu.make_async_copy(ys_ref.at[pl.ds(pos_ref[base + 1], 1), :], buf1.at[pl.ds(r, 1), :], sem))

    def issue(r, c):
        c0, c1 = copies(r)
        c0.start()
        c1.start()
        return c

    def drain(r, c):
        c0, c1 = copies(r)
        c0.wait()
        c1.wait()
        return c

    lax.fori_loop(0, tb, issue, 0, unroll=8)
    lax.fori_loop(0, tb, drain, 0, unroll=8)
    route = route_ref[...]
    out = x1_ref[...] + (route[:, 2:3] * buf0[...] + route[:, 3:4] * buf1[...])
    y_ref[...] = _rms(out, gfin_ref[...])


def _combine(pos_flat, x1, route, gfin, ys, tb, tok0):
    n = x1.shape[0]
    grid_spec = pltpu.PrefetchScalarGridSpec(
        num_scalar_prefetch=1,
        grid=(n // tb,),
        in_specs=[pl.BlockSpec((tb, D_MODEL), lambda i, p: (i, 0)),
                  pl.BlockSpec((tb, ROUTE_LANES), lambda i, p: (i, 0)),
                  pl.BlockSpec((1, D_MODEL), lambda i, p: (0, 0)),
                  pl.BlockSpec(memory_space=pl.ANY)],
        out_specs=pl.BlockSpec((tb, D_MODEL), lambda i, p: (i, 0)),
        scratch_shapes=[pltpu.VMEM((tb, D_MODEL), F32), pltpu.VMEM((tb, D_MODEL), F32),
                        pltpu.SemaphoreType.DMA(())])
    return pl.pallas_call(
        functools.partial(_combine_kernel, tb=tb, tok0=tok0),
        grid_spec=grid_spec,
        out_shape=jax.ShapeDtypeStruct((n, D_MODEL), F32),
        compiler_params=pltpu.CompilerParams(dimension_semantics=("arbitrary",), vmem_limit_bytes=VMEM_LIMIT),
        name="combine",
    )(pos_flat, x1, route, gfin, ys)


def kernel(x_prompt, x_sample, state_conv, state_hgrn, norm_mix_g, w_in, b_gates, conv_dw_w, conv_dw_b,
           conv_ln_g, conv_ln_b, w_conv_out, hgrn_lb_param, hgrn_norm_g, w_hgrn_out, w_out, norm_ffn_g,
           w_router_group, w_router_expert, w_expert_gate, w_expert_up, w_expert_down, norm_final_g):
    batch, seq, _ = x_prompt.shape
    dec_batch = x_sample.shape[0]
    assert x_sample.shape[1] == 1 and w_in.shape[0] == 1
    tb = min(256, seq)
    tm = 256
    bs = min(8, dec_batch)
    n_p = batch * seq
    n_all = n_p + dec_batch
    tbd = min(1024, n_p)
    tbc = min(512, n_p)
    assert seq % tb == 0 and tb % CHUNK == 0 and dec_batch % bs == 0 and n_p % tbd == 0 and n_p % tbc == 0

    w_router = jnp.concatenate(
        [w_router_group[0], w_router_expert[0],
         jnp.zeros((D_MODEL, ROUTE_LANES - N_GROUPS - N_EXPERTS), F32)], axis=1).astype(BF)
    wts = dict(gmix=norm_mix_g, w_in=w_in[0].astype(BF), b_gates=b_gates, conv_w=conv_dw_w[0], conv_b=conv_dw_b,
               ln_g=conv_ln_g, ln_b=conv_ln_b, w_conv_out=w_conv_out[0].astype(BF), lb_param=hgrn_lb_param,
               hn_g=hgrn_norm_g, w_hgrn_out=w_hgrn_out[0].astype(BF), w_out=w_out[0].astype(BF),
               gffn=norm_ffn_g, w_router=w_router)

    x1_p, hpk_p, route_p, nconv_p, nhgrn_p = _prompt_mixer(x_prompt.reshape(n_p, D_MODEL), batch, seq, tb, wts)

    xs_tok = x_sample.reshape(dec_batch, D_MODEL)
    proj_s = _sample_proj(xs_tok, wts)
    nhgrn_s, o_s = _sample_state(proj_s, state_hgrn[0], wts, bs)
    conv2 = state_conv[0].reshape(dec_batch, (CONV_WIDTH - 1) * CONV_DIM)
    x1_s, hpk_s, route_s, nconv_s = _sample_tail(xs_tok, proj_s, o_s, conv2, wts)

    max_tiles = -(-((2 * n_all) // tm + N_EXPERTS) // SUBLANES) * SUBLANES
    pos, tinfo = _plan(route_p, route_s, tm, max_tiles)
    pos_flat = pos[0:2].T.reshape(-1)
    tile_expert, n_tiles = tinfo[0], tinfo[1, 0:1]

    xs = jnp.zeros((max_tiles * tm, ROW_W), F32)
    xs = _dispatch(pos_flat, hpk_p, xs, tbd, 0)
    xs = _dispatch(pos_flat, hpk_s, xs, dec_batch, n_p)
    ys = _experts(xs, tile_expert, n_tiles, w_expert_gate[0], w_expert_up[0], w_expert_down[0], tm, max_tiles)

    gfin = norm_final_g.reshape(1, D_MODEL)
    y_p = _combine(pos_flat, x1_p, route_p, gfin, ys, tbc, 0)
    y_s = _combine(pos_flat, x1_s, route_s, gfin, ys, dec_batch, n_p)

    return (y_p.reshape(batch, seq, D_MODEL), y_s.reshape(dec_batch, 1, D_MODEL),
            nconv_p[None, :, HIST_PAD:, :], nhgrn_p[None],
            nconv_s.reshape(1, dec_batch, CONV_WIDTH - 1, CONV_DIM), nhgrn_s[None])
```

```python
import functools

import jax
import jax.numpy as jnp
from jax import lax
from jax.experimental import pallas as pl
from jax.experimental.pallas import tpu as pltpu

D_MODEL = 1024
CONV_DIM = D_MODEL // 2
CONV_WIDTH = 31
HG_HEADS = 8
HG_DK = 128
HG_DV = 128
HG_DIM = HG_HEADS * HG_DK
N_GROUPS = 4
EXPERTS_PER_GROUP = 8
N_EXPERTS = N_GROUPS * EXPERTS_PER_GROUP
EXPERT_FF = D_MODEL // 2
EPS = 1e-6
IN_COLS = 2 * CONV_DIM + 4 * HG_DIM + 2 * D_MODEL

OFF_CV, OFF_CG = 0, CONV_DIM
OFF_Q = 2 * CONV_DIM
OFF_F = OFF_Q + HG_DIM
OFF_V = OFF_F + HG_DIM
OFF_OG = OFF_V + HG_DIM
OFF_GA = OFF_OG + HG_DIM
OFF_GB = OFF_GA + D_MODEL

SUBLANES = 8
LANES = 128
HIST_ROWS = 32
HIST_PAD = HIST_ROWS - (CONV_WIDTH - 1)
CHUNK = 64
NBLK = CHUNK // SUBLANES
CONV_ROWS = 32
ROUTE_LANES = 128
ROW_W = D_MODEL
VMEM_LIMIT = 56 * 1024 * 1024

BF = jnp.bfloat16
F32 = jnp.float32


def _dot(a, b):
    return jnp.dot(a, b, preferred_element_type=F32)


def _dot_nt(a, b):
    return lax.dot_general(a, b, (((1,), (1,)), ((), ())), preferred_element_type=F32)


def _dot_tn(a, b, precision=None):
    return lax.dot_general(a, b, (((0,), (0,)), ((), ())), preferred_element_type=F32,
                           precision=precision)


NEG_LOG2E = -1.4426950408889634


def _sigmoid(x):
    return 1.0 / (1.0 + jnp.exp2(x * NEG_LOG2E))


def _silu(x):
    return x * _sigmoid(x)


def _rms(xf, g):
    return xf * lax.rsqrt(jnp.mean(xf * xf, axis=-1, keepdims=True) + EPS) * g


def _lower_bound(lb_param):
    m = jnp.max(lb_param, axis=0, keepdims=True)
    e = jnp.exp(lb_param - m)
    return e[0:1] / jnp.sum(e, axis=0, keepdims=True)


def _conv_post(c, conv_b, ln_g, ln_b):
    c = c + conv_b
    mu = jnp.mean(c, axis=-1, keepdims=True)
    d = c - mu
    var = jnp.mean(d * d, axis=-1, keepdims=True)
    return _silu(d * lax.rsqrt(var + EPS) * ln_g + ln_b)


def _head_norm(o, g):
    return o * lax.rsqrt(jnp.mean(o * o, axis=-1, keepdims=True) + EPS) * g


def _route(logits):
    col = lax.broadcasted_iota(jnp.int32, logits.shape, 1)
    big = jnp.int32(1 << 20)
    neg = jnp.float32(-jnp.inf)
    gmask = col < N_GROUPS
    lg = jnp.where(gmask, logits, neg)
    gmax = jnp.max(lg, axis=-1, keepdims=True)
    gsum = jnp.sum(jnp.where(gmask, jnp.exp(lg - gmax), 0.0), axis=-1, keepdims=True)
    gval = 1.0 / gsum
    gidx = jnp.min(jnp.where(lg == gmax, col, big), axis=-1, keepdims=True)
    lo = N_GROUPS + EXPERTS_PER_GROUP * gidx
    emask = (col >= lo) & (col < lo + EXPERTS_PER_GROUP)
    el = jnp.where(emask, logits, neg)
    m1 = jnp.max(el, axis=-1, keepdims=True)
    i1 = jnp.min(jnp.where(el == m1, col, big), axis=-1, keepdims=True)
    el2 = jnp.where(col == i1, neg, el)
    m2 = jnp.max(el2, axis=-1, keepdims=True)
    i2 = jnp.min(jnp.where(el2 == m2, col, big), axis=-1, keepdims=True)
    r = jnp.exp(m2 - m1)
    w1 = gval / (1.0 + r)
    w2 = gval * r / (1.0 + r)
    e1 = (i1 - N_GROUPS).astype(F32)
    e2 = (i2 - N_GROUPS).astype(F32)
    return jnp.where(col == 0, e1, jnp.where(col == 1, e2, jnp.where(col == 2, w1, jnp.where(col == 3, w2, 0.0))))


LEVELS = (NBLK // 2, NBLK // 4, NBLK // 8)
ROW_TOT, ROW_QE, ROW_KD, ROW_LEVEL = 0, NBLK, 2 * NBLK, 3 * NBLK
TABLE_ROWS = ROW_LEVEL + 2 * NBLK * len(LEVELS)


def _row_bcast(ref, row):
    return jnp.broadcast_to(ref[pl.ds(row, 1), :], (SUBLANES, LANES))


def _per_block(ref, base):
    return jnp.concatenate([_row_bcast(ref, base + j) for j in range(NBLK)], axis=0)


def _block_rows(ref, c):
    return jnp.concatenate([_row_bcast(ref, j * SUBLANES + c) for j in range(NBLK)], axis=0)


def _sparse_tile(x, ref, base, blocks):
    groups = []
    for g in range(0, NBLK, 2):
        if g not in blocks and g + 1 not in blocks:
            groups.append(jnp.zeros((2 * SUBLANES, LANES), BF))
            continue
        halves = [x[j * SUBLANES:(j + 1) * SUBLANES, :] * _row_bcast(ref, base + j) if j in blocks
                  else jnp.zeros((SUBLANES, LANES), F32) for j in (g, g + 1)]
        groups.append(jnp.concatenate(halves, axis=0).astype(BF))
    return jnp.concatenate(groups, axis=0)


def _chunk_tables(l2, bl_ref, tab_ref):
    rin = lax.broadcasted_iota(jnp.int32, (CHUNK, LANES), 0) % SUBLANES
    bl = l2
    for s in (1, 2, 4):
        bl = bl + jnp.where(rin >= s, pltpu.roll(bl, s, axis=0), 0.0)
    bl_ref[...] = bl
    tot = bl_ref[pl.ds(SUBLANES - 1, NBLK, stride=SUBLANES), :]
    brow = lax.broadcasted_iota(jnp.int32, (NBLK, LANES), 0)
    rb = tot
    for s in (1, 2, 4):
        rb = rb + jnp.where(brow >= s, pltpu.roll(rb, s, axis=0), 0.0)
    rb_prev = rb - tot
    total = rb[NBLK - 1:NBLK, :]

    tab_ref[pl.ds(ROW_TOT, NBLK), :] = tot
    tab_ref[pl.ds(ROW_QE, NBLK), :] = jnp.exp2(rb_prev)
    tab_ref[pl.ds(ROW_KD, NBLK), :] = jnp.exp2(total - rb)
    for lv, cb in enumerate(LEVELS):
        edges = [(j // (2 * cb)) * (2 * cb) + cb - 1 for j in range(NBLK)]
        rb_edge = jnp.concatenate([rb[e:e + 1, :] for e in edges], axis=0)
        base = ROW_LEVEL + 2 * NBLK * lv
        tab_ref[pl.ds(base, NBLK), :] = jnp.exp2(jnp.minimum(rb_prev - rb_edge, 0.0))
        tab_ref[pl.ds(base + NBLK, NBLK), :] = jnp.exp2(jnp.minimum(rb_edge - rb, 0.0))
    return bl, total


def _chunk_operands(q, k, v, st, bl, row_masks, bl_ref, tab_ref):
    qp = q * jnp.exp2(bl)
    kp = k * jnp.exp2(_per_block(tab_ref, ROW_TOT) - bl)
    qe = (qp * _per_block(tab_ref, ROW_QE)).astype(BF)
    kdec = (kp * _per_block(tab_ref, ROW_KD)).astype(BF)

    q_tiles, k_tiles = [], []
    for lv, cb in enumerate(LEVELS):
        base = ROW_LEVEL + 2 * NBLK * lv
        for p0 in range(0, NBLK, 2 * cb):
            q_tiles.append(_sparse_tile(qp, tab_ref, base, range(p0 + cb, p0 + 2 * cb)))
            k_tiles.append(_sparse_tile(kp, tab_ref, base + NBLK, range(p0, p0 + cb)))

    k_b = k.astype(BF)
    lhs = [(q * jnp.exp2(jnp.minimum(bl - _block_rows(bl_ref, c), 0.0))).astype(BF) for c in range(SUBLANES)]
    rhs = [k_b * row_masks[c] for c in range(SUBLANES)]
    return dict(qe=qe, kdec=kdec, st=st.astype(BF), v=v.astype(BF),
                off_l=jnp.concatenate(q_tiles, axis=1), off_r=jnp.concatenate(k_tiles, axis=1),
                diag_l=jnp.concatenate(lhs, axis=1), diag_r=jnp.concatenate(rhs, axis=1))


def _chunk_products(ops):
    return (_dot_nt(ops['qe'], ops['st']), _dot_tn(ops['v'], ops['kdec']),
            _dot_nt(ops['off_l'], ops['off_r']), _dot_nt(ops['diag_l'], ops['diag_r']))


def _chunk_output(o_inter, off_diag, diag, v_b):
    ti = lax.broadcasted_iota(jnp.int32, (CHUNK, CHUNK), 0)
    si = lax.broadcasted_iota(jnp.int32, (CHUNK, CHUNK), 1)
    scores = jnp.where((ti // SUBLANES == si // SUBLANES) & (si <= ti), diag, off_diag)
    return o_inter + _dot(scores.astype(BF), v_b)


def _mixer_tail(x, y_a, y_b, ga, gb, b_gates, w_out, g_ffn, w_router):
    m = _sigmoid(ga + b_gates[:, :D_MODEL]) * y_a + _sigmoid(gb + b_gates[:, D_MODEL:]) * y_b
    x1 = x + _dot(m.astype(BF), w_out)
    h2 = _rms(x1, g_ffn).astype(BF)
    route = _route(_dot(h2, w_router))
    return x1, h2.astype(F32), route


def _prompt_mixer_kernel(x_ref, gmix_ref, win_ref, bg_ref, cw_ref, cb_ref, lng_ref, lnb_ref, wco_ref,
                         lbp_ref, hng_ref, who_ref, wout_ref, gffn_ref, wr_ref, masks_ref,
                         x1_ref, h2_ref, route_ref, nconv_ref, nhgrn_ref,
                         hist_ref, phase_ref, conv_ref, st_ref, q_s, k_s, v_s, lf_s, o_s, *head_scr, tb):
    t = pl.program_id(1)

    @pl.when(t == 0)
    def _():
        hist_ref[pl.ds(0, HIST_ROWS), :] = jnp.zeros((HIST_ROWS, CONV_DIM), F32)
        st_ref[...] = jnp.zeros_like(st_ref)

    x = x_ref[...]
    h = _rms(x, gmix_ref[...]).astype(BF)

    cv = _dot(h, win_ref[:, OFF_CV:OFF_CV + CONV_DIM])
    cg = _dot(h, win_ref[:, OFF_CG:OFF_CG + CONV_DIM])
    hist_ref[pl.ds(HIST_ROWS, tb), :] = cv * _sigmoid(cg)
    span = tb + SUBLANES * (-(-CONV_WIDTH // SUBLANES) - 1)
    for r in range(SUBLANES):
        n = min(span, HIST_ROWS + tb - HIST_PAD - r)
        phase_ref[r, pl.ds(0, n), :] = hist_ref[pl.ds(HIST_PAD + r, n), :]
    for r0 in range(0, tb, CONV_ROWS):
        acc = jnp.zeros((CONV_ROWS, CONV_DIM), F32)
        for j in range(CONV_WIDTH):
            acc = acc + (phase_ref[j % SUBLANES, pl.ds(r0 + j - j % SUBLANES, CONV_ROWS), :]
                         * cw_ref[pl.ds(j, 1), :])
        conv_ref[pl.ds(r0, CONV_ROWS), :] = acc
    tail = hist_ref[pl.ds(tb, HIST_ROWS), :]
    hist_ref[pl.ds(0, HIST_ROWS), :] = tail
    nconv_ref[0] = tail
    c = _conv_post(conv_ref[...], cb_ref[...], lng_ref[...], lnb_ref[...])
    y_a = _dot(c.astype(BF), wco_ref[...])

    lb = _lower_bound(lbp_ref[...])
    qv = _dot(h, win_ref[:, OFF_Q:OFF_Q + HG_DIM])
    q_s[...] = _silu(qv) * (HG_DK ** -0.5)
    fv = _dot(h, win_ref[:, OFF_F:OFF_F + HG_DIM])
    fg = lb + (1.0 - lb) * _sigmoid(fv)
    lf_s[...] = jnp.log2(fg)
    k_s[...] = 1.0 - fg
    v_s[...] = _dot(h, win_ref[:, OFF_V:OFF_V + HG_DIM])
    row_masks = masks_ref[...]

    def chunk_body(ci, carry):
        rows = pl.ds(pl.multiple_of(ci * CHUNK, CHUNK), CHUNK)
        cols = [slice(hd * HG_DK, (hd + 1) * HG_DK) for hd in range(HG_HEADS)]
        bl_refs, tab_refs = head_scr[:HG_HEADS], head_scr[HG_HEADS:]
        tabs = [_chunk_tables(lf_s[rows, cols[hd]], bl_refs[hd], tab_refs[hd]) for hd in range(HG_HEADS)]
        ops = [_chunk_operands(q_s[rows, cols[hd]], k_s[rows, cols[hd]], v_s[rows, cols[hd]], st_ref[hd],
                               tabs[hd][0], row_masks, bl_refs[hd], tab_refs[hd]) for hd in range(HG_HEADS)]
        prods = [_chunk_products(op) for op in ops]
        for hd in range(HG_HEADS):
            o_inter, update, off_diag, diag = prods[hd]
            st_ref[hd] = st_ref[hd] * jnp.exp2(tabs[hd][1]) + update
            o = _chunk_output(o_inter, off_diag, diag, ops[hd]['v'])
            o_s[rows, cols[hd]] = _head_norm(o, hng_ref[:, cols[hd]])
        return carry

    lax.fori_loop(0, tb // CHUNK, chunk_body, 0)
    for hd in range(HG_HEADS):
        nhgrn_ref[0, hd] = st_ref[hd].T

    og = _dot(h, win_ref[:, OFF_OG:OFF_OG + HG_DIM])
    y_b = _dot((o_s[...] * _silu(og)).astype(BF), who_ref[...])

    ga = _dot(h, win_ref[:, OFF_GA:OFF_GA + D_MODEL])
    gb = _dot(h, win_ref[:, OFF_GB:OFF_GB + D_MODEL])
    x1, h2, route = _mixer_tail(x, y_a, y_b, ga, gb, bg_ref[...], wout_ref[...], gffn_ref[...], wr_ref[...])
    x1_ref[...] = x1
    h2_ref[...] = h2
    route_ref[...] = route


def _const_spec(shape):
    nd = len(shape)
    return pl.BlockSpec(shape, lambda *_: (0,) * nd, pipeline_mode=pl.Buffered(1))


def _row_masks():
    c = jnp.arange(SUBLANES)[:, None, None]
    r = jnp.arange(CHUNK)[None, :, None] % SUBLANES
    return jnp.broadcast_to(r == c, (SUBLANES, CHUNK, LANES)).astype(BF)


def _prompt_mixer(x2, batch, seq, tb, wts):
    nt = seq // tb
    n = batch * seq
    row_spec = lambda w: pl.BlockSpec((tb, w), lambda b, t: (b * nt + t, 0))
    consts = [wts['gmix'], wts['w_in'], wts['b_gates'], wts['conv_w'], wts['conv_b'], wts['ln_g'], wts['ln_b'],
              wts['w_conv_out'], wts['lb_param'], wts['hn_g'], wts['w_hgrn_out'], wts['w_out'], wts['gffn'],
              wts['w_router'], _row_masks()]
    span = tb + SUBLANES * (-(-CONV_WIDTH // SUBLANES) - 1)
    return pl.pallas_call(
        functools.partial(_prompt_mixer_kernel, tb=tb),
        grid=(batch, nt),
        in_specs=[row_spec(D_MODEL)] + [_const_spec(c.shape) for c in consts],
        out_specs=[row_spec(D_MODEL), row_spec(ROW_W), row_spec(ROUTE_LANES),
                   pl.BlockSpec((1, HIST_ROWS, CONV_DIM), lambda b, t: (b, 0, 0)),
                   pl.BlockSpec((1, HG_HEADS, HG_DK, HG_DV), lambda b, t: (b, 0, 0, 0))],
        out_shape=[jax.ShapeDtypeStruct((n, D_MODEL), F32), jax.ShapeDtypeStruct((n, ROW_W), F32),
                   jax.ShapeDtypeStruct((n, ROUTE_LANES), F32),
                   jax.ShapeDtypeStruct((batch, HIST_ROWS, CONV_DIM), F32),
                   jax.ShapeDtypeStruct((batch, HG_HEADS, HG_DK, HG_DV), F32)],
        scratch_shapes=[pltpu.VMEM((HIST_ROWS + tb, CONV_DIM), F32),
                        pltpu.VMEM((SUBLANES, span, CONV_DIM), F32),
                        pltpu.VMEM((tb, CONV_DIM), F32),
                        pltpu.VMEM((HG_HEADS, HG_DV, HG_DK), F32)]
                       + [pltpu.VMEM((tb, HG_DIM), F32)] * 5
                       + [pltpu.VMEM((CHUNK, LANES), F32)] * HG_HEADS
                       + [pltpu.VMEM((TABLE_ROWS, LANES), F32)] * HG_HEADS,
        compiler_params=pltpu.CompilerParams(dimension_semantics=("arbitrary", "arbitrary"),
                                             vmem_limit_bytes=VMEM_LIMIT),
        name="prompt_mixer",
    )(x2, *consts)


def _sample_proj_kernel(x_ref, gmix_ref, win_ref, proj_ref):
    h = _rms(x_ref[...], gmix_ref[...]).astype(BF)
    proj_ref[...] = _dot(h, win_ref[...])


def _sample_proj(xs, wts):
    n = xs.shape[0]
    return pl.pallas_call(
        _sample_proj_kernel,
        grid=(1,),
        in_specs=[_const_spec(xs.shape), _const_spec(wts['gmix'].shape), _const_spec(wts['w_in'].shape)],
        out_specs=pl.BlockSpec((n, IN_COLS), lambda i: (0, 0)),
        out_shape=jax.ShapeDtypeStruct((n, IN_COLS), F32),
        compiler_params=pltpu.CompilerParams(vmem_limit_bytes=VMEM_LIMIT),
        name="sample_proj",
    )(xs, wts['gmix'], wts['w_in'])


def _split3(x):
    hi = x.astype(BF).astype(F32)
    mid = (x - hi).astype(BF).astype(F32)
    lo = ((x - hi) - mid).astype(BF).astype(F32)
    return hi, mid, lo


def _sample_state_kernel(q_ref, f_ref, v_ref, lbp_ref, hng_ref, lane_masks_ref, s_ref, snew_ref, o_ref, *, bs):
    assert bs == SUBLANES
    lb = _lower_bound(lbp_ref[...])
    qf = _silu(q_ref[...]) * (HG_DK ** -0.5)
    fg = lb + (1.0 - lb) * _sigmoid(f_ref[...])
    kf = 1.0 - fg
    v = v_ref[...]
    ones = jnp.ones((bs, HG_DV), F32)
    zeros = jnp.zeros((bs, HG_DV), F32)
    n_groups = 9
    pad = jnp.zeros((LANES - n_groups * bs, LANES), F32)
    for hd in range(HG_HEADS):
        cs = slice(hd * HG_DK, (hd + 1) * HG_DK)
        f3, k3, q3, v3 = _split3(fg[:, cs]), _split3(kf[:, cs]), _split3(qf[:, cs]), _split3(v[:, cs])
        left = jnp.concatenate([f3[0], f3[1], f3[2], k3[0], k3[1], k3[0], q3[0], q3[1], q3[2], pad], axis=0)
        left_t = left.T.astype(BF)
        right = jnp.concatenate(
            [jnp.concatenate(blk, axis=1) for blk in
             [(ones, zeros, zeros)] * 3 + [(zeros, v3[0], zeros), (zeros, v3[0], zeros), (zeros, v3[1], zeros)]
             + [(zeros, zeros, ones)] * 3] + [jnp.concatenate((pad, pad, pad), axis=1)], axis=0).astype(BF)
        o_rows = []
        for r in range(bs):
            prod = _dot(left_t * lane_masks_ref[r], right)
            s_new = prod[:, :HG_DV] * s_ref[r, hd] + prod[:, HG_DV:2 * HG_DV]
            snew_ref[r, hd] = s_new
            o_rows.append(jnp.sum(prod[:, 2 * HG_DV:] * s_new, axis=0, keepdims=True))
        o_ref[:, cs] = _head_norm(jnp.concatenate(o_rows, axis=0), hng_ref[:, cs])


def _lane_masks():
    r = jnp.arange(SUBLANES)[:, None, None]
    lane = jnp.arange(LANES)[None, None, :] % SUBLANES
    return jnp.broadcast_to(lane == r, (SUBLANES, LANES, LANES)).astype(BF)


def _sample_state(proj, state, wts, bs):
    n = proj.shape[0]
    col_spec = lambda off: pl.BlockSpec((bs, HG_DIM), lambda i, off=off: (i, off // HG_DIM))
    st_spec = pl.BlockSpec((bs, HG_HEADS, HG_DK, HG_DV), lambda i: (i, 0, 0, 0))
    return pl.pallas_call(
        functools.partial(_sample_state_kernel, bs=bs),
        grid=(n // bs,),
        in_specs=[col_spec(OFF_Q), col_spec(OFF_F), col_spec(OFF_V),
                  _const_spec(wts['lb_param'].shape), _const_spec(wts['hn_g'].shape),
                  _const_spec((SUBLANES, LANES, LANES)), st_spec],
        out_specs=[st_spec, pl.BlockSpec((bs, HG_DIM), lambda i: (i, 0))],
        out_shape=[jax.ShapeDtypeStruct(state.shape, F32), jax.ShapeDtypeStruct((n, HG_DIM), F32)],
        compiler_params=pltpu.CompilerParams(dimension_semantics=("arbitrary",), vmem_limit_bytes=VMEM_LIMIT),
        name="sample_state",
    )(proj, proj, proj, wts['lb_param'], wts['hn_g'], _lane_masks(), state)


def _sample_tail_kernel(x_ref, proj_ref, o_ref, cst_ref, bg_ref, cw_ref, cb_ref, lng_ref, lnb_ref, wco_ref,
                        who_ref, wout_ref, gffn_ref, wr_ref, x1_ref, h2_ref, route_ref, nconv_ref):
    keep = (CONV_WIDTH - 2) * CONV_DIM
    u = proj_ref[:, OFF_CV:OFF_CV + CONV_DIM] * _sigmoid(proj_ref[:, OFF_CG:OFF_CG + CONV_DIM])
    acc = u * cw_ref[pl.ds(CONV_WIDTH - 1, 1), :]
    for j in range(CONV_WIDTH - 1):
        acc = acc + cst_ref[:, j * CONV_DIM:(j + 1) * CONV_DIM] * cw_ref[pl.ds(j, 1), :]
    nconv_ref[:, :keep] = cst_ref[:, CONV_DIM:]
    nconv_ref[:, keep:] = u
    c = _conv_post(acc, cb_ref[...], lng_ref[...], lnb_ref[...])
    y_a = _dot(c.astype(BF), wco_ref[...])
    og = proj_ref[:, OFF_OG:OFF_OG + HG_DIM]
    y_b = _dot((o_ref[...] * _silu(og)).astype(BF), who_ref[...])
    x1, h2, route = _mixer_tail(x_ref[...], y_a, y_b, proj_ref[:, OFF_GA:OFF_GA + D_MODEL],
                                proj_ref[:, OFF_GB:OFF_GB + D_MODEL], bg_ref[...], wout_ref[...],
                                gffn_ref[...], wr_ref[...])
    x1_ref[...] = x1
    h2_ref[...] = h2
    route_ref[...] = route


def _sample_tail(xs, proj, o, conv_state2, wts):
    n = xs.shape[0]
    ins = [xs, proj, o, conv_state2, wts['b_gates'], wts['conv_w'], wts['conv_b'], wts['ln_g'], wts['ln_b'],
           wts['w_conv_out'], wts['w_hgrn_out'], wts['w_out'], wts['gffn'], wts['w_router']]
    full = lambda shape: pl.BlockSpec(shape, lambda i: (0,) * len(shape))
    return pl.pallas_call(
        _sample_tail_kernel,
        grid=(1,),
        in_specs=[_const_spec(a.shape) for a in ins],
        out_specs=[full((n, D_MODEL)), full((n, ROW_W)), full((n, ROUTE_LANES)), full(conv_state2.shape)],
        out_shape=[jax.ShapeDtypeStruct((n, D_MODEL), F32), jax.ShapeDtypeStruct((n, ROW_W), F32),
                   jax.ShapeDtypeStruct((n, ROUTE_LANES), F32), jax.ShapeDtypeStruct(conv_state2.shape, F32)],
        compiler_params=pltpu.CompilerParams(vmem_limit_bytes=VMEM_LIMIT),
        name="sample_tail",
    )(*ins)


def _plan_kernel(slab_p_ref, slab_s_ref, pos_ref, tinfo_ref, e_ref, pre_ref, *, tm, n_tok, max_tiles):
    n_p = slab_p_ref.shape[0]
    nblk = n_tok // LANES
    blk_p = n_p // LANES
    eidx = lax.broadcasted_iota(jnp.int32, (N_EXPERTS, LANES), 0).astype(F32)
    ti = lax.broadcasted_iota(jnp.int32, (LANES, LANES), 0)
    si = lax.broadcasted_iota(jnp.int32, (LANES, LANES), 1)
    before = (ti < si).astype(BF)

    def onehots(b):
        c0 = pl.multiple_of(b * LANES, LANES)
        e1 = e_ref[0:1, pl.ds(c0, LANES)]
        e2 = e_ref[1:2, pl.ds(c0, LANES)]
        return (e1 == eidx).astype(F32), (e2 == eidx).astype(F32), c0

    def transpose_block(slab_ref, b_local, b_global):
        r0 = pl.multiple_of(b_local * LANES, LANES)
        c0 = pl.multiple_of(b_global * LANES, LANES)
        e_ref[:, pl.ds(c0, LANES)] = slab_ref[pl.ds(r0, LANES), :].T[0:SUBLANES, :]

    def load_p(b, c):
        transpose_block(slab_p_ref, b, b)
        return c

    lax.fori_loop(0, blk_p, load_p, 0)

    def load_s(b, c):
        transpose_block(slab_s_ref, b, b + blk_p)
        return c

    lax.fori_loop(0, nblk - blk_p, load_s, 0)

    def count(b, carry):
        h1, h2, c0 = onehots(b)
        h = h1 + h2
        pre_ref[:, pl.ds(c0, LANES)] = _dot(h.astype(BF), before) + carry
        return carry + jnp.sum(h, axis=1, keepdims=True)

    counts = lax.fori_loop(0, nblk, count, jnp.zeros((N_EXPERTS, 1), F32))
    tiles_per = jnp.floor((counts + (tm - 1)) * (1.0 / tm))
    ei = lax.broadcasted_iota(jnp.int32, (N_EXPERTS, N_EXPERTS), 0)
    ej = lax.broadcasted_iota(jnp.int32, (N_EXPERTS, N_EXPERTS), 1)
    upto = (ej <= ei).astype(BF)
    tile_end = _dot(upto, jnp.broadcast_to(tiles_per, (N_EXPERTS, LANES)).astype(BF))[:, 0:1]
    starts = (tile_end - tiles_per) * tm

    def place(b, c):
        h1, h2, c0 = onehots(b)
        dest = pre_ref[:, pl.ds(c0, LANES)] + starts
        pos_ref[0:1, pl.ds(c0, LANES)] = jnp.sum(h1 * dest, axis=0, keepdims=True).astype(jnp.int32)
        pos_ref[1:2, pl.ds(c0, LANES)] = jnp.sum(h2 * dest, axis=0, keepdims=True).astype(jnp.int32)
        return c

    pos_ref[...] = jnp.zeros_like(pos_ref)
    lax.fori_loop(0, nblk, place, 0)

    tile = lax.broadcasted_iota(jnp.int32, (N_EXPERTS, max_tiles), 1).astype(F32)
    t_exp = jnp.sum((tile_end <= tile).astype(F32), axis=0, keepdims=True)
    t_exp = jnp.minimum(t_exp, N_EXPERTS - 1.0).astype(jnp.int32)
    n_used = jnp.broadcast_to(tile_end[N_EXPERTS - 1:N_EXPERTS, :], (1, max_tiles)).astype(jnp.int32)
    row = lax.broadcasted_iota(jnp.int32, (SUBLANES, max_tiles), 0)
    tinfo_ref[...] = jnp.where(row == 0, t_exp, jnp.where(row == 1, n_used, 0))


def _plan(slab_p, slab_s, tm, max_tiles):
    n_tok = slab_p.shape[0] + slab_s.shape[0]
    assert slab_p.shape[0] % LANES == 0 and slab_s.shape[0] % LANES == 0
    vm = pl.BlockSpec(memory_space=pltpu.VMEM)
    return pl.pallas_call(
        functools.partial(_plan_kernel, tm=tm, n_tok=n_tok, max_tiles=max_tiles),
        in_specs=[vm, vm],
        out_specs=[vm, vm],
        out_shape=[jax.ShapeDtypeStruct((SUBLANES, n_tok), jnp.int32),
                   jax.ShapeDtypeStruct((SUBLANES, max_tiles), jnp.int32)],
        scratch_shapes=[pltpu.VMEM((SUBLANES, n_tok), F32), pltpu.VMEM((N_EXPERTS, n_tok), F32)],
        compiler_params=pltpu.CompilerParams(vmem_limit_bytes=VMEM_LIMIT),
        name="route_plan",
    )(slab_p, slab_s)


def _dispatch_kernel(pos_ref, h_ref, xs_in_ref, xs_ref, sem, *, tb, tok0):
    del xs_in_ref
    i = pl.program_id(0)

    def copies(r):
        base = 2 * (tok0 + i * tb + r)
        src = h_ref.at[pl.ds(r, 1), :]
        return (pltpu.make_async_copy(src, xs_ref.at[pl.ds(pos_ref[base], 1), :], sem),
                pltpu.make_async_copy(src, xs_ref.at[pl.ds(pos_ref[base + 1], 1), :], sem))

    def issue(r, c):
        c0, c1 = copies(r)
        c0.start()
        c1.start()
        return c

    def drain(r, c):
        c0, c1 = copies(r)
        c0.wait()
        c1.wait()
        return c

    lax.fori_loop(0, tb, issue, 0, unroll=8)
    lax.fori_loop(0, tb, drain, 0, unroll=8)


def _dispatch(pos_flat, hpk, xs, tb, tok0):
    n = hpk.shape[0]
    grid_spec = pltpu.PrefetchScalarGridSpec(
        num_scalar_prefetch=1,
        grid=(n // tb,),
        in_specs=[pl.BlockSpec((tb, ROW_W), lambda i, p: (i, 0)),
                  pl.BlockSpec(memory_space=pl.ANY)],
        out_specs=pl.BlockSpec(memory_space=pl.ANY),
        scratch_shapes=[pltpu.SemaphoreType.DMA(())])
    return pl.pallas_call(
        functools.partial(_dispatch_kernel, tb=tb, tok0=tok0),
        grid_spec=grid_spec,
        out_shape=jax.ShapeDtypeStruct(xs.shape, xs.dtype),
        input_output_aliases={2: 0},
        compiler_params=pltpu.CompilerParams(dimension_semantics=("arbitrary",), vmem_limit_bytes=VMEM_LIMIT),
        name="dispatch",
    )(pos_flat, hpk, xs)


def _expert_kernel(te_ref, nt_ref, xs_ref, wg_ref, wu_ref, wd_ref, y_ref, wg_b, wu_b, wd_b):
    i = pl.program_id(0)

    @pl.when(i < nt_ref[0])
    def _():
        changed = jnp.logical_or(i == 0, te_ref[i] != te_ref[jnp.maximum(i - 1, 0)])

        @pl.when(changed)
        def _():
            wg_b[...] = wg_ref[0].astype(BF)
            wu_b[...] = wu_ref[0].astype(BF)
            wd_b[...] = wd_ref[0].astype(BF)

        xb = xs_ref[...].astype(BF)
        gate = _dot(xb, wg_b[...])
        up = _dot(xb, wu_b[...])
        y_ref[...] = _dot((_silu(gate) * up).astype(BF), wd_b[...])

    @pl.when(i >= nt_ref[0])
    def _():
        y_ref[...] = jnp.zeros_like(y_ref)


def _experts(xs, tile_expert, n_tiles, wg, wu, wd, tm, max_tiles):
    grid_spec = pltpu.PrefetchScalarGridSpec(
        num_scalar_prefetch=2,
        grid=(max_tiles,),
        in_specs=[pl.BlockSpec((tm, ROW_W), lambda i, te, nt: (i, 0)),
                  pl.BlockSpec((1, D_MODEL, EXPERT_FF), lambda i, te, nt: (te[i], 0, 0)),
                  pl.BlockSpec((1, D_MODEL, EXPERT_FF), lambda i, te, nt: (te[i], 0, 0)),
                  pl.BlockSpec((1, EXPERT_FF, D_MODEL), lambda i, te, nt: (te[i], 0, 0))],
        out_specs=pl.BlockSpec((tm, D_MODEL), lambda i, te, nt: (i, 0)),
        scratch_shapes=[pltpu.VMEM((D_MODEL, EXPERT_FF), BF), pltpu.VMEM((D_MODEL, EXPERT_FF), BF),
                        pltpu.VMEM((EXPERT_FF, D_MODEL), BF)])
    return pl.pallas_call(
        _expert_kernel,
        grid_spec=grid_spec,
        out_shape=jax.ShapeDtypeStruct((max_tiles * tm, D_MODEL), F32),
        compiler_params=pltpu.CompilerParams(dimension_semantics=("arbitrary",), vmem_limit_bytes=VMEM_LIMIT),
        name="experts",
    )(tile_expert, n_tiles, xs, wg, wu, wd)


def _combine_kernel(pos_ref, x1_ref, route_ref, gfin_ref, ys_ref, y_ref, buf0, buf1, sem, *, tb, tok0):
    i = pl.program_id(0)

    def copies(r):
        base = 2 * (tok0 + i * tb + r)
        return (pltpu.make_async_copy(ys_ref.at[pl.ds(pos_ref[base], 1), :], buf0.at[pl.ds(r, 1), :], sem),
                pltpu.make_async_copy(ys_ref.at[pl.ds(pos_ref[base + 1], 1), :], buf1.at[pl.ds(r, 1), :], sem))

    def issue(r, c):
        c0, c1 = copies(r)
        c0.start()
        c1.start()
        return c

    def drain(r, c):
        c0, c1 = copies(r)
        c0.wait()
        c1.wait()
        return c

    lax.fori_loop(0, tb, issue, 0, unroll=8)
    lax.fori_loop(0, tb, drain, 0, unroll=8)
    route = route_ref[...]
    out = x1_ref[...] + (route[:, 2:3] * buf0[...] + route[:, 3:4] * buf1[...])
    y_ref[...] = _rms(out, gfin_ref[...])


def _combine(pos_flat, x1, route, gfin, ys, tb, tok0):
    n = x1.shape[0]
    grid_spec = pltpu.PrefetchScalarGridSpec(
        num_scalar_prefetch=1,
        grid=(n // tb,),
        in_specs=[pl.BlockSpec((tb, D_MODEL), lambda i, p: (i, 0)),
                  pl.BlockSpec((tb, ROUTE_LANES), lambda i, p: (i, 0)),
                  pl.BlockSpec((1, D_MODEL), lambda i, p: (0, 0)),
                  pl.BlockSpec(memory_space=pl.ANY)],
        out_specs=pl.BlockSpec((tb, D_MODEL), lambda i, p: (i, 0)),
        scratch_shapes=[pltpu.VMEM((tb, D_MODEL), F32), pltpu.VMEM((tb, D_MODEL), F32),
                        pltpu.SemaphoreType.DMA(())])
    return pl.pallas_call(
        functools.partial(_combine_kernel, tb=tb, tok0=tok0),
        grid_spec=grid_spec,
        out_shape=jax.ShapeDtypeStruct((n, D_MODEL), F32),
        compiler_params=pltpu.CompilerParams(dimension_semantics=("arbitrary",), vmem_limit_bytes=VMEM_LIMIT),
        name="combine",
    )(pos_flat, x1, route, gfin, ys)


def kernel(x_prompt, x_sample, state_conv, state_hgrn, norm_mix_g, w_in, b_gates, conv_dw_w, conv_dw_b,
           conv_ln_g, conv_ln_b, w_conv_out, hgrn_lb_param, hgrn_norm_g, w_hgrn_out, w_out, norm_ffn_g,
           w_router_group, w_router_expert, w_expert_gate, w_expert_up, w_expert_down, norm_final_g):
    batch, seq, _ = x_prompt.shape
    dec_batch = x_sample.shape[0]
    assert x_sample.shape[1] == 1 and w_in.shape[0] == 1
    tb = min(256, seq)
    tm = 256
    bs = min(8, dec_batch)
    n_p = batch * seq
    n_all = n_p + dec_batch
    tbd = min(1024, n_p)
    tbc = min(512, n_p)
    assert seq % tb == 0 and tb % CHUNK == 0 and dec_batch % bs == 0 and n_p % tbd == 0 and n_p % tbc == 0

    w_router = jnp.concatenate(
        [w_router_group[0], w_router_expert[0],
         jnp.zeros((D_MODEL, ROUTE_LANES - N_GROUPS - N_EXPERTS), F32)], axis=1).astype(BF)
    wts = dict(gmix=norm_mix_g, w_in=w_in[0].astype(BF), b_gates=b_gates, conv_w=conv_dw_w[0], conv_b=conv_dw_b,
               ln_g=conv_ln_g, ln_b=conv_ln_b, w_conv_out=w_conv_out[0].astype(BF), lb_param=hgrn_lb_param,
               hn_g=hgrn_norm_g, w_hgrn_out=w_hgrn_out[0].astype(BF), w_out=w_out[0].astype(BF),
               gffn=norm_ffn_g, w_router=w_router)

    x1_p, hpk_p, route_p, nconv_p, nhgrn_p = _prompt_mixer(x_prompt.reshape(n_p, D_MODEL), batch, seq, tb, wts)

    xs_tok = x_sample.reshape(dec_batch, D_MODEL)
    proj_s = _sample_proj(xs_tok, wts)
    nhgrn_s, o_s = _sample_state(proj_s, state_hgrn[0], wts, bs)
    conv2 = state_conv[0].reshape(dec_batch, (CONV_WIDTH - 1) * CONV_DIM)
    x1_s, hpk_s, route_s, nconv_s = _sample_tail(xs_tok, proj_s, o_s, conv2, wts)

    max_tiles = -(-((2 * n_all) // tm + N_EXPERTS) // SUBLANES) * SUBLANES
    pos, tinfo = _plan(route_p, route_s, tm, max_tiles)
    pos_flat = pos[0:2].T.reshape(-1)
    tile_expert, n_tiles = tinfo[0], tinfo[1, 0:1]

    xs = jnp.zeros((max_tiles * tm, ROW_W), F32)
    xs = _dispatch(pos_flat, hpk_p, xs, tbd, 0)
    xs = _dispatch(pos_flat, hpk_s, xs, dec_batch, n_p)
    ys = _experts(xs, tile_expert, n_tiles, w_expert_gate[0], w_expert_up[0], w_expert_down[0], tm, max_tiles)

    gfin = norm_final_g.reshape(1, D_MODEL)
    y_p = _combine(pos_flat, x1_p, route_p, gfin, ys, tbc, 0)
    y_s = _combine(pos_flat, x1_s, route_s, gfin, ys, dec_batch, n_p)

    return (y_p.reshape(batch, seq, D_MODEL), y_s.reshape(dec_batch, 1, D_MODEL),
            nconv_p[None, :, HIST_PAD:, :], nhgrn_p[None],
            nconv_s.reshape(1, dec_batch, CONV_WIDTH - 1, CONV_DIM), nhgrn_s[None])
```

```python
import functools

import jax
import jax.numpy as jnp
from jax import lax
from jax.experimental import pallas as pl
from jax.experimental.pallas import tpu as pltpu

D_MODEL = 1024
CONV_DIM = D_MODEL // 2
CONV_WIDTH = 31
HG_HEADS = 8
HG_DK = 128
HG_DV = 128
HG_DIM = HG_HEADS * HG_DK
N_GROUPS = 4
EXPERTS_PER_GROUP = 8
N_EXPERTS = N_GROUPS * EXPERTS_PER_GROUP
EXPERT_FF = D_MODEL // 2
EPS = 1e-6
IN_COLS = 2 * CONV_DIM + 4 * HG_DIM + 2 * D_MODEL

OFF_CV, OFF_CG = 0, CONV_DIM
OFF_Q = 2 * CONV_DIM
OFF_F = OFF_Q + HG_DIM
OFF_V = OFF_F + HG_DIM
OFF_OG = OFF_V + HG_DIM
OFF_GA = OFF_OG + HG_DIM
OFF_GB = OFF_GA + D_MODEL

SUBLANES = 8
LANES = 128
HIST_ROWS = 32
HIST_PAD = HIST_ROWS - (CONV_WIDTH - 1)
CHUNK = 64
NBLK = CHUNK // SUBLANES
CONV_ROWS = 32
PROJ_COLS = 256
ROUTE_LANES = 128
ROW_W = D_MODEL
VMEM_LIMIT = 56 * 1024 * 1024

BF = jnp.bfloat16
F32 = jnp.float32


def _dot(a, b):
    return jnp.dot(a, b, preferred_element_type=F32)


def _dot_nt(a, b):
    return lax.dot_general(a, b, (((1,), (1,)), ((), ())), preferred_element_type=F32)


def _dot_tn(a, b, precision=None):
    return lax.dot_general(a, b, (((0,), (0,)), ((), ())), preferred_element_type=F32,
                           precision=precision)


NEG_LOG2E = -1.4426950408889634


def _sigmoid(x):
    return 1.0 / (1.0 + jnp.exp2(x * NEG_LOG2E))


def _silu(x):
    return x * _sigmoid(x)


def _rms(xf, g):
    return xf * lax.rsqrt(jnp.mean(xf * xf, axis=-1, keepdims=True) + EPS) * g


def _lower_bound(lb_param):
    m = jnp.max(lb_param, axis=0, keepdims=True)
    e = jnp.exp(lb_param - m)
    return e[0:1] / jnp.sum(e, axis=0, keepdims=True)


def _conv_post(c, conv_b, ln_g, ln_b):
    c = c + conv_b
    mu = jnp.mean(c, axis=-1, keepdims=True)
    d = c - mu
    var = jnp.mean(d * d, axis=-1, keepdims=True)
    return _silu(d * lax.rsqrt(var + EPS) * ln_g + ln_b)


def _head_norm(o, g):
    return o * lax.rsqrt(jnp.mean(o * o, axis=-1, keepdims=True) + EPS) * g


def _route(logits):
    col = lax.broadcasted_iota(jnp.int32, logits.shape, 1)
    big = jnp.int32(1 << 20)
    neg = jnp.float32(-jnp.inf)
    gmask = col < N_GROUPS
    lg = jnp.where(gmask, logits, neg)
    gmax = jnp.max(lg, axis=-1, keepdims=True)
    gsum = jnp.sum(jnp.where(gmask, jnp.exp(lg - gmax), 0.0), axis=-1, keepdims=True)
    gval = 1.0 / gsum
    gidx = jnp.min(jnp.where(lg == gmax, col, big), axis=-1, keepdims=True)
    lo = N_GROUPS + EXPERTS_PER_GROUP * gidx
    emask = (col >= lo) & (col < lo + EXPERTS_PER_GROUP)
    el = jnp.where(emask, logits, neg)
    m1 = jnp.max(el, axis=-1, keepdims=True)
    i1 = jnp.min(jnp.where(el == m1, col, big), axis=-1, keepdims=True)
    el2 = jnp.where(col == i1, neg, el)
    m2 = jnp.max(el2, axis=-1, keepdims=True)
    i2 = jnp.min(jnp.where(el2 == m2, col, big), axis=-1, keepdims=True)
    r = jnp.exp(m2 - m1)
    w1 = gval / (1.0 + r)
    w2 = gval * r / (1.0 + r)
    e1 = (i1 - N_GROUPS).astype(F32)
    e2 = (i2 - N_GROUPS).astype(F32)
    return jnp.where(col == 0, e1, jnp.where(col == 1, e2, jnp.where(col == 2, w1, jnp.where(col == 3, w2, 0.0))))


LEVELS = (NBLK // 2, NBLK // 4, NBLK // 8)
ROW_TOT, ROW_QE, ROW_KD, ROW_LEVEL = 0, NBLK, 2 * NBLK, 3 * NBLK
TABLE_ROWS = ROW_LEVEL + 2 * NBLK * len(LEVELS)


def _row_bcast(ref, row):
    return jnp.broadcast_to(ref[pl.ds(row, 1), :], (SUBLANES, LANES))


def _per_block(ref, base):
    return jnp.concatenate([_row_bcast(ref, base + j) for j in range(NBLK)], axis=0)


def _block_rows(ref, c):
    return jnp.concatenate([_row_bcast(ref, j * SUBLANES + c) for j in range(NBLK)], axis=0)


def _sparse_tile(x, ref, base, blocks):
    groups = []
    for g in range(0, NBLK, 2):
        if g not in blocks and g + 1 not in blocks:
            groups.append(jnp.zeros((2 * SUBLANES, LANES), BF))
            continue
        halves = [x[j * SUBLANES:(j + 1) * SUBLANES, :] * _row_bcast(ref, base + j) if j in blocks
                  else jnp.zeros((SUBLANES, LANES), F32) for j in (g, g + 1)]
        groups.append(jnp.concatenate(halves, axis=0).astype(BF))
    return jnp.concatenate(groups, axis=0)


def _chunk_tables(l2, bl_ref, tab_ref):
    rin = lax.broadcasted_iota(jnp.int32, (CHUNK, LANES), 0) % SUBLANES
    bl = l2
    for s in (1, 2, 4):
        bl = bl + jnp.where(rin >= s, pltpu.roll(bl, s, axis=0), 0.0)
    bl_ref[...] = bl
    tot = bl_ref[pl.ds(SUBLANES - 1, NBLK, stride=SUBLANES), :]
    brow = lax.broadcasted_iota(jnp.int32, (NBLK, LANES), 0)
    rb = tot
    for s in (1, 2, 4):
        rb = rb + jnp.where(brow >= s, pltpu.roll(rb, s, axis=0), 0.0)
    rb_prev = rb - tot
    total = rb[NBLK - 1:NBLK, :]

    tab_ref[pl.ds(ROW_TOT, NBLK), :] = tot
    tab_ref[pl.ds(ROW_QE, NBLK), :] = jnp.exp2(rb_prev)
    tab_ref[pl.ds(ROW_KD, NBLK), :] = jnp.exp2(total - rb)
    for lv, cb in enumerate(LEVELS):
        edges = [(j // (2 * cb)) * (2 * cb) + cb - 1 for j in range(NBLK)]
        rb_edge = jnp.concatenate([rb[e:e + 1, :] for e in edges], axis=0)
        base = ROW_LEVEL + 2 * NBLK * lv
        tab_ref[pl.ds(base, NBLK), :] = jnp.exp2(jnp.minimum(rb_prev - rb_edge, 0.0))
        tab_ref[pl.ds(base + NBLK, NBLK), :] = jnp.exp2(jnp.minimum(rb_edge - rb, 0.0))
    return bl, total


def _chunk_operands(q, k, v, st, bl, row_masks, bl_ref, tab_ref):
    qp = q * jnp.exp2(bl)
    kp = k * jnp.exp2(_per_block(tab_ref, ROW_TOT) - bl)
    qe = (qp * _per_block(tab_ref, ROW_QE)).astype(BF)
    kdec = (kp * _per_block(tab_ref, ROW_KD)).astype(BF)

    q_tiles, k_tiles = [], []
    for lv, cb in enumerate(LEVELS):
        base = ROW_LEVEL + 2 * NBLK * lv
        for p0 in range(0, NBLK, 2 * cb):
            q_tiles.append(_sparse_tile(qp, tab_ref, base, range(p0 + cb, p0 + 2 * cb)))
            k_tiles.append(_sparse_tile(kp, tab_ref, base + NBLK, range(p0, p0 + cb)))

    k_b = k.astype(BF)
    lhs = [(q * jnp.exp2(jnp.minimum(bl - _block_rows(bl_ref, c), 0.0))).astype(BF) for c in range(SUBLANES)]
    rhs = [k_b * row_masks[c] for c in range(SUBLANES)]
    return dict(qe=qe, kdec=kdec, st=st.astype(BF), v=v.astype(BF),
                off_l=jnp.concatenate(q_tiles, axis=1), off_r=jnp.concatenate(k_tiles, axis=1),
                diag_l=jnp.concatenate(lhs, axis=1), diag_r=jnp.concatenate(rhs, axis=1))


def _chunk_products(ops):
    return (_dot_nt(ops['qe'], ops['st']), _dot_tn(ops['v'], ops['kdec']),
            _dot_nt(ops['off_l'], ops['off_r']), _dot_nt(ops['diag_l'], ops['diag_r']))


def _chunk_output(o_inter, off_diag, diag, v_b):
    ti = lax.broadcasted_iota(jnp.int32, (CHUNK, CHUNK), 0)
    si = lax.broadcasted_iota(jnp.int32, (CHUNK, CHUNK), 1)
    scores = jnp.where((ti // SUBLANES == si // SUBLANES) & (si <= ti), diag, off_diag)
    return o_inter + _dot(scores.astype(BF), v_b)


def _mixer_tail(x, m, w_out, g_ffn, w_router):
    x1 = x + _dot(m.astype(BF), w_out)
    h2 = _rms(x1, g_ffn).astype(BF)
    route = _route(_dot(h2, w_router))
    return x1, h2.astype(F32), route


def _prompt_mixer_kernel(x_ref, gmix_ref, win_ref, bg_ref, cw_ref, cb_ref, lng_ref, lnb_ref, wco_ref,
                         lbp_ref, hng_ref, who_ref, wout_ref, gffn_ref, wr_ref, masks_ref,
                         x1_ref, h2_ref, route_ref, nconv_ref, nhgrn_ref,
                         hist_ref, phase_ref, conv_ref, st_ref, q_s, k_s, v_s, lf_s, o_s, og_s, ga_s, gb_s, *head_scr, tb):
    t = pl.program_id(1)

    @pl.when(t == 0)
    def _():
        hist_ref[pl.ds(0, HIST_ROWS), :] = jnp.zeros((HIST_ROWS, CONV_DIM), F32)
        st_ref[...] = jnp.zeros_like(st_ref)

    x = x_ref[...]
    h = _rms(x, gmix_ref[...]).astype(BF)

    cv = _dot(h, win_ref[:, OFF_CV:OFF_CV + CONV_DIM])
    cg = _dot(h, win_ref[:, OFF_CG:OFF_CG + CONV_DIM])
    hist_ref[pl.ds(HIST_ROWS, tb), :] = cv * _sigmoid(cg)

    lb = _lower_bound(lbp_ref[...])

    def project(seg, c0):
        z = _dot(h, win_ref[:, seg + c0:seg + c0 + PROJ_COLS])
        cols = slice(c0, c0 + PROJ_COLS)
        if seg == OFF_Q:
            q_s[:, cols] = _silu(z) * (HG_DK ** -0.5)
        elif seg == OFF_F:
            fg = lb[:, cols] + (1.0 - lb[:, cols]) * _sigmoid(z)
            lf_s[:, cols] = jnp.log2(fg)
            k_s[:, cols] = 1.0 - fg
        elif seg == OFF_V:
            v_s[:, cols] = z
        elif seg == OFF_OG:
            og_s[:, cols] = _silu(z)
        elif seg == OFF_GA:
            ga_s[:, cols] = _sigmoid(z + bg_ref[:, c0:c0 + PROJ_COLS])
        else:
            gb_s[:, cols] = _sigmoid(z + bg_ref[:, D_MODEL + c0:D_MODEL + c0 + PROJ_COLS])

    proj_jobs = [functools.partial(project, seg, c0)
                 for seg in (OFF_Q, OFF_F, OFF_V, OFF_OG, OFF_GA, OFF_GB) for c0 in range(0, HG_DIM, PROJ_COLS)]

    span = tb + SUBLANES * (-(-CONV_WIDTH // SUBLANES) - 1)

    def phase_copy(r):
        n = min(span, HIST_ROWS + tb - HIST_PAD - r)
        phase_ref[r, pl.ds(0, n), :] = hist_ref[pl.ds(HIST_PAD + r, n), :]

    def conv_rows(r0):
        acc = jnp.zeros((CONV_ROWS, CONV_DIM), F32)
        for j in range(CONV_WIDTH):
            w_j = jnp.broadcast_to(cw_ref[pl.ds(j, 1), :], (SUBLANES, CONV_DIM))
            acc = acc + (phase_ref[j % SUBLANES, pl.ds(r0 + j - j % SUBLANES, CONV_ROWS), :]
                         * jnp.concatenate([w_j] * (CONV_ROWS // SUBLANES), axis=0))
        conv_ref[pl.ds(r0, CONV_ROWS), :] = acc

    conv_jobs = ([functools.partial(phase_copy, r) for r in range(SUBLANES)]
                 + [functools.partial(conv_rows, r0) for r0 in range(0, tb, CONV_ROWS)])
    per_conv_job = -(-len(proj_jobs) // len(conv_jobs))
    for i, job in enumerate(conv_jobs):
        job()
        for pj in proj_jobs[i * per_conv_job:(i + 1) * per_conv_job]:
            pj()
    for pj in proj_jobs[len(conv_jobs) * per_conv_job:]:
        pj()

    tail = hist_ref[pl.ds(tb, HIST_ROWS), :]
    hist_ref[pl.ds(0, HIST_ROWS), :] = tail
    nconv_ref[0] = tail
    c = _conv_post(conv_ref[...], cb_ref[...], lng_ref[...], lnb_ref[...])
    y_a = _dot(c.astype(BF), wco_ref[...])

    row_masks = masks_ref[...]
    cols = [slice(hd * HG_DK, (hd + 1) * HG_DK) for hd in range(HG_HEADS)]
    bl_refs, tab_refs = head_scr[:HG_HEADS], head_scr[HG_HEADS:]
    for r0 in range(0, tb, CHUNK):
        rows = pl.ds(r0, CHUNK)
        tabs = [_chunk_tables(lf_s[rows, cols[hd]], bl_refs[hd], tab_refs[hd]) for hd in range(HG_HEADS)]
        ops = [_chunk_operands(q_s[rows, cols[hd]], k_s[rows, cols[hd]], v_s[rows, cols[hd]], st_ref[hd],
                               tabs[hd][0], row_masks, bl_refs[hd], tab_refs[hd]) for hd in range(HG_HEADS)]
        prods = [_chunk_products(op) for op in ops]
        for hd in range(HG_HEADS):
            o_inter, update, off_diag, diag = prods[hd]
            st_ref[hd] = st_ref[hd] * jnp.exp2(tabs[hd][1]) + update
            o = _chunk_output(o_inter, off_diag, diag, ops[hd]['v'])
            o_s[rows, cols[hd]] = _head_norm(o, hng_ref[:, cols[hd]])
    for hd in range(HG_HEADS):
        nhgrn_ref[0, hd] = st_ref[hd].T

    y_b = _dot((o_s[...] * og_s[...]).astype(BF), who_ref[...])
    x1, h2, route = _mixer_tail(x, ga_s[...] * y_a + gb_s[...] * y_b, wout_ref[...], gffn_ref[...], wr_ref[...])
    x1_ref[...] = x1
    h2_ref[...] = h2
    route_ref[...] = route


def _const_spec(shape):
    nd = len(shape)
    return pl.BlockSpec(shape, lambda *_: (0,) * nd, pipeline_mode=pl.Buffered(1))


def _row_masks():
    c = jnp.arange(SUBLANES)[:, None, None]
    r = jnp.arange(CHUNK)[None, :, None] % SUBLANES
    return jnp.broadcast_to(r == c, (SUBLANES, CHUNK, LANES)).astype(BF)


def _prompt_mixer(x2, batch, seq, tb, wts):
    nt = seq // tb
    n = batch * seq
    row_spec = lambda w: pl.BlockSpec((tb, w), lambda b, t: (b * nt + t, 0))
    consts = [wts['gmix'], wts['w_in'], wts['b_gates'], wts['conv_w'], wts['conv_b'], wts['ln_g'], wts['ln_b'],
              wts['w_conv_out'], wts['lb_param'], wts['hn_g'], wts['w_hgrn_out'], wts['w_out'], wts['gffn'],
              wts['w_router'], _row_masks()]
    span = tb + SUBLANES * (-(-CONV_WIDTH // SUBLANES) - 1)
    return pl.pallas_call(
        functools.partial(_prompt_mixer_kernel, tb=tb),
        grid=(batch, nt),
        in_specs=[row_spec(D_MODEL)] + [_const_spec(c.shape) for c in consts],
        out_specs=[row_spec(D_MODEL), row_spec(ROW_W), row_spec(ROUTE_LANES),
                   pl.BlockSpec((1, HIST_ROWS, CONV_DIM), lambda b, t: (b, 0, 0)),
                   pl.BlockSpec((1, HG_HEADS, HG_DK, HG_DV), lambda b, t: (b, 0, 0, 0))],
        out_shape=[jax.ShapeDtypeStruct((n, D_MODEL), F32), jax.ShapeDtypeStruct((n, ROW_W), F32),
                   jax.ShapeDtypeStruct((n, ROUTE_LANES), F32),
                   jax.ShapeDtypeStruct((batch, HIST_ROWS, CONV_DIM), F32),
                   jax.ShapeDtypeStruct((batch, HG_HEADS, HG_DK, HG_DV), F32)],
        scratch_shapes=[pltpu.VMEM((HIST_ROWS + tb, CONV_DIM), F32),
                        pltpu.VMEM((SUBLANES, span, CONV_DIM), F32),
                        pltpu.VMEM((tb, CONV_DIM), F32),
                        pltpu.VMEM((HG_HEADS, HG_DV, HG_DK), F32)]
                       + [pltpu.VMEM((tb, HG_DIM), F32)] * 8
                       + [pltpu.VMEM((CHUNK, LANES), F32)] * HG_HEADS
                       + [pltpu.VMEM((TABLE_ROWS, LANES), F32)] * HG_HEADS,
        compiler_params=pltpu.CompilerParams(dimension_semantics=("arbitrary", "arbitrary"),
                                             vmem_limit_bytes=VMEM_LIMIT),
        name="prompt_mixer",
    )(x2, *consts)


def _sample_proj_kernel(x_ref, gmix_ref, win_ref, proj_ref):
    h = _rms(x_ref[...], gmix_ref[...]).astype(BF)
    proj_ref[...] = _dot(h, win_ref[...])


def _sample_proj(xs, wts):
    n = xs.shape[0]
    return pl.pallas_call(
        _sample_proj_kernel,
        grid=(1,),
        in_specs=[_const_spec(xs.shape), _const_spec(wts['gmix'].shape), _const_spec(wts['w_in'].shape)],
        out_specs=pl.BlockSpec((n, IN_COLS), lambda i: (0, 0)),
        out_shape=jax.ShapeDtypeStruct((n, IN_COLS), F32),
        compiler_params=pltpu.CompilerParams(vmem_limit_bytes=VMEM_LIMIT),
        name="sample_proj",
    )(xs, wts['gmix'], wts['w_in'])


def _split3(x):
    hi = x.astype(BF).astype(F32)
    mid = (x - hi).astype(BF).astype(F32)
    lo = ((x - hi) - mid).astype(BF).astype(F32)
    return hi, mid, lo


def _sample_state_kernel(q_ref, f_ref, v_ref, lbp_ref, hng_ref, lane_masks_ref, s_ref, snew_ref, o_ref, *, bs):
    assert bs == SUBLANES
    lb = _lower_bound(lbp_ref[...])
    qf = _silu(q_ref[...]) * (HG_DK ** -0.5)
    fg = lb + (1.0 - lb) * _sigmoid(f_ref[...])
    kf = 1.0 - fg
    v = v_ref[...]
    ones = jnp.ones((bs, HG_DV), F32)
    zeros = jnp.zeros((bs, HG_DV), F32)
    n_groups = 9
    pad = jnp.zeros((LANES - n_groups * bs, LANES), F32)
    for hd in range(HG_HEADS):
        cs = slice(hd * HG_DK, (hd + 1) * HG_DK)
        f3, k3, q3, v3 = _split3(fg[:, cs]), _split3(kf[:, cs]), _split3(qf[:, cs]), _split3(v[:, cs])
        left = jnp.concatenate([f3[0], f3[1], f3[2], k3[0], k3[1], k3[0], q3[0], q3[1], q3[2], pad], axis=0)
        left_t = left.T.astype(BF)
        right = jnp.concatenate(
            [jnp.concatenate(blk, axis=1) for blk in
             [(ones, zeros, zeros)] * 3 + [(zeros, v3[0], zeros), (zeros, v3[0], zeros), (zeros, v3[1], zeros)]
             + [(zeros, zeros, ones)] * 3] + [jnp.concatenate((pad, pad, pad), axis=1)], axis=0).astype(BF)
        o_rows = []
        for r in range(bs):
            prod = _dot(left_t * lane_masks_ref[r], right)
            s_new = prod[:, :HG_DV] * s_ref[r, hd] + prod[:, HG_DV:2 * HG_DV]
            snew_ref[r, hd] = s_new
            o_rows.append(jnp.sum(prod[:, 2 * HG_DV:] * s_new, axis=0, keepdims=True))
        o_ref[:, cs] = _head_norm(jnp.concatenate(o_rows, axis=0), hng_ref[:, cs])


def _lane_masks():
    r = jnp.arange(SUBLANES)[:, None, None]
    lane = jnp.arange(LANES)[None, None, :] % SUBLANES
    return jnp.broadcast_to(lane == r, (SUBLANES, LANES, LANES)).astype(BF)


def _sample_state(proj, state, wts, bs):
    n = proj.shape[0]
    col_spec = lambda off: pl.BlockSpec((bs, HG_DIM), lambda i, off=off: (i, off // HG_DIM))
    st_spec = pl.BlockSpec((bs, HG_HEADS, HG_DK, HG_DV), lambda i: (i, 0, 0, 0))
    return pl.pallas_call(
        functools.partial(_sample_state_kernel, bs=bs),
        grid=(n // bs,),
        in_specs=[col_spec(OFF_Q), col_spec(OFF_F), col_spec(OFF_V),
                  _const_spec(wts['lb_param'].shape), _const_spec(wts['hn_g'].shape),
                  _const_spec((SUBLANES, LANES, LANES)), st_spec],
        out_specs=[st_spec, pl.BlockSpec((bs, HG_DIM), lambda i: (i, 0))],
        out_shape=[jax.ShapeDtypeStruct(state.shape, F32), jax.ShapeDtypeStruct((n, HG_DIM), F32)],
        compiler_params=pltpu.CompilerParams(dimension_semantics=("arbitrary",), vmem_limit_bytes=VMEM_LIMIT),
        name="sample_state",
    )(proj, proj, proj, wts['lb_param'], wts['hn_g'], _lane_masks(), state)


def _sample_tail_kernel(x_ref, proj_ref, o_ref, cst_ref, bg_ref, cw_ref, cb_ref, lng_ref, lnb_ref, wco_ref,
                        who_ref, wout_ref, gffn_ref, wr_ref, x1_ref, h2_ref, route_ref, nconv_ref):
    keep = (CONV_WIDTH - 2) * CONV_DIM
    u = proj_ref[:, OFF_CV:OFF_CV + CONV_DIM] * _sigmoid(proj_ref[:, OFF_CG:OFF_CG + CONV_DIM])
    acc = u * cw_ref[pl.ds(CONV_WIDTH - 1, 1), :]
    for j in range(CONV_WIDTH - 1):
        acc = acc + cst_ref[:, j * CONV_DIM:(j + 1) * CONV_DIM] * cw_ref[pl.ds(j, 1), :]
    nconv_ref[:, :keep] = cst_ref[:, CONV_DIM:]
    nconv_ref[:, keep:] = u
    c = _conv_post(acc, cb_ref[...], lng_ref[...], lnb_ref[...])
    y_a = _dot(c.astype(BF), wco_ref[...])
    og = proj_ref[:, OFF_OG:OFF_OG + HG_DIM]
    y_b = _dot((o_ref[...] * _silu(og)).astype(BF), who_ref[...])
    m = (_sigmoid(proj_ref[:, OFF_GA:OFF_GA + D_MODEL] + bg_ref[:, :D_MODEL]) * y_a
         + _sigmoid(proj_ref[:, OFF_GB:OFF_GB + D_MODEL] + bg_ref[:, D_MODEL:]) * y_b)
    x1, h2, route = _mixer_tail(x_ref[...], m, wout_ref[...], gffn_ref[...], wr_ref[...])
    x1_ref[...] = x1
    h2_ref[...] = h2
    route_ref[...] = route


def _sample_tail(xs, proj, o, conv_state2, wts):
    n = xs.shape[0]
    ins = [xs, proj, o, conv_state2, wts['b_gates'], wts['conv_w'], wts['conv_b'], wts['ln_g'], wts['ln_b'],
           wts['w_conv_out'], wts['w_hgrn_out'], wts['w_out'], wts['gffn'], wts['w_router']]
    full = lambda shape: pl.BlockSpec(shape, lambda i: (0,) * len(shape))
    return pl.pallas_call(
        _sample_tail_kernel,
        grid=(1,),
        in_specs=[_const_spec(a.shape) for a in ins],
        out_specs=[full((n, D_MODEL)), full((n, ROW_W)), full((n, ROUTE_LANES)), full(conv_state2.shape)],
        out_shape=[jax.ShapeDtypeStruct((n, D_MODEL), F32), jax.ShapeDtypeStruct((n, ROW_W), F32),
                   jax.ShapeDtypeStruct((n, ROUTE_LANES), F32), jax.ShapeDtypeStruct(conv_state2.shape, F32)],
        compiler_params=pltpu.CompilerParams(vmem_limit_bytes=VMEM_LIMIT),
        name="sample_tail",
    )(*ins)


def _plan_kernel(slab_p_ref, slab_s_ref, pos_ref, tinfo_ref, e_ref, pre_ref, *, tm, n_tok, max_tiles):
    n_p = slab_p_ref.shape[0]
    nblk = n_tok // LANES
    blk_p = n_p // LANES
    eidx = lax.broadcasted_iota(jnp.int32, (N_EXPERTS, LANES), 0).astype(F32)
    ti = lax.broadcasted_iota(jnp.int32, (LANES, LANES), 0)
    si = lax.broadcasted_iota(jnp.int32, (LANES, LANES), 1)
    before = (ti < si).astype(BF)

    def onehots(b):
        c0 = pl.multiple_of(b * LANES, LANES)
        e1 = e_ref[0:1, pl.ds(c0, LANES)]
        e2 = e_ref[1:2, pl.ds(c0, LANES)]
        return (e1 == eidx).astype(F32), (e2 == eidx).astype(F32), c0

    def transpose_block(slab_ref, b_local, b_global):
        r0 = pl.multiple_of(b_local * LANES, LANES)
        c0 = pl.multiple_of(b_global * LANES, LANES)
        e_ref[:, pl.ds(c0, LANES)] = slab_ref[pl.ds(r0, LANES), :].T[0:SUBLANES, :]

    def load_p(b, c):
        transpose_block(slab_p_ref, b, b)
        return c

    lax.fori_loop(0, blk_p, load_p, 0)

    def load_s(b, c):
        transpose_block(slab_s_ref, b, b + blk_p)
        return c

    lax.fori_loop(0, nblk - blk_p, load_s, 0)

    def count(b, carry):
        h1, h2, c0 = onehots(b)
        h = h1 + h2
        pre_ref[:, pl.ds(c0, LANES)] = _dot(h.astype(BF), before) + carry
        return carry + jnp.sum(h, axis=1, keepdims=True)

    counts = lax.fori_loop(0, nblk, count, jnp.zeros((N_EXPERTS, 1), F32))
    tiles_per = jnp.floor((counts + (tm - 1)) * (1.0 / tm))
    ei = lax.broadcasted_iota(jnp.int32, (N_EXPERTS, N_EXPERTS), 0)
    ej = lax.broadcasted_iota(jnp.int32, (N_EXPERTS, N_EXPERTS), 1)
    upto = (ej <= ei).astype(BF)
    tile_end = _dot(upto, jnp.broadcast_to(tiles_per, (N_EXPERTS, LANES)).astype(BF))[:, 0:1]
    starts = (tile_end - tiles_per) * tm

    def place(b, c):
        h1, h2, c0 = onehots(b)
        dest = pre_ref[:, pl.ds(c0, LANES)] + starts
        pos_ref[0:1, pl.ds(c0, LANES)] = jnp.sum(h1 * dest, axis=0, keepdims=True).astype(jnp.int32)
        pos_ref[1:2, pl.ds(c0, LANES)] = jnp.sum(h2 * dest, axis=0, keepdims=True).astype(jnp.int32)
        return c

    pos_ref[...] = jnp.zeros_like(pos_ref)
    lax.fori_loop(0, nblk, place, 0)

    tile = lax.broadcasted_iota(jnp.int32, (N_EXPERTS, max_tiles), 1).astype(F32)
    t_exp = jnp.sum((tile_end <= tile).astype(F32), axis=0, keepdims=True)
    t_exp = jnp.minimum(t_exp, N_EXPERTS - 1.0).astype(jnp.int32)
    n_used = jnp.broadcast_to(tile_end[N_EXPERTS - 1:N_EXPERTS, :], (1, max_tiles)).astype(jnp.int32)
    row = lax.broadcasted_iota(jnp.int32, (SUBLANES, max_tiles), 0)
    tinfo_ref[...] = jnp.where(row == 0, t_exp, jnp.where(row == 1, n_used, 0))


def _plan(slab_p, slab_s, tm, max_tiles):
    n_tok = slab_p.shape[0] + slab_s.shape[0]
    assert slab_p.shape[0] % LANES == 0 and slab_s.shape[0] % LANES == 0
    vm = pl.BlockSpec(memory_space=pltpu.VMEM)
    return pl.pallas_call(
        functools.partial(_plan_kernel, tm=tm, n_tok=n_tok, max_tiles=max_tiles),
        in_specs=[vm, vm],
        out_specs=[vm, vm],
        out_shape=[jax.ShapeDtypeStruct((SUBLANES, n_tok), jnp.int32),
                   jax.ShapeDtypeStruct((SUBLANES, max_tiles), jnp.int32)],
        scratch_shapes=[pltpu.VMEM((SUBLANES, n_tok), F32), pltpu.VMEM((N_EXPERTS, n_tok), F32)],
        compiler_params=pltpu.CompilerParams(vmem_limit_bytes=VMEM_LIMIT),
        name="route_plan",
    )(slab_p, slab_s)


def _dispatch_kernel(pos_ref, h_ref, xs_in_ref, xs_ref, sem, *, tb, tok0):
    del xs_in_ref
    i = pl.program_id(0)

    def copies(r):
        base = 2 * (tok0 + i * tb + r)
        src = h_ref.at[pl.ds(r, 1), :]
        return (pltpu.make_async_copy(src, xs_ref.at[pl.ds(pos_ref[base], 1), :], sem),
                pltpu.make_async_copy(src, xs_ref.at[pl.ds(pos_ref[base + 1], 1), :], sem))

    def issue(r, c):
        c0, c1 = copies(r)
        c0.start()
        c1.start()
        return c

    def drain(r, c):
        c0, c1 = copies(r)
        c0.wait()
        c1.wait()
        return c

    lax.fori_loop(0, tb, issue, 0, unroll=8)
    lax.fori_loop(0, tb, drain, 0, unroll=8)


def _dispatch(pos_flat, hpk, xs, tb, tok0):
    n = hpk.shape[0]
    grid_spec = pltpu.PrefetchScalarGridSpec(
        num_scalar_prefetch=1,
        grid=(n // tb,),
        in_specs=[pl.BlockSpec((tb, ROW_W), lambda i, p: (i, 0)),
                  pl.BlockSpec(memory_space=pl.ANY)],
        out_specs=pl.BlockSpec(memory_space=pl.ANY),
        scratch_shapes=[pltpu.SemaphoreType.DMA(())])
    return pl.pallas_call(
        functools.partial(_dispatch_kernel, tb=tb, tok0=tok0),
        grid_spec=grid_spec,
        out_shape=jax.ShapeDtypeStruct(xs.shape, xs.dtype),
        input_output_aliases={2: 0},
        compiler_params=pltpu.CompilerParams(dimension_semantics=("arbitrary",), vmem_limit_bytes=VMEM_LIMIT),
        name="dispatch",
    )(pos_flat, hpk, xs)


def _expert_kernel(te_ref, nt_ref, xs_ref, wg_ref, wu_ref, wd_ref, y_ref, wg_b, wu_b, wd_b):
    i = pl.program_id(0)

    @pl.when(i < nt_ref[0])
    def _():
        changed = jnp.logical_or(i == 0, te_ref[i] != te_ref[jnp.maximum(i - 1, 0)])

        @pl.when(changed)
        def _():
            wg_b[...] = wg_ref[0].astype(BF)
            wu_b[...] = wu_ref[0].astype(BF)
            wd_b[...] = wd_ref[0].astype(BF)

        xb = xs_ref[...].astype(BF)
        gate = _dot(xb, wg_b[...])
        up = _dot(xb, wu_b[...])
        y_ref[...] = _dot((_silu(gate) * up).astype(BF), wd_b[...])

    @pl.when(i >= nt_ref[0])
    def _():
        y_ref[...] = jnp.zeros_like(y_ref)


def _experts(xs, tile_expert, n_tiles, wg, wu, wd, tm, max_tiles):
    grid_spec = pltpu.PrefetchScalarGridSpec(
        num_scalar_prefetch=2,
        grid=(max_tiles,),
        in_specs=[pl.BlockSpec((tm, ROW_W), lambda i, te, nt: (i, 0)),
                  pl.BlockSpec((1, D_MODEL, EXPERT_FF), lambda i, te, nt: (te[i], 0, 0)),
                  pl.BlockSpec((1, D_MODEL, EXPERT_FF), lambda i, te, nt: (te[i], 0, 0)),
                  pl.BlockSpec((1, EXPERT_FF, D_MODEL), lambda i, te, nt: (te[i], 0, 0))],
        out_specs=pl.BlockSpec((tm, D_MODEL), lambda i, te, nt: (i, 0)),
        scratch_shapes=[pltpu.VMEM((D_MODEL, EXPERT_FF), BF), pltpu.VMEM((D_MODEL, EXPERT_FF), BF),
                        pltpu.VMEM((EXPERT_FF, D_MODEL), BF)])
    return pl.pallas_call(
        _expert_kernel,
        grid_spec=grid_spec,
        out_shape=jax.ShapeDtypeStruct((max_tiles * tm, D_MODEL), F32),
        compiler_params=pltpu.CompilerParams(dimension_semantics=("arbitrary",), vmem_limit_bytes=VMEM_LIMIT),
        name="experts",
    )(tile_expert, n_tiles, xs, wg, wu, wd)


def _combine_kernel(pos_ref, x1_ref, route_ref, gfin_ref, ys_ref, y_ref, buf0, buf1, sem, *, tb, tok0):
    i = pl.program_id(0)

    def copies(r):
        base = 2 * (tok0 + i * tb + r)
        return (pltpu.make_async_copy(ys_ref.at[pl.ds(pos_ref[base], 1), :], buf0.at[pl.ds(r, 1), :], sem),
                pltpu.make_async_copy(ys_ref.at[pl.ds(pos_ref[base + 1], 1), :], buf1.at[pl.ds(r, 1), :], sem))

    def issue(r, c):
        c0, c1 = copies(r)
        c0.start()
        c1.start()
        return c

    def drain(r, c):
        c0, c1 = copies(r)
        c0.wait()
        c1.wait()
        return c

    lax.fori_loop(0, tb, issue, 0, unroll=8)
    lax.fori_loop(0, tb, drain, 0, unroll=8)
    route = route_ref[...]
    out = x1_ref[...] + (route[:, 2:3] * buf0[...] + route[:, 3:4] * buf1[...])
    y_ref[...] = _rms(out, gfin_ref[...])


def _combine(pos_flat, x1, route, gfin, ys, tb, tok0):
    n = x1.shape[0]
    grid_spec = pltpu.PrefetchScalarGridSpec(
        num_scalar_prefetch=1,
        grid=(n // tb,),
        in_specs=[pl.BlockSpec((tb, D_MODEL), lambda i, p: (i, 0)),
                  pl.BlockSpec((tb, ROUTE_LANES), lambda i, p: (i, 0)),
                  pl.BlockSpec((1, D_MODEL), lambda i, p: (0, 0)),
                  pl.BlockSpec(memory_space=pl.ANY)],
        out_specs=pl.BlockSpec((tb, D_MODEL), lambda i, p: (i, 0)),
        scratch_shapes=[pltpu.VMEM((tb, D_MODEL), F32), pltpu.VMEM((tb, D_MODEL), F32),
                        pltpu.SemaphoreType.DMA(())])
    return pl.pallas_call(
        functools.partial(_combine_kernel, tb=tb, tok0=tok0),
        grid_spec=grid_spec,
        out_shape=jax.ShapeDtypeStruct((n, D_MODEL), F32),
        compiler_params=pltpu.CompilerParams(dimension_semantics=("arbitrary",), vmem_limit_bytes=VMEM_LIMIT),
        name="combine",
    )(pos_flat, x1, route, gfin, ys)


def kernel(x_prompt, x_sample, state_conv, state_hgrn, norm_mix_g, w_in, b_gates, conv_dw_w, conv_dw_b,
           conv_ln_g, conv_ln_b, w_conv_out, hgrn_lb_param, hgrn_norm_g, w_hgrn_out, w_out, norm_ffn_g,
           w_router_group, w_router_expert, w_expert_gate, w_expert_up, w_expert_down, norm_final_g):
    batch, seq, _ = x_prompt.shape
    dec_batch = x_sample.shape[0]
    assert x_sample.shape[1] == 1 and w_in.shape[0] == 1
    tb = min(256, seq)
    tm = 256
    bs = min(8, dec_batch)
    n_p = batch * seq
    n_all = n_p + dec_batch
    tbd = min(1024, n_p)
    tbc = min(512, n_p)
    assert seq % tb == 0 and tb % CHUNK == 0 and dec_batch % bs == 0 and n_p % tbd == 0 and n_p % tbc == 0

    w_router = jnp.concatenate(
        [w_router_group[0], w_router_expert[0],
         jnp.zeros((D_MODEL, ROUTE_LANES - N_GROUPS - N_EXPERTS), F32)], axis=1).astype(BF)
    wts = dict(gmix=norm_mix_g, w_in=w_in[0].astype(BF), b_gates=b_gates, conv_w=conv_dw_w[0], conv_b=conv_dw_b,
               ln_g=conv_ln_g, ln_b=conv_ln_b, w_conv_out=w_conv_out[0].astype(BF), lb_param=hgrn_lb_param,
               hn_g=hgrn_norm_g, w_hgrn_out=w_hgrn_out[0].astype(BF), w_out=w_out[0].astype(BF),
               gffn=norm_ffn_g, w_router=w_router)

    x1_p, hpk_p, route_p, nconv_p, nhgrn_p = _prompt_mixer(x_prompt.reshape(n_p, D_MODEL), batch, seq, tb, wts)

    xs_tok = x_sample.reshape(dec_batch, D_MODEL)
    proj_s = _sample_proj(xs_tok, wts)
    nhgrn_s, o_s = _sample_state(proj_s, state_hgrn[0], wts, bs)
    conv2 = state_conv[0].reshape(dec_batch, (CONV_WIDTH - 1) * CONV_DIM)
    x1_s, hpk_s, route_s, nconv_s = _sample_tail(xs_tok, proj_s, o_s, conv2, wts)

    max_tiles = -(-((2 * n_all) // tm + N_EXPERTS) // SUBLANES) * SUBLANES
    pos, tinfo = _plan(route_p, route_s, tm, max_tiles)
    pos_flat = pos[0:2].T.reshape(-1)
    tile_expert, n_tiles = tinfo[0], tinfo[1, 0:1]

    xs = jnp.zeros((max_tiles * tm, ROW_W), F32)
    xs = _dispatch(pos_flat, hpk_p, xs, tbd, 0)
    xs = _dispatch(pos_flat, hpk_s, xs, dec_batch, n_p)
    ys = _experts(xs, tile_expert, n_tiles, w_expert_gate[0], w_expert_up[0], w_expert_down[0], tm, max_tiles)

    gfin = norm_final_g.reshape(1, D_MODEL)
    y_p = _combine(pos_flat, x1_p, route_p, gfin, ys, tbc, 0)
    y_s = _combine(pos_flat, x1_s, route_s, gfin, ys, dec_batch, n_p)

    return (y_p.reshape(batch, seq, D_MODEL), y_s.reshape(dec_batch, 1, D_MODEL),
            nconv_p[None, :, HIST_PAD:, :], nhgrn_p[None],
            nconv_s.reshape(1, dec_batch, CONV_WIDTH - 1, CONV_DIM), nhgrn_s[None])
```

```python
import functools

import jax
import jax.numpy as jnp
from jax import lax
from jax.experimental import pallas as pl
from jax.experimental.pallas import tpu as pltpu

D_MODEL = 1024
CONV_DIM = D_MODEL // 2
CONV_WIDTH = 31
HG_HEADS = 8
HG_DK = 128
HG_DV = 128
HG_DIM = HG_HEADS * HG_DK
N_GROUPS = 4
EXPERTS_PER_GROUP = 8
N_EXPERTS = N_GROUPS * EXPERTS_PER_GROUP
EXPERT_FF = D_MODEL // 2
EPS = 1e-6
IN_COLS = 2 * CONV_DIM + 4 * HG_DIM + 2 * D_MODEL

OFF_CV, OFF_CG = 0, CONV_DIM
OFF_Q = 2 * CONV_DIM
OFF_F = OFF_Q + HG_DIM
OFF_V = OFF_F + HG_DIM
OFF_OG = OFF_V + HG_DIM
OFF_GA = OFF_OG + HG_DIM
OFF_GB = OFF_GA + D_MODEL

SUBLANES = 8
LANES = 128
HIST_ROWS = 32
HIST_PAD = HIST_ROWS - (CONV_WIDTH - 1)
CHUNK = 64
NBLK = CHUNK // SUBLANES
CONV_ROWS = 32
PROJ_COLS = 256
ROUTE_LANES = 128
ROW_W = D_MODEL
VMEM_LIMIT = 56 * 1024 * 1024

BF = jnp.bfloat16
F32 = jnp.float32


def _dot(a, b):
    return jnp.dot(a, b, preferred_element_type=F32)


def _dot_nt(a, b):
    return lax.dot_general(a, b, (((1,), (1,)), ((), ())), preferred_element_type=F32)


def _dot_tn(a, b, precision=None):
    return lax.dot_general(a, b, (((0,), (0,)), ((), ())), preferred_element_type=F32,
                           precision=precision)


NEG_LOG2E = -1.4426950408889634


def _sigmoid(x):
    return 1.0 / (1.0 + jnp.exp2(x * NEG_LOG2E))


def _silu(x):
    return x * _sigmoid(x)


def _rms(xf, g):
    return xf * lax.rsqrt(jnp.mean(xf * xf, axis=-1, keepdims=True) + EPS) * g


def _lower_bound(lb_param):
    m = jnp.max(lb_param, axis=0, keepdims=True)
    e = jnp.exp(lb_param - m)
    return e[0:1] / jnp.sum(e, axis=0, keepdims=True)


def _conv_post(c, conv_b, ln_g, ln_b):
    c = c + conv_b
    mu = jnp.mean(c, axis=-1, keepdims=True)
    d = c - mu
    var = jnp.mean(d * d, axis=-1, keepdims=True)
    return _silu(d * lax.rsqrt(var + EPS) * ln_g + ln_b)


def _head_norm(o, g):
    return o * lax.rsqrt(jnp.mean(o * o, axis=-1, keepdims=True) + EPS) * g


def _route(logits):
    col = lax.broadcasted_iota(jnp.int32, logits.shape, 1)
    big = jnp.int32(1 << 20)
    neg = jnp.float32(-jnp.inf)
    gmask = col < N_GROUPS
    lg = jnp.where(gmask, logits, neg)
    gmax = jnp.max(lg, axis=-1, keepdims=True)
    gsum = jnp.sum(jnp.where(gmask, jnp.exp(lg - gmax), 0.0), axis=-1, keepdims=True)
    gval = 1.0 / gsum
    gidx = jnp.min(jnp.where(lg == gmax, col, big), axis=-1, keepdims=True)
    lo = N_GROUPS + EXPERTS_PER_GROUP * gidx
    emask = (col >= lo) & (col < lo + EXPERTS_PER_GROUP)
    el = jnp.where(emask, logits, neg)
    m1 = jnp.max(el, axis=-1, keepdims=True)
    i1 = jnp.min(jnp.where(el == m1, col, big), axis=-1, keepdims=True)
    el2 = jnp.where(col == i1, neg, el)
    m2 = jnp.max(el2, axis=-1, keepdims=True)
    i2 = jnp.min(jnp.where(el2 == m2, col, big), axis=-1, keepdims=True)
    r = jnp.exp(m2 - m1)
    w1 = gval / (1.0 + r)
    w2 = gval * r / (1.0 + r)
    e1 = (i1 - N_GROUPS).astype(F32)
    e2 = (i2 - N_GROUPS).astype(F32)
    return jnp.where(col == 0, e1, jnp.where(col == 1, e2, jnp.where(col == 2, w1, jnp.where(col == 3, w2, 0.0))))


LEVELS = (NBLK // 2, NBLK // 4, NBLK // 8)
ROW_TOT, ROW_QE, ROW_KD, ROW_LEVEL = 0, NBLK, 2 * NBLK, 3 * NBLK
TABLE_ROWS = ROW_LEVEL + 2 * NBLK * len(LEVELS)


def _row_bcast(ref, row):
    return jnp.broadcast_to(ref[pl.ds(row, 1), :], (SUBLANES, LANES))


def _per_block(ref, base):
    return jnp.concatenate([_row_bcast(ref, base + j) for j in range(NBLK)], axis=0)


def _block_rows(ref, c):
    return jnp.concatenate([_row_bcast(ref, j * SUBLANES + c) for j in range(NBLK)], axis=0)


def _sparse_tile(x, ref, base, blocks):
    groups = []
    for g in range(0, NBLK, 2):
        if g not in blocks and g + 1 not in blocks:
            groups.append(jnp.zeros((2 * SUBLANES, LANES), BF))
            continue
        halves = [x[j * SUBLANES:(j + 1) * SUBLANES, :] * _row_bcast(ref, base + j) if j in blocks
                  else jnp.zeros((SUBLANES, LANES), F32) for j in (g, g + 1)]
        groups.append(jnp.concatenate(halves, axis=0).astype(BF))
    return jnp.concatenate(groups, axis=0)


def _chunk_tables(l2, bl_ref, tab_ref):
    rin = lax.broadcasted_iota(jnp.int32, (CHUNK, LANES), 0) % SUBLANES
    bl = l2
    for s in (1, 2, 4):
        bl = bl + jnp.where(rin >= s, pltpu.roll(bl, s, axis=0), 0.0)
    bl_ref[...] = bl
    tot = bl_ref[pl.ds(SUBLANES - 1, NBLK, stride=SUBLANES), :]
    brow = lax.broadcasted_iota(jnp.int32, (NBLK, LANES), 0)
    rb = tot
    for s in (1, 2, 4):
        rb = rb + jnp.where(brow >= s, pltpu.roll(rb, s, axis=0), 0.0)
    rb_prev = rb - tot
    total = rb[NBLK - 1:NBLK, :]

    tab_ref[pl.ds(ROW_TOT, NBLK), :] = tot
    tab_ref[pl.ds(ROW_QE, NBLK), :] = jnp.exp2(rb_prev)
    tab_ref[pl.ds(ROW_KD, NBLK), :] = jnp.exp2(total - rb)
    for lv, cb in enumerate(LEVELS):
        edges = [(j // (2 * cb)) * (2 * cb) + cb - 1 for j in range(NBLK)]
        rb_edge = jnp.concatenate([rb[e:e + 1, :] for e in edges], axis=0)
        base = ROW_LEVEL + 2 * NBLK * lv
        tab_ref[pl.ds(base, NBLK), :] = jnp.exp2(jnp.minimum(rb_prev - rb_edge, 0.0))
        tab_ref[pl.ds(base + NBLK, NBLK), :] = jnp.exp2(jnp.minimum(rb_edge - rb, 0.0))
    return bl, total


def _chunk_operands(q, k, v, st, bl, row_masks, bl_ref, tab_ref):
    qp = q * jnp.exp2(bl)
    kp = k * jnp.exp2(_per_block(tab_ref, ROW_TOT) - bl)
    qe = (qp * _per_block(tab_ref, ROW_QE)).astype(BF)
    kdec = (kp * _per_block(tab_ref, ROW_KD)).astype(BF)

    q_tiles, k_tiles = [], []
    for lv, cb in enumerate(LEVELS):
        base = ROW_LEVEL + 2 * NBLK * lv
        for p0 in range(0, NBLK, 2 * cb):
            q_tiles.append(_sparse_tile(qp, tab_ref, base, range(p0 + cb, p0 + 2 * cb)))
            k_tiles.append(_sparse_tile(kp, tab_ref, base + NBLK, range(p0, p0 + cb)))

    k_b = k.astype(BF)
    lhs = [(q * jnp.exp2(jnp.minimum(bl - _block_rows(bl_ref, c), 0.0))).astype(BF) for c in range(SUBLANES)]
    rhs = [k_b * row_masks[c] for c in range(SUBLANES)]
    return dict(qe=qe, kdec=kdec, st=st.astype(BF), v=v.astype(BF),
                off_l=jnp.concatenate(q_tiles, axis=1), off_r=jnp.concatenate(k_tiles, axis=1),
                diag_l=jnp.concatenate(lhs, axis=1), diag_r=jnp.concatenate(rhs, axis=1))


def _chunk_products(ops):
    return (_dot_nt(ops['qe'], ops['st']), _dot_tn(ops['v'], ops['kdec']),
            _dot_nt(ops['off_l'], ops['off_r']), _dot_nt(ops['diag_l'], ops['diag_r']))


def _chunk_output(o_inter, off_diag, diag, v_b):
    ti = lax.broadcasted_iota(jnp.int32, (CHUNK, CHUNK), 0)
    si = lax.broadcasted_iota(jnp.int32, (CHUNK, CHUNK), 1)
    scores = jnp.where((ti // SUBLANES == si // SUBLANES) & (si <= ti), diag, off_diag)
    return o_inter + _dot(scores.astype(BF), v_b)


def _mixer_tail(x, m, w_out, g_ffn, w_router):
    x1 = x + _dot(m.astype(BF), w_out)
    h2 = _rms(x1, g_ffn).astype(BF)
    route = _route(_dot(h2, w_router))
    return x1, h2.astype(F32), route


N_MIXER_INPUTS = 17


def _prompt_mixer_kernel(*refs, tb, nt, n_blocks, n_tail):
    i = pl.program_id(0)
    h2_tail_ref, h2_ref = refs[N_MIXER_INPUTS - 1], refs[N_MIXER_INPUTS + 1]

    @pl.when(i < n_blocks)
    def _():
        _mixer_step(*refs[:N_MIXER_INPUTS - 1], *refs[N_MIXER_INPUTS:], t=i % nt, tb=tb)

    @pl.when(i == n_blocks)
    def _():
        h2_ref[pl.ds(0, n_tail), :] = h2_tail_ref[...]


def _mixer_step(x_ref, gmix_ref, win_ref, bg_ref, cw_ref, cb_ref, lng_ref, lnb_ref, wco_ref,
                lbp_ref, hng_ref, who_ref, wout_ref, gffn_ref, wr_ref, masks_ref,
                x1_ref, h2_ref, route_ref, nconv_ref, nhgrn_ref,
                hist_ref, phase_ref, conv_ref, st_ref, q_s, k_s, v_s, lf_s, o_s, og_s, ga_s, gb_s, *head_scr, t, tb):
    @pl.when(t == 0)
    def _():
        hist_ref[pl.ds(0, HIST_ROWS), :] = jnp.zeros((HIST_ROWS, CONV_DIM), F32)
        st_ref[...] = jnp.zeros_like(st_ref)

    x = x_ref[...]
    h = _rms(x, gmix_ref[...]).astype(BF)

    cv = _dot(h, win_ref[:, OFF_CV:OFF_CV + CONV_DIM])
    cg = _dot(h, win_ref[:, OFF_CG:OFF_CG + CONV_DIM])
    hist_ref[pl.ds(HIST_ROWS, tb), :] = cv * _sigmoid(cg)

    lb = _lower_bound(lbp_ref[...])

    def project(seg, c0):
        z = _dot(h, win_ref[:, seg + c0:seg + c0 + PROJ_COLS])
        cols = slice(c0, c0 + PROJ_COLS)
        if seg == OFF_Q:
            q_s[:, cols] = _silu(z) * (HG_DK ** -0.5)
        elif seg == OFF_F:
            fg = lb[:, cols] + (1.0 - lb[:, cols]) * _sigmoid(z)
            lf_s[:, cols] = jnp.log2(fg)
            k_s[:, cols] = 1.0 - fg
        elif seg == OFF_V:
            v_s[:, cols] = z
        elif seg == OFF_OG:
            og_s[:, cols] = _silu(z)
        elif seg == OFF_GA:
            ga_s[:, cols] = _sigmoid(z + bg_ref[:, c0:c0 + PROJ_COLS])
        else:
            gb_s[:, cols] = _sigmoid(z + bg_ref[:, D_MODEL + c0:D_MODEL + c0 + PROJ_COLS])

    proj_jobs = [functools.partial(project, seg, c0)
                 for seg in (OFF_Q, OFF_F, OFF_V, OFF_OG, OFF_GA, OFF_GB) for c0 in range(0, HG_DIM, PROJ_COLS)]

    span = tb + SUBLANES * (-(-CONV_WIDTH // SUBLANES) - 1)

    def phase_copy(r):
        n = min(span, HIST_ROWS + tb - HIST_PAD - r)
        phase_ref[r, pl.ds(0, n), :] = hist_ref[pl.ds(HIST_PAD + r, n), :]

    def conv_rows(r0):
        acc = jnp.zeros((CONV_ROWS, CONV_DIM), F32)
        for j in range(CONV_WIDTH):
            w_j = jnp.broadcast_to(cw_ref[pl.ds(j, 1), :], (SUBLANES, CONV_DIM))
            acc = acc + (phase_ref[j % SUBLANES, pl.ds(r0 + j - j % SUBLANES, CONV_ROWS), :]
                         * jnp.concatenate([w_j] * (CONV_ROWS // SUBLANES), axis=0))
        conv_ref[pl.ds(r0, CONV_ROWS), :] = acc

    conv_jobs = ([functools.partial(phase_copy, r) for r in range(SUBLANES)]
                 + [functools.partial(conv_rows, r0) for r0 in range(0, tb, CONV_ROWS)])
    per_conv_job = -(-len(proj_jobs) // len(conv_jobs))
    for i, job in enumerate(conv_jobs):
        job()
        for pj in proj_jobs[i * per_conv_job:(i + 1) * per_conv_job]:
            pj()
    for pj in proj_jobs[len(conv_jobs) * per_conv_job:]:
        pj()

    tail = hist_ref[pl.ds(tb, HIST_ROWS), :]
    hist_ref[pl.ds(0, HIST_ROWS), :] = tail
    nconv_ref[0] = tail
    c = _conv_post(conv_ref[...], cb_ref[...], lng_ref[...], lnb_ref[...])
    y_a = _dot(c.astype(BF), wco_ref[...])

    row_masks = masks_ref[...]
    cols = [slice(hd * HG_DK, (hd + 1) * HG_DK) for hd in range(HG_HEADS)]
    bl_refs, tab_refs = head_scr[:HG_HEADS], head_scr[HG_HEADS:]
    for r0 in range(0, tb, CHUNK):
        rows = pl.ds(r0, CHUNK)
        tabs = [_chunk_tables(lf_s[rows, cols[hd]], bl_refs[hd], tab_refs[hd]) for hd in range(HG_HEADS)]
        ops = [_chunk_operands(q_s[rows, cols[hd]], k_s[rows, cols[hd]], v_s[rows, cols[hd]], st_ref[hd],
                               tabs[hd][0], row_masks, bl_refs[hd], tab_refs[hd]) for hd in range(HG_HEADS)]
        prods = [_chunk_products(op) for op in ops]
        for hd in range(HG_HEADS):
            o_inter, update, off_diag, diag = prods[hd]
            st_ref[hd] = st_ref[hd] * jnp.exp2(tabs[hd][1]) + update
            o = _chunk_output(o_inter, off_diag, diag, ops[hd]['v'])
            o_s[rows, cols[hd]] = _head_norm(o, hng_ref[:, cols[hd]])
    for hd in range(HG_HEADS):
        nhgrn_ref[0, hd] = st_ref[hd].T

    y_b = _dot((o_s[...] * og_s[...]).astype(BF), who_ref[...])
    x1, h2, route = _mixer_tail(x, ga_s[...] * y_a + gb_s[...] * y_b, wout_ref[...], gffn_ref[...], wr_ref[...])
    x1_ref[...] = x1
    h2_ref[...] = h2
    route_ref[...] = route


def _const_spec(shape):
    nd = len(shape)
    return pl.BlockSpec(shape, lambda *_: (0,) * nd, pipeline_mode=pl.Buffered(1))


def _row_masks():
    c = jnp.arange(SUBLANES)[:, None, None]
    r = jnp.arange(CHUNK)[None, :, None] % SUBLANES
    return jnp.broadcast_to(r == c, (SUBLANES, CHUNK, LANES)).astype(BF)


def _prompt_mixer(x2, h2_tail, batch, seq, tb, wts):
    nt = seq // tb
    n = batch * seq
    n_blocks = batch * nt
    n_tail = h2_tail.shape[0]
    assert n_tail <= tb
    row_spec = lambda w: pl.BlockSpec((tb, w), lambda i: (jnp.minimum(i, n_blocks - 1), 0))
    seq_of = lambda i: jnp.minimum(i // nt, batch - 1)
    consts = [wts['gmix'], wts['w_in'], wts['b_gates'], wts['conv_w'], wts['conv_b'], wts['ln_g'], wts['ln_b'],
              wts['w_conv_out'], wts['lb_param'], wts['hn_g'], wts['w_hgrn_out'], wts['w_out'], wts['gffn'],
              wts['w_router'], _row_masks()]
    span = tb + SUBLANES * (-(-CONV_WIDTH // SUBLANES) - 1)
    return pl.pallas_call(
        functools.partial(_prompt_mixer_kernel, tb=tb, nt=nt, n_blocks=n_blocks, n_tail=n_tail),
        grid=(n_blocks + 1,),
        in_specs=[row_spec(D_MODEL)] + [_const_spec(c.shape) for c in consts] + [_const_spec(h2_tail.shape)],
        out_specs=[row_spec(D_MODEL), pl.BlockSpec((tb, ROW_W), lambda i: (i, 0)), row_spec(ROUTE_LANES),
                   pl.BlockSpec((1, HIST_ROWS, CONV_DIM), lambda i: (seq_of(i), 0, 0)),
                   pl.BlockSpec((1, HG_HEADS, HG_DK, HG_DV), lambda i: (seq_of(i), 0, 0, 0))],
        out_shape=[jax.ShapeDtypeStruct((n, D_MODEL), F32), jax.ShapeDtypeStruct((n + n_tail, ROW_W), F32),
                   jax.ShapeDtypeStruct((n, ROUTE_LANES), F32),
                   jax.ShapeDtypeStruct((batch, HIST_ROWS, CONV_DIM), F32),
                   jax.ShapeDtypeStruct((batch, HG_HEADS, HG_DK, HG_DV), F32)],
        scratch_shapes=[pltpu.VMEM((HIST_ROWS + tb, CONV_DIM), F32),
                        pltpu.VMEM((SUBLANES, span, CONV_DIM), F32),
                        pltpu.VMEM((tb, CONV_DIM), F32),
                        pltpu.VMEM((HG_HEADS, HG_DV, HG_DK), F32)]
                       + [pltpu.VMEM((tb, HG_DIM), F32)] * 8
                       + [pltpu.VMEM((CHUNK, LANES), F32)] * HG_HEADS
                       + [pltpu.VMEM((TABLE_ROWS, LANES), F32)] * HG_HEADS,
        compiler_params=pltpu.CompilerParams(dimension_semantics=("arbitrary",), vmem_limit_bytes=VMEM_LIMIT),
        name="prompt_mixer",
    )(x2, *consts, h2_tail)


def _sample_proj_kernel(x_ref, gmix_ref, win_ref, proj_ref):
    h = _rms(x_ref[...], gmix_ref[...]).astype(BF)
    proj_ref[...] = _dot(h, win_ref[...])


def _sample_proj(xs, wts):
    n = xs.shape[0]
    return pl.pallas_call(
        _sample_proj_kernel,
        grid=(1,),
        in_specs=[_const_spec(xs.shape), _const_spec(wts['gmix'].shape), _const_spec(wts['w_in'].shape)],
        out_specs=pl.BlockSpec((n, IN_COLS), lambda i: (0, 0)),
        out_shape=jax.ShapeDtypeStruct((n, IN_COLS), F32),
        compiler_params=pltpu.CompilerParams(vmem_limit_bytes=VMEM_LIMIT),
        name="sample_proj",
    )(xs, wts['gmix'], wts['w_in'])


def _split3(x):
    hi = x.astype(BF).astype(F32)
    mid = (x - hi).astype(BF).astype(F32)
    lo = ((x - hi) - mid).astype(BF).astype(F32)
    return hi, mid, lo


def _sample_state_kernel(q_ref, f_ref, v_ref, lbp_ref, hng_ref, lane_masks_ref, s_ref, snew_ref, o_ref, *, bs):
    assert bs == SUBLANES
    lb = _lower_bound(lbp_ref[...])
    qf = _silu(q_ref[...]) * (HG_DK ** -0.5)
    fg = lb + (1.0 - lb) * _sigmoid(f_ref[...])
    kf = 1.0 - fg
    v = v_ref[...]
    ones = jnp.ones((bs, HG_DV), F32)
    zeros = jnp.zeros((bs, HG_DV), F32)
    n_groups = 9
    pad = jnp.zeros((LANES - n_groups * bs, LANES), F32)
    for hd in range(HG_HEADS):
        cs = slice(hd * HG_DK, (hd + 1) * HG_DK)
        f3, k3, q3, v3 = _split3(fg[:, cs]), _split3(kf[:, cs]), _split3(qf[:, cs]), _split3(v[:, cs])
        left = jnp.concatenate([f3[0], f3[1], f3[2], k3[0], k3[1], k3[0], q3[0], q3[1], q3[2], pad], axis=0)
        left_t = left.T.astype(BF)
        right = jnp.concatenate(
            [jnp.concatenate(blk, axis=1) for blk in
             [(ones, zeros, zeros)] * 3 + [(zeros, v3[0], zeros), (zeros, v3[0], zeros), (zeros, v3[1], zeros)]
             + [(zeros, zeros, ones)] * 3] + [jnp.concatenate((pad, pad, pad), axis=1)], axis=0).astype(BF)
        o_rows = []
        for r in range(bs):
            prod = _dot(left_t * lane_masks_ref[r], right)
            s_new = prod[:, :HG_DV] * s_ref[r, hd] + prod[:, HG_DV:2 * HG_DV]
            snew_ref[r, hd] = s_new
            o_rows.append(jnp.sum(prod[:, 2 * HG_DV:] * s_new, axis=0, keepdims=True))
        o_ref[:, cs] = _head_norm(jnp.concatenate(o_rows, axis=0), hng_ref[:, cs])


def _lane_masks():
    r = jnp.arange(SUBLANES)[:, None, None]
    lane = jnp.arange(LANES)[None, None, :] % SUBLANES
    return jnp.broadcast_to(lane == r, (SUBLANES, LANES, LANES)).astype(BF)


def _sample_state(proj, state, wts, bs):
    n = proj.shape[0]
    col_spec = lambda off: pl.BlockSpec((bs, HG_DIM), lambda i, off=off: (i, off // HG_DIM))
    st_spec = pl.BlockSpec((bs, HG_HEADS, HG_DK, HG_DV), lambda i: (i, 0, 0, 0))
    return pl.pallas_call(
        functools.partial(_sample_state_kernel, bs=bs),
        grid=(n // bs,),
        in_specs=[col_spec(OFF_Q), col_spec(OFF_F), col_spec(OFF_V),
                  _const_spec(wts['lb_param'].shape), _const_spec(wts['hn_g'].shape),
                  _const_spec((SUBLANES, LANES, LANES)), st_spec],
        out_specs=[st_spec, pl.BlockSpec((bs, HG_DIM), lambda i: (i, 0))],
        out_shape=[jax.ShapeDtypeStruct(state.shape, F32), jax.ShapeDtypeStruct((n, HG_DIM), F32)],
        compiler_params=pltpu.CompilerParams(dimension_semantics=("arbitrary",), vmem_limit_bytes=VMEM_LIMIT),
        name="sample_state",
    )(proj, proj, proj, wts['lb_param'], wts['hn_g'], _lane_masks(), state)


def _sample_tail_kernel(x_ref, proj_ref, o_ref, cst_ref, bg_ref, cw_ref, cb_ref, lng_ref, lnb_ref, wco_ref,
                        who_ref, wout_ref, gffn_ref, wr_ref, x1_ref, h2_ref, route_ref, nconv_ref):
    keep = (CONV_WIDTH - 2) * CONV_DIM
    u = proj_ref[:, OFF_CV:OFF_CV + CONV_DIM] * _sigmoid(proj_ref[:, OFF_CG:OFF_CG + CONV_DIM])
    acc = u * cw_ref[pl.ds(CONV_WIDTH - 1, 1), :]
    for j in range(CONV_WIDTH - 1):
        acc = acc + cst_ref[:, j * CONV_DIM:(j + 1) * CONV_DIM] * cw_ref[pl.ds(j, 1), :]
    nconv_ref[:, :keep] = cst_ref[:, CONV_DIM:]
    nconv_ref[:, keep:] = u
    c = _conv_post(acc, cb_ref[...], lng_ref[...], lnb_ref[...])
    y_a = _dot(c.astype(BF), wco_ref[...])
    og = proj_ref[:, OFF_OG:OFF_OG + HG_DIM]
    y_b = _dot((o_ref[...] * _silu(og)).astype(BF), who_ref[...])
    m = (_sigmoid(proj_ref[:, OFF_GA:OFF_GA + D_MODEL] + bg_ref[:, :D_MODEL]) * y_a
         + _sigmoid(proj_ref[:, OFF_GB:OFF_GB + D_MODEL] + bg_ref[:, D_MODEL:]) * y_b)
    x1, h2, route = _mixer_tail(x_ref[...], m, wout_ref[...], gffn_ref[...], wr_ref[...])
    x1_ref[...] = x1
    h2_ref[...] = h2
    route_ref[...] = route


def _sample_tail(xs, proj, o, conv_state2, wts):
    n = xs.shape[0]
    ins = [xs, proj, o, conv_state2, wts['b_gates'], wts['conv_w'], wts['conv_b'], wts['ln_g'], wts['ln_b'],
           wts['w_conv_out'], wts['w_hgrn_out'], wts['w_out'], wts['gffn'], wts['w_router']]
    full = lambda shape: pl.BlockSpec(shape, lambda i: (0,) * len(shape))
    return pl.pallas_call(
        _sample_tail_kernel,
        grid=(1,),
        in_specs=[_const_spec(a.shape) for a in ins],
        out_specs=[full((n, D_MODEL)), full((n, ROW_W)), full((n, ROUTE_LANES)), full(conv_state2.shape)],
        out_shape=[jax.ShapeDtypeStruct((n, D_MODEL), F32), jax.ShapeDtypeStruct((n, ROW_W), F32),
                   jax.ShapeDtypeStruct((n, ROUTE_LANES), F32), jax.ShapeDtypeStruct(conv_state2.shape, F32)],
        compiler_params=pltpu.CompilerParams(vmem_limit_bytes=VMEM_LIMIT),
        name="sample_tail",
    )(*ins)


def _plan_kernel(slab_p_ref, slab_s_ref, pos_ref, tinfo_ref, e_ref, pre_ref, *, tm, n_tok, max_tiles):
    n_p = slab_p_ref.shape[0]
    nblk = n_tok // LANES
    blk_p = n_p // LANES
    eidx = lax.broadcasted_iota(jnp.int32, (N_EXPERTS, LANES), 0).astype(F32)
    ti = lax.broadcasted_iota(jnp.int32, (LANES, LANES), 0)
    si = lax.broadcasted_iota(jnp.int32, (LANES, LANES), 1)
    before = (ti < si).astype(BF)

    def onehots(b):
        c0 = pl.multiple_of(b * LANES, LANES)
        e1 = e_ref[0:1, pl.ds(c0, LANES)]
        e2 = e_ref[1:2, pl.ds(c0, LANES)]
        return (e1 == eidx).astype(F32), (e2 == eidx).astype(F32), c0

    def transpose_block(slab_ref, b_local, b_global):
        r0 = pl.multiple_of(b_local * LANES, LANES)
        c0 = pl.multiple_of(b_global * LANES, LANES)
        e_ref[:, pl.ds(c0, LANES)] = slab_ref[pl.ds(r0, LANES), :].T[0:SUBLANES, :]

    def load_p(b, c):
        transpose_block(slab_p_ref, b, b)
        return c

    lax.fori_loop(0, blk_p, load_p, 0)

    def load_s(b, c):
        transpose_block(slab_s_ref, b, b + blk_p)
        return c

    lax.fori_loop(0, nblk - blk_p, load_s, 0)

    def count(b, carry):
        h1, h2, c0 = onehots(b)
        h = h1 + h2
        pre_ref[:, pl.ds(c0, LANES)] = _dot(h.astype(BF), before) + carry
        return carry + jnp.sum(h, axis=1, keepdims=True)

    counts = lax.fori_loop(0, nblk, count, jnp.zeros((N_EXPERTS, 1), F32))
    tiles_per = jnp.floor((counts + (tm - 1)) * (1.0 / tm))
    ei = lax.broadcasted_iota(jnp.int32, (N_EXPERTS, N_EXPERTS), 0)
    ej = lax.broadcasted_iota(jnp.int32, (N_EXPERTS, N_EXPERTS), 1)
    upto = (ej <= ei).astype(BF)
    tile_end = _dot(upto, jnp.broadcast_to(tiles_per, (N_EXPERTS, LANES)).astype(BF))[:, 0:1]
    starts = (tile_end - tiles_per) * tm

    def place(b, c):
        h1, h2, c0 = onehots(b)
        dest = pre_ref[:, pl.ds(c0, LANES)] + starts
        pos_ref[0:1, pl.ds(c0, LANES)] = jnp.sum(h1 * dest, axis=0, keepdims=True).astype(jnp.int32)
        pos_ref[1:2, pl.ds(c0, LANES)] = jnp.sum(h2 * dest, axis=0, keepdims=True).astype(jnp.int32)
        return c

    pos_ref[...] = jnp.zeros_like(pos_ref)
    lax.fori_loop(0, nblk, place, 0)

    tile = lax.broadcasted_iota(jnp.int32, (N_EXPERTS, max_tiles), 1).astype(F32)
    t_exp = jnp.sum((tile_end <= tile).astype(F32), axis=0, keepdims=True)
    t_exp = jnp.minimum(t_exp, N_EXPERTS - 1.0).astype(jnp.int32)
    n_used = jnp.broadcast_to(tile_end[N_EXPERTS - 1:N_EXPERTS, :], (1, max_tiles)).astype(jnp.int32)
    row = lax.broadcasted_iota(jnp.int32, (SUBLANES, max_tiles), 0)
    tinfo_ref[...] = jnp.where(row == 0, t_exp, jnp.where(row == 1, n_used, 0))


def _plan(slab_p, slab_s, tm, max_tiles):
    n_tok = slab_p.shape[0] + slab_s.shape[0]
    assert slab_p.shape[0] % LANES == 0 and slab_s.shape[0] % LANES == 0
    vm = pl.BlockSpec(memory_space=pltpu.VMEM)
    return pl.pallas_call(
        functools.partial(_plan_kernel, tm=tm, n_tok=n_tok, max_tiles=max_tiles),
        in_specs=[vm, vm],
        out_specs=[vm, vm],
        out_shape=[jax.ShapeDtypeStruct((SUBLANES, n_tok), jnp.int32),
                   jax.ShapeDtypeStruct((SUBLANES, max_tiles), jnp.int32)],
        scratch_shapes=[pltpu.VMEM((SUBLANES, n_tok), F32), pltpu.VMEM((N_EXPERTS, n_tok), F32)],
        compiler_params=pltpu.CompilerParams(vmem_limit_bytes=VMEM_LIMIT),
        name="route_plan",
    )(slab_p, slab_s)


def _invert_kernel(pos_ref, src_ref, *, n_pairs, n_rows):
    def clear(j, c):
        src_ref[j] = 0
        return c

    def put(j, c):
        src_ref[pos_ref[j]] = j // 2
        return c

    lax.fori_loop(0, n_rows, clear, 0, unroll=8)
    lax.fori_loop(0, n_pairs, put, 0, unroll=8)


def _invert(pos_flat, n_rows):
    grid_spec = pltpu.PrefetchScalarGridSpec(
        num_scalar_prefetch=1, grid=(1,), in_specs=[],
        out_specs=pl.BlockSpec(memory_space=pltpu.SMEM))
    return pl.pallas_call(
        functools.partial(_invert_kernel, n_pairs=pos_flat.shape[0], n_rows=n_rows),
        grid_spec=grid_spec,
        out_shape=jax.ShapeDtypeStruct((n_rows,), jnp.int32),
        name="invert_plan",
    )(pos_flat)


def _expert_kernel(te_ref, nt_ref, src_ref, h_ref, wg_ref, wu_ref, wd_ref, y_ref,
                   xbuf, wg_b, wu_b, wd_b, sems, *, tm):
    i = pl.program_id(0)
    n_used = nt_ref[0]
    slot = i % 2

    def request(tile, dst, r):
        pltpu.make_async_copy(h_ref.at[pl.ds(src_ref[tile * tm + r], 1), :],
                              xbuf.at[dst, pl.ds(r, 1), :], sems.at[dst]).start()

    def receive(dst):
        for r in range(tm):
            pltpu.make_async_copy(h_ref.at[pl.ds(0, 1), :], xbuf.at[dst, pl.ds(r, 1), :], sems.at[dst]).wait()

    @pl.when(i == 0)
    def _():
        def first(r, c):
            request(0, 0, r)
            return c

        lax.fori_loop(0, tm, first, 0, unroll=8)

    @pl.when(i < n_used)
    def _():
        receive(slot)
        changed = jnp.logical_or(i == 0, te_ref[i] != te_ref[jnp.maximum(i - 1, 0)])

        @pl.when(changed)
        def _():
            wg_b[...] = wg_ref[0].astype(BF)
            wu_b[...] = wu_ref[0].astype(BF)
            wd_b[...] = wd_ref[0].astype(BF)

        nxt = i + 1
        third = -(-tm // 3)
        xb = xbuf[slot].astype(BF)
        gate = _dot(xb, wg_b[...])
        for r in range(0, third):
            request(nxt, 1 - slot, r)
        up = _dot(xb, wu_b[...])
        for r in range(third, 2 * third):
            request(nxt, 1 - slot, r)
        y_ref[...] = _dot((_silu(gate) * up).astype(BF), wd_b[...])
        for r in range(2 * third, tm):
            request(nxt, 1 - slot, r)

    @pl.when(i == n_used - 1)
    def _():
        receive(1 - slot)

    @pl.when(i >= n_used)
    def _():
        y_ref[...] = jnp.zeros_like(y_ref)


def _experts(h2, tile_expert, n_tiles, src, wg, wu, wd, tm, max_tiles):
    grid_spec = pltpu.PrefetchScalarGridSpec(
        num_scalar_prefetch=3,
        grid=(max_tiles,),
        in_specs=[pl.BlockSpec(memory_space=pl.ANY),
                  pl.BlockSpec((1, D_MODEL, EXPERT_FF), lambda i, te, nt, src: (te[i], 0, 0)),
                  pl.BlockSpec((1, D_MODEL, EXPERT_FF), lambda i, te, nt, src: (te[i], 0, 0)),
                  pl.BlockSpec((1, EXPERT_FF, D_MODEL), lambda i, te, nt, src: (te[i], 0, 0))],
        out_specs=pl.BlockSpec((tm, D_MODEL), lambda i, te, nt, src: (i, 0)),
        scratch_shapes=[pltpu.VMEM((2, tm, ROW_W), F32),
                        pltpu.VMEM((D_MODEL, EXPERT_FF), BF), pltpu.VMEM((D_MODEL, EXPERT_FF), BF),
                        pltpu.VMEM((EXPERT_FF, D_MODEL), BF),
                        pltpu.SemaphoreType.DMA((2,))])
    return pl.pallas_call(
        functools.partial(_expert_kernel, tm=tm),
        grid_spec=grid_spec,
        out_shape=jax.ShapeDtypeStruct((max_tiles * tm, D_MODEL), F32),
        compiler_params=pltpu.CompilerParams(dimension_semantics=("arbitrary",), vmem_limit_bytes=VMEM_LIMIT),
        name="experts",
    )(tile_expert, n_tiles, src, h2, wg, wu, wd)


def _combine_kernel(pos_ref, x1_ref, route_ref, gfin_ref, ys_ref, y_ref, buf0, buf1, sem, *, tb, tok0):
    i = pl.program_id(0)

    def copies(r):
        base = 2 * (tok0 + i * tb + r)
        return (pltpu.make_async_copy(ys_ref.at[pl.ds(pos_ref[base], 1), :], buf0.at[pl.ds(r, 1), :], sem),
                pltpu.make_async_copy(ys_ref.at[pl.ds(pos_ref[base + 1], 1), :], buf1.at[pl.ds(r, 1), :], sem))

    def issue(r, c):
        c0, c1 = copies(r)
        c0.start()
        c1.start()
        return c

    def drain(r, c):
        c0, c1 = copies(r)
        c0.wait()
        c1.wait()
        return c

    lax.fori_loop(0, tb, issue, 0, unroll=8)
    lax.fori_loop(0, tb, drain, 0, unroll=8)
    route = route_ref[...]
    out = x1_ref[...] + (route[:, 2:3] * buf0[...] + route[:, 3:4] * buf1[...])
    y_ref[...] = _rms(out, gfin_ref[...])


def _combine(pos_flat, x1, route, gfin, ys, tb, tok0):
    n = x1.shape[0]
    grid_spec = pltpu.PrefetchScalarGridSpec(
        num_scalar_prefetch=1,
        grid=(n // tb,),
        in_specs=[pl.BlockSpec((tb, D_MODEL), lambda i, p: (i, 0)),
                  pl.BlockSpec((tb, ROUTE_LANES), lambda i, p: (i, 0)),
                  pl.BlockSpec((1, D_MODEL), lambda i, p: (0, 0)),
                  pl.BlockSpec(memory_space=pl.ANY)],
        out_specs=pl.BlockSpec((tb, D_MODEL), lambda i, p: (i, 0)),
        scratch_shapes=[pltpu.VMEM((tb, D_MODEL), F32), pltpu.VMEM((tb, D_MODEL), F32),
                        pltpu.SemaphoreType.DMA(())])
    return pl.pallas_call(
        functools.partial(_combine_kernel, tb=tb, tok0=tok0),
        grid_spec=grid_spec,
        out_shape=jax.ShapeDtypeStruct((n, D_MODEL), F32),
        compiler_params=pltpu.CompilerParams(dimension_semantics=("arbitrary",), vmem_limit_bytes=VMEM_LIMIT),
        name="combine",
    )(pos_flat, x1, route, gfin, ys)


def kernel(x_prompt, x_sample, state_conv, state_hgrn, norm_mix_g, w_in, b_gates, conv_dw_w, conv_dw_b,
           conv_ln_g, conv_ln_b, w_conv_out, hgrn_lb_param, hgrn_norm_g, w_hgrn_out, w_out, norm_ffn_g,
           w_router_group, w_router_expert, w_expert_gate, w_expert_up, w_expert_down, norm_final_g):
    batch, seq, _ = x_prompt.shape
    dec_batch = x_sample.shape[0]
    assert x_sample.shape[1] == 1 and w_in.shape[0] == 1
    tb = min(256, seq)
    tm = 256
    bs = min(8, dec_batch)
    n_p = batch * seq
    n_all = n_p + dec_batch
    tbc = min(512, n_p)
    assert seq % tb == 0 and tb % CHUNK == 0 and dec_batch % bs == 0 and n_p % tbc == 0

    w_router = jnp.concatenate(
        [w_router_group[0], w_router_expert[0],
         jnp.zeros((D_MODEL, ROUTE_LANES - N_GROUPS - N_EXPERTS), F32)], axis=1).astype(BF)
    wts = dict(gmix=norm_mix_g, w_in=w_in[0].astype(BF), b_gates=b_gates, conv_w=conv_dw_w[0], conv_b=conv_dw_b,
               ln_g=conv_ln_g, ln_b=conv_ln_b, w_conv_out=w_conv_out[0].astype(BF), lb_param=hgrn_lb_param,
               hn_g=hgrn_norm_g, w_hgrn_out=w_hgrn_out[0].astype(BF), w_out=w_out[0].astype(BF),
               gffn=norm_ffn_g, w_router=w_router)

    xs_tok = x_sample.reshape(dec_batch, D_MODEL)
    proj_s = _sample_proj(xs_tok, wts)
    nhgrn_s, o_s = _sample_state(proj_s, state_hgrn[0], wts, bs)
    conv2 = state_conv[0].reshape(dec_batch, (CONV_WIDTH - 1) * CONV_DIM)
    x1_s, h2_s, route_s, nconv_s = _sample_tail(xs_tok, proj_s, o_s, conv2, wts)

    x1_p, h2_all, route_p, nconv_p, nhgrn_p = _prompt_mixer(x_prompt.reshape(n_p, D_MODEL), h2_s,
                                                            batch, seq, tb, wts)

    max_tiles = -(-((2 * n_all) // tm + N_EXPERTS) // SUBLANES) * SUBLANES
    pos, tinfo = _plan(route_p, route_s, tm, max_tiles)
    pos_flat = pos[0:2].T.reshape(-1)
    tile_expert, n_tiles = tinfo[0], tinfo[1, 0:1]

    src = _invert(pos_flat, (max_tiles + 1) * tm)
    ys = _experts(h2_all, tile_expert, n_tiles, src, w_expert_gate[0], w_expert_up[0], w_expert_down[0],
                  tm, max_tiles)

    gfin = norm_final_g.reshape(1, D_MODEL)
    y_p = _combine(pos_flat, x1_p, route_p, gfin, ys, tbc, 0)
    y_s = _combine(pos_flat, x1_s, route_s, gfin, ys, dec_batch, n_p)

    return (y_p.reshape(batch, seq, D_MODEL), y_s.reshape(dec_batch, 1, D_MODEL),
            nconv_p[None, :, HIST_PAD:, :], nhgrn_p[None],
            nconv_s.reshape(1, dec_batch, CONV_WIDTH - 1, CONV_DIM), nhgrn_s[None])
```

```python
import functools

import jax
import jax.numpy as jnp
from jax import lax
from jax.experimental import pallas as pl
from jax.experimental.pallas import tpu as pltpu

D_MODEL = 1024
CONV_DIM = D_MODEL // 2
CONV_WIDTH = 31
HG_HEADS = 8
HG_DK = 128
HG_DV = 128
HG_DIM = HG_HEADS * HG_DK
N_GROUPS = 4
EXPERTS_PER_GROUP = 8
N_EXPERTS = N_GROUPS * EXPERTS_PER_GROUP
EXPERT_FF = D_MODEL // 2
EPS = 1e-6
IN_COLS = 2 * CONV_DIM + 4 * HG_DIM + 2 * D_MODEL

OFF_CV, OFF_CG = 0, CONV_DIM
OFF_Q = 2 * CONV_DIM
OFF_F = OFF_Q + HG_DIM
OFF_V = OFF_F + HG_DIM
OFF_OG = OFF_V + HG_DIM
OFF_GA = OFF_OG + HG_DIM
OFF_GB = OFF_GA + D_MODEL

SUBLANES = 8
LANES = 128
HIST_ROWS = 32
HIST_PAD = HIST_ROWS - (CONV_WIDTH - 1)
CHUNK = 64
NBLK = CHUNK // SUBLANES
CONV_ROWS = 32
PROJ_COLS = 256
ROUTE_LANES = 128
ROW_W = D_MODEL
VMEM_LIMIT = 56 * 1024 * 1024

BF = jnp.bfloat16
F32 = jnp.float32


def _dot(a, b):
    return jnp.dot(a, b, preferred_element_type=F32)


def _dot_nt(a, b):
    return lax.dot_general(a, b, (((1,), (1,)), ((), ())), preferred_element_type=F32)


def _dot_tn(a, b, precision=None):
    return lax.dot_general(a, b, (((0,), (0,)), ((), ())), preferred_element_type=F32,
                           precision=precision)


NEG_LOG2E = -1.4426950408889634


def _sigmoid(x):
    return 1.0 / (1.0 + jnp.exp2(x * NEG_LOG2E))


def _silu(x):
    return x * _sigmoid(x)


def _rms(xf, g):
    return xf * lax.rsqrt(jnp.mean(xf * xf, axis=-1, keepdims=True) + EPS) * g


def _lower_bound(lb_param):
    m = jnp.max(lb_param, axis=0, keepdims=True)
    e = jnp.exp(lb_param - m)
    return e[0:1] / jnp.sum(e, axis=0, keepdims=True)


def _conv_post(c, conv_b, ln_g, ln_b):
    c = c + conv_b
    mu = jnp.mean(c, axis=-1, keepdims=True)
    d = c - mu
    var = jnp.mean(d * d, axis=-1, keepdims=True)
    return _silu(d * lax.rsqrt(var + EPS) * ln_g + ln_b)


def _head_norm(o, g):
    return o * lax.rsqrt(jnp.mean(o * o, axis=-1, keepdims=True) + EPS) * g


def _route(logits):
    col = lax.broadcasted_iota(jnp.int32, logits.shape, 1)
    big = jnp.int32(1 << 20)
    neg = jnp.float32(-jnp.inf)
    gmask = col < N_GROUPS
    lg = jnp.where(gmask, logits, neg)
    gmax = jnp.max(lg, axis=-1, keepdims=True)
    gsum = jnp.sum(jnp.where(gmask, jnp.exp(lg - gmax), 0.0), axis=-1, keepdims=True)
    gval = 1.0 / gsum
    gidx = jnp.min(jnp.where(lg == gmax, col, big), axis=-1, keepdims=True)
    lo = N_GROUPS + EXPERTS_PER_GROUP * gidx
    emask = (col >= lo) & (col < lo + EXPERTS_PER_GROUP)
    el = jnp.where(emask, logits, neg)
    m1 = jnp.max(el, axis=-1, keepdims=True)
    i1 = jnp.min(jnp.where(el == m1, col, big), axis=-1, keepdims=True)
    el2 = jnp.where(col == i1, neg, el)
    m2 = jnp.max(el2, axis=-1, keepdims=True)
    i2 = jnp.min(jnp.where(el2 == m2, col, big), axis=-1, keepdims=True)
    r = jnp.exp(m2 - m1)
    w1 = gval / (1.0 + r)
    w2 = gval * r / (1.0 + r)
    e1 = (i1 - N_GROUPS).astype(F32)
    e2 = (i2 - N_GROUPS).astype(F32)
    return jnp.where(col == 0, e1, jnp.where(col == 1, e2, jnp.where(col == 2, w1, jnp.where(col == 3, w2, 0.0))))


LEVELS = (NBLK // 2, NBLK // 4, NBLK // 8)
ROW_TOT, ROW_QE, ROW_KD, ROW_LEVEL = 0, NBLK, 2 * NBLK, 3 * NBLK
TABLE_ROWS = ROW_LEVEL + 2 * NBLK * len(LEVELS)


def _row_bcast(ref, row):
    return jnp.broadcast_to(ref[pl.ds(row, 1), :], (SUBLANES, LANES))


def _per_block(ref, base):
    return jnp.concatenate([_row_bcast(ref, base + j) for j in range(NBLK)], axis=0)


def _block_rows(ref, c):
    return jnp.concatenate([_row_bcast(ref, j * SUBLANES + c) for j in range(NBLK)], axis=0)


def _sparse_tile(x, ref, base, blocks):
    groups = []
    for g in range(0, NBLK, 2):
        if g not in blocks and g + 1 not in blocks:
            groups.append(jnp.zeros((2 * SUBLANES, LANES), BF))
            continue
        halves = [x[j * SUBLANES:(j + 1) * SUBLANES, :] * _row_bcast(ref, base + j) if j in blocks
                  else jnp.zeros((SUBLANES, LANES), F32) for j in (g, g + 1)]
        groups.append(jnp.concatenate(halves, axis=0).astype(BF))
    return jnp.concatenate(groups, axis=0)


def _chunk_cumsum(l2, bl_ref):
    rin = lax.broadcasted_iota(jnp.int32, (CHUNK, LANES), 0) % SUBLANES
    bl = l2
    for s in (1, 2, 4):
        bl = bl + jnp.where(rin >= s, pltpu.roll(bl, s, axis=0), 0.0)
    bl_ref[...] = bl
    return bl


def _chunk_table(bl_ref, tab_ref):
    tot = bl_ref[pl.ds(SUBLANES - 1, NBLK, stride=SUBLANES), :]
    brow = lax.broadcasted_iota(jnp.int32, (NBLK, LANES), 0)
    rb = tot
    for s in (1, 2, 4):
        rb = rb + jnp.where(brow >= s, pltpu.roll(rb, s, axis=0), 0.0)
    rb_prev = rb - tot
    total = rb[NBLK - 1:NBLK, :]

    tab_ref[pl.ds(ROW_TOT, NBLK), :] = tot
    tab_ref[pl.ds(ROW_QE, NBLK), :] = jnp.exp2(rb_prev)
    tab_ref[pl.ds(ROW_KD, NBLK), :] = jnp.exp2(total - rb)
    for lv, cb in enumerate(LEVELS):
        edges = [(j // (2 * cb)) * (2 * cb) + cb - 1 for j in range(NBLK)]
        rb_edge = jnp.concatenate([rb[e:e + 1, :] for e in edges], axis=0)
        base = ROW_LEVEL + 2 * NBLK * lv
        tab_ref[pl.ds(base, NBLK), :] = jnp.exp2(jnp.minimum(rb_prev - rb_edge, 0.0))
        tab_ref[pl.ds(base + NBLK, NBLK), :] = jnp.exp2(jnp.minimum(rb_edge - rb, 0.0))
    return total


def _chunk_scaled(q, k, bl, tab_ref):
    return q * jnp.exp2(bl), k * jnp.exp2(_per_block(tab_ref, ROW_TOT) - bl)


def _chunk_inter_operands(qp, kp, tab_ref):
    return (qp * _per_block(tab_ref, ROW_QE)).astype(BF), (kp * _per_block(tab_ref, ROW_KD)).astype(BF)


def _chunk_pair_operands(qp, kp, tab_ref):
    q_tiles, k_tiles = [], []
    for lv, cb in enumerate(LEVELS):
        base = ROW_LEVEL + 2 * NBLK * lv
        for p0 in range(0, NBLK, 2 * cb):
            q_tiles.append(_sparse_tile(qp, tab_ref, base, range(p0 + cb, p0 + 2 * cb)))
            k_tiles.append(_sparse_tile(kp, tab_ref, base + NBLK, range(p0, p0 + cb)))
    return jnp.concatenate(q_tiles, axis=1), jnp.concatenate(k_tiles, axis=1)


def _chunk_block_operands(q, k, bl, row_masks, bl_ref):
    k_b = k.astype(BF)
    lhs = [(q * jnp.exp2(jnp.minimum(bl - _block_rows(bl_ref, c), 0.0))).astype(BF) for c in range(SUBLANES)]
    rhs = [k_b * row_masks[c] for c in range(SUBLANES)]
    return jnp.concatenate(lhs, axis=1), jnp.concatenate(rhs, axis=1)


def _chunk_products(qe, kdec, v_b, pair_ops, block_ops, st):
    return (_dot_nt(qe, st.astype(BF)), _dot_tn(v_b, kdec), _dot_nt(*pair_ops), _dot_nt(*block_ops))


def _chunk_output(o_inter, off_diag, diag, v_b):
    ti = lax.broadcasted_iota(jnp.int32, (CHUNK, CHUNK), 0)
    si = lax.broadcasted_iota(jnp.int32, (CHUNK, CHUNK), 1)
    scores = jnp.where((ti // SUBLANES == si // SUBLANES) & (si <= ti), diag, off_diag)
    return o_inter + _dot(scores.astype(BF), v_b)


def _mixer_tail(x, m, w_out, g_ffn, w_router):
    x1 = x + _dot(m.astype(BF), w_out)
    h2 = _rms(x1, g_ffn).astype(BF)
    route = _route(_dot(h2, w_router))
    return x1, h2.astype(F32), route


def _prompt_mixer_kernel(x_ref, gmix_ref, win_ref, bg_ref, cw_ref, cb_ref, lng_ref, lnb_ref, wco_ref,
                         lbp_ref, hng_ref, who_ref, wout_ref, gffn_ref, wr_ref, masks_ref,
                         x1_ref, h2_ref, route_ref, nconv_ref, nhgrn_ref,
                         hist_ref, phase_ref, conv_ref, st_ref, q_s, k_s, v_s, lf_s, o_s, og_s, ga_s, gb_s, *head_scr, tb):
    t = pl.program_id(1)

    @pl.when(t == 0)
    def _():
        hist_ref[pl.ds(0, HIST_ROWS), :] = jnp.zeros((HIST_ROWS, CONV_DIM), F32)
        st_ref[...] = jnp.zeros_like(st_ref)

    x = x_ref[...]
    h = _rms(x, gmix_ref[...]).astype(BF)

    cv = _dot(h, win_ref[:, OFF_CV:OFF_CV + CONV_DIM])
    cg = _dot(h, win_ref[:, OFF_CG:OFF_CG + CONV_DIM])
    hist_ref[pl.ds(HIST_ROWS, tb), :] = cv * _sigmoid(cg)

    lb = _lower_bound(lbp_ref[...])

    def project(seg, c0):
        z = _dot(h, win_ref[:, seg + c0:seg + c0 + PROJ_COLS])
        cols = slice(c0, c0 + PROJ_COLS)
        if seg == OFF_Q:
            q_s[:, cols] = _silu(z) * (HG_DK ** -0.5)
        elif seg == OFF_F:
            fg = lb[:, cols] + (1.0 - lb[:, cols]) * _sigmoid(z)
            lf_s[:, cols] = jnp.log2(fg)
            k_s[:, cols] = 1.0 - fg
        elif seg == OFF_V:
            v_s[:, cols] = z
        elif seg == OFF_OG:
            og_s[:, cols] = _silu(z)
        elif seg == OFF_GA:
            ga_s[:, cols] = _sigmoid(z + bg_ref[:, c0:c0 + PROJ_COLS])
        else:
            gb_s[:, cols] = _sigmoid(z + bg_ref[:, D_MODEL + c0:D_MODEL + c0 + PROJ_COLS])

    proj_jobs = [functools.partial(project, seg, c0)
                 for seg in (OFF_Q, OFF_F, OFF_V, OFF_OG, OFF_GA, OFF_GB) for c0 in range(0, HG_DIM, PROJ_COLS)]

    span = tb + SUBLANES * (-(-CONV_WIDTH // SUBLANES) - 1)

    def phase_copy(r):
        n = min(span, HIST_ROWS + tb - HIST_PAD - r)
        phase_ref[r, pl.ds(0, n), :] = hist_ref[pl.ds(HIST_PAD + r, n), :]

    def conv_rows(r0):
        acc = jnp.zeros((CONV_ROWS, CONV_DIM), F32)
        for j in range(CONV_WIDTH):
            w_j = jnp.broadcast_to(cw_ref[pl.ds(j, 1), :], (SUBLANES, CONV_DIM))
            acc = acc + (phase_ref[j % SUBLANES, pl.ds(r0 + j - j % SUBLANES, CONV_ROWS), :]
                         * jnp.concatenate([w_j] * (CONV_ROWS // SUBLANES), axis=0))
        conv_ref[pl.ds(r0, CONV_ROWS), :] = acc

    conv_jobs = ([functools.partial(phase_copy, r) for r in range(SUBLANES)]
                 + [functools.partial(conv_rows, r0) for r0 in range(0, tb, CONV_ROWS)])
    per_conv_job = -(-len(proj_jobs) // len(conv_jobs))
    for i, job in enumerate(conv_jobs):
        job()
        for pj in proj_jobs[i * per_conv_job:(i + 1) * per_conv_job]:
            pj()
    for pj in proj_jobs[len(conv_jobs) * per_conv_job:]:
        pj()

    tail = hist_ref[pl.ds(tb, HIST_ROWS), :]
    hist_ref[pl.ds(0, HIST_ROWS), :] = tail
    nconv_ref[0] = tail
    c = _conv_post(conv_ref[...], cb_ref[...], lng_ref[...], lnb_ref[...])
    y_a = _dot(c.astype(BF), wco_ref[...])

    row_masks = masks_ref[...]
    cols = [slice(hd * HG_DK, (hd + 1) * HG_DK) for hd in range(HG_HEADS)]
    bl_refs, tab_refs = head_scr[:HG_HEADS], head_scr[HG_HEADS:]
    heads = range(HG_HEADS)
    for r0 in range(0, tb, CHUNK):
        rows = pl.ds(r0, CHUNK)
        q = [q_s[rows, cols[hd]] for hd in heads]
        k = [k_s[rows, cols[hd]] for hd in heads]
        bl = [_chunk_cumsum(lf_s[rows, cols[hd]], bl_refs[hd]) for hd in heads]
        total = [_chunk_table(bl_refs[hd], tab_refs[hd]) for hd in heads]
        scaled = [_chunk_scaled(q[hd], k[hd], bl[hd], tab_refs[hd]) for hd in heads]
        inter = [_chunk_inter_operands(*scaled[hd], tab_refs[hd]) for hd in heads]
        pair_ops = [_chunk_pair_operands(*scaled[hd], tab_refs[hd]) for hd in heads]
        block_ops = [_chunk_block_operands(q[hd], k[hd], bl[hd], row_masks, bl_refs[hd]) for hd in heads]
        v_b = [v_s[rows, cols[hd]].astype(BF) for hd in heads]
        prods = [_chunk_products(*inter[hd], v_b[hd], pair_ops[hd], block_ops[hd], st_ref[hd]) for hd in heads]
        for hd in heads:
            o_inter, update, off_diag, diag = prods[hd]
            st_ref[hd] = st_ref[hd] * jnp.exp2(total[hd]) + update
            o = _chunk_output(o_inter, off_diag, diag, v_b[hd])
            o_s[rows, cols[hd]] = _head_norm(o, hng_ref[:, cols[hd]])
    for hd in range(HG_HEADS):
        nhgrn_ref[0, hd] = st_ref[hd].T

    y_b = _dot((o_s[...] * og_s[...]).astype(BF), who_ref[...])
    x1, h2, route = _mixer_tail(x, ga_s[...] * y_a + gb_s[...] * y_b, wout_ref[...], gffn_ref[...], wr_ref[...])
    x1_ref[...] = x1
    h2_ref[...] = h2
    route_ref[...] = route


def _const_spec(shape):
    nd = len(shape)
    return pl.BlockSpec(shape, lambda *_: (0,) * nd, pipeline_mode=pl.Buffered(1))


def _row_masks():
    c = jnp.arange(SUBLANES)[:, None, None]
    r = jnp.arange(CHUNK)[None, :, None] % SUBLANES
    return jnp.broadcast_to(r == c, (SUBLANES, CHUNK, LANES)).astype(BF)


def _prompt_mixer(x2, batch, seq, tb, wts):
    nt = seq // tb
    n = batch * seq
    row_spec = lambda w: pl.BlockSpec((tb, w), lambda b, t: (b * nt + t, 0))
    consts = [wts['gmix'], wts['w_in'], wts['b_gates'], wts['conv_w'], wts['conv_b'], wts['ln_g'], wts['ln_b'],
              wts['w_conv_out'], wts['lb_param'], wts['hn_g'], wts['w_hgrn_out'], wts['w_out'], wts['gffn'],
              wts['w_router'], _row_masks()]
    span = tb + SUBLANES * (-(-CONV_WIDTH // SUBLANES) - 1)
    return pl.pallas_call(
        functools.partial(_prompt_mixer_kernel, tb=tb),
        grid=(batch, nt),
        in_specs=[row_spec(D_MODEL)] + [_const_spec(c.shape) for c in consts],
        out_specs=[row_spec(D_MODEL), row_spec(ROW_W), row_spec(ROUTE_LANES),
                   pl.BlockSpec((1, HIST_ROWS, CONV_DIM), lambda b, t: (b, 0, 0)),
                   pl.BlockSpec((1, HG_HEADS, HG_DK, HG_DV), lambda b, t: (b, 0, 0, 0))],
        out_shape=[jax.ShapeDtypeStruct((n, D_MODEL), F32), jax.ShapeDtypeStruct((n, ROW_W), F32),
                   jax.ShapeDtypeStruct((n, ROUTE_LANES), F32),
                   jax.ShapeDtypeStruct((batch, HIST_ROWS, CONV_DIM), F32),
                   jax.ShapeDtypeStruct((batch, HG_HEADS, HG_DK, HG_DV), F32)],
        scratch_shapes=[pltpu.VMEM((HIST_ROWS + tb, CONV_DIM), F32),
                        pltpu.VMEM((SUBLANES, span, CONV_DIM), F32),
                        pltpu.VMEM((tb, CONV_DIM), F32),
                        pltpu.VMEM((HG_HEADS, HG_DV, HG_DK), F32)]
                       + [pltpu.VMEM((tb, HG_DIM), F32)] * 8
                       + [pltpu.VMEM((CHUNK, LANES), F32)] * HG_HEADS
                       + [pltpu.VMEM((TABLE_ROWS, LANES), F32)] * HG_HEADS,
        compiler_params=pltpu.CompilerParams(dimension_semantics=("arbitrary", "arbitrary"),
                                             vmem_limit_bytes=VMEM_LIMIT),
        name="prompt_mixer",
    )(x2, *consts)


def _sample_proj_kernel(x_ref, gmix_ref, win_ref, proj_ref):
    h = _rms(x_ref[...], gmix_ref[...]).astype(BF)
    proj_ref[...] = _dot(h, win_ref[...])


def _sample_proj(xs, wts):
    n = xs.shape[0]
    return pl.pallas_call(
        _sample_proj_kernel,
        grid=(1,),
        in_specs=[_const_spec(xs.shape), _const_spec(wts['gmix'].shape), _const_spec(wts['w_in'].shape)],
        out_specs=pl.BlockSpec((n, IN_COLS), lambda i: (0, 0)),
        out_shape=jax.ShapeDtypeStruct((n, IN_COLS), F32),
        compiler_params=pltpu.CompilerParams(vmem_limit_bytes=VMEM_LIMIT),
        name="sample_proj",
    )(xs, wts['gmix'], wts['w_in'])


def _split3(x):
    hi = x.astype(BF).astype(F32)
    mid = (x - hi).astype(BF).astype(F32)
    lo = ((x - hi) - mid).astype(BF).astype(F32)
    return hi, mid, lo


def _sample_state_kernel(q_ref, f_ref, v_ref, lbp_ref, hng_ref, lane_masks_ref, s_ref, snew_ref, o_ref, *, bs):
    assert bs == SUBLANES
    lb = _lower_bound(lbp_ref[...])
    qf = _silu(q_ref[...]) * (HG_DK ** -0.5)
    fg = lb + (1.0 - lb) * _sigmoid(f_ref[...])
    kf = 1.0 - fg
    v = v_ref[...]
    ones = jnp.ones((bs, HG_DV), F32)
    zeros = jnp.zeros((bs, HG_DV), F32)
    n_groups = 9
    pad = jnp.zeros((LANES - n_groups * bs, LANES), F32)
    for hd in range(HG_HEADS):
        cs = slice(hd * HG_DK, (hd + 1) * HG_DK)
        f3, k3, q3, v3 = _split3(fg[:, cs]), _split3(kf[:, cs]), _split3(qf[:, cs]), _split3(v[:, cs])
        left = jnp.concatenate([f3[0], f3[1], f3[2], k3[0], k3[1], k3[0], q3[0], q3[1], q3[2], pad], axis=0)
        left_t = left.T.astype(BF)
        right = jnp.concatenate(
            [jnp.concatenate(blk, axis=1) for blk in
             [(ones, zeros, zeros)] * 3 + [(zeros, v3[0], zeros), (zeros, v3[0], zeros), (zeros, v3[1], zeros)]
             + [(zeros, zeros, ones)] * 3] + [jnp.concatenate((pad, pad, pad), axis=1)], axis=0).astype(BF)
        o_rows = []
        for r in range(bs):
            prod = _dot(left_t * lane_masks_ref[r], right)
            s_new = prod[:, :HG_DV] * s_ref[r, hd] + prod[:, HG_DV:2 * HG_DV]
            snew_ref[r, hd] = s_new
            o_rows.append(jnp.sum(prod[:, 2 * HG_DV:] * s_new, axis=0, keepdims=True))
        o_ref[:, cs] = _head_norm(jnp.concatenate(o_rows, axis=0), hng_ref[:, cs])


def _lane_masks():
    r = jnp.arange(SUBLANES)[:, None, None]
    lane = jnp.arange(LANES)[None, None, :] % SUBLANES
    return jnp.broadcast_to(lane == r, (SUBLANES, LANES, LANES)).astype(BF)


def _sample_state(proj, state, wts, bs):
    n = proj.shape[0]
    col_spec = lambda off: pl.BlockSpec((bs, HG_DIM), lambda i, off=off: (i, off // HG_DIM))
    st_spec = pl.BlockSpec((bs, HG_HEADS, HG_DK, HG_DV), lambda i: (i, 0, 0, 0))
    return pl.pallas_call(
        functools.partial(_sample_state_kernel, bs=bs),
        grid=(n // bs,),
        in_specs=[col_spec(OFF_Q), col_spec(OFF_F), col_spec(OFF_V),
                  _const_spec(wts['lb_param'].shape), _const_spec(wts['hn_g'].shape),
                  _const_spec((SUBLANES, LANES, LANES)), st_spec],
        out_specs=[st_spec, pl.BlockSpec((bs, HG_DIM), lambda i: (i, 0))],
        out_shape=[jax.ShapeDtypeStruct(state.shape, F32), jax.ShapeDtypeStruct((n, HG_DIM), F32)],
        compiler_params=pltpu.CompilerParams(dimension_semantics=("arbitrary",), vmem_limit_bytes=VMEM_LIMIT),
        name="sample_state",
    )(proj, proj, proj, wts['lb_param'], wts['hn_g'], _lane_masks(), state)


def _sample_tail_kernel(x_ref, proj_ref, o_ref, cst_ref, bg_ref, cw_ref, cb_ref, lng_ref, lnb_ref, wco_ref,
                        who_ref, wout_ref, gffn_ref, wr_ref, x1_ref, h2_ref, route_ref, nconv_ref):
    keep = (CONV_WIDTH - 2) * CONV_DIM
    u = proj_ref[:, OFF_CV:OFF_CV + CONV_DIM] * _sigmoid(proj_ref[:, OFF_CG:OFF_CG + CONV_DIM])
    acc = u * cw_ref[pl.ds(CONV_WIDTH - 1, 1), :]
    for j in range(CONV_WIDTH - 1):
        acc = acc + cst_ref[:, j * CONV_DIM:(j + 1) * CONV_DIM] * cw_ref[pl.ds(j, 1), :]
    nconv_ref[:, :keep] = cst_ref[:, CONV_DIM:]
    nconv_ref[:, keep:] = u
    c = _conv_post(acc, cb_ref[...], lng_ref[...], lnb_ref[...])
    y_a = _dot(c.astype(BF), wco_ref[...])
    og = proj_ref[:, OFF_OG:OFF_OG + HG_DIM]
    y_b = _dot((o_ref[...] * _silu(og)).astype(BF), who_ref[...])
    m = (_sigmoid(proj_ref[:, OFF_GA:OFF_GA + D_MODEL] + bg_ref[:, :D_MODEL]) * y_a
         + _sigmoid(proj_ref[:, OFF_GB:OFF_GB + D_MODEL] + bg_ref[:, D_MODEL:]) * y_b)
    x1, h2, route = _mixer_tail(x_ref[...], m, wout_ref[...], gffn_ref[...], wr_ref[...])
    x1_ref[...] = x1
    h2_ref[...] = h2
    route_ref[...] = route


def _sample_tail(xs, proj, o, conv_state2, wts):
    n = xs.shape[0]
    ins = [xs, proj, o, conv_state2, wts['b_gates'], wts['conv_w'], wts['conv_b'], wts['ln_g'], wts['ln_b'],
           wts['w_conv_out'], wts['w_hgrn_out'], wts['w_out'], wts['gffn'], wts['w_router']]
    full = lambda shape: pl.BlockSpec(shape, lambda i: (0,) * len(shape))
    return pl.pallas_call(
        _sample_tail_kernel,
        grid=(1,),
        in_specs=[_const_spec(a.shape) for a in ins],
        out_specs=[full((n, D_MODEL)), full((n, ROW_W)), full((n, ROUTE_LANES)), full(conv_state2.shape)],
        out_shape=[jax.ShapeDtypeStruct((n, D_MODEL), F32), jax.ShapeDtypeStruct((n, ROW_W), F32),
                   jax.ShapeDtypeStruct((n, ROUTE_LANES), F32), jax.ShapeDtypeStruct(conv_state2.shape, F32)],
        compiler_params=pltpu.CompilerParams(vmem_limit_bytes=VMEM_LIMIT),
        name="sample_tail",
    )(*ins)


def _plan_kernel(slab_p_ref, slab_s_ref, pos_ref, tinfo_ref, e_ref, pre_ref, *, tm, n_tok, max_tiles):
    n_p = slab_p_ref.shape[0]
    nblk = n_tok // LANES
    blk_p = n_p // LANES
    eidx = lax.broadcasted_iota(jnp.int32, (N_EXPERTS, LANES), 0).astype(F32)
    ti = lax.broadcasted_iota(jnp.int32, (LANES, LANES), 0)
    si = lax.broadcasted_iota(jnp.int32, (LANES, LANES), 1)
    before = (ti < si).astype(BF)

    def onehots(b):
        c0 = pl.multiple_of(b * LANES, LANES)
        e1 = e_ref[0:1, pl.ds(c0, LANES)]
        e2 = e_ref[1:2, pl.ds(c0, LANES)]
        return (e1 == eidx).astype(F32), (e2 == eidx).astype(F32), c0

    def transpose_block(slab_ref, b_local, b_global):
        r0 = pl.multiple_of(b_local * LANES, LANES)
        c0 = pl.multiple_of(b_global * LANES, LANES)
        e_ref[:, pl.ds(c0, LANES)] = slab_ref[pl.ds(r0, LANES), :].T[0:SUBLANES, :]

    def load_p(b, c):
        transpose_block(slab_p_ref, b, b)
        return c

    lax.fori_loop(0, blk_p, load_p, 0)

    def load_s(b, c):
        transpose_block(slab_s_ref, b, b + blk_p)
        return c

    lax.fori_loop(0, nblk - blk_p, load_s, 0)

    def count(b, carry):
        h1, h2, c0 = onehots(b)
        h = h1 + h2
        pre_ref[:, pl.ds(c0, LANES)] = _dot(h.astype(BF), before) + carry
        return carry + jnp.sum(h, axis=1, keepdims=True)

    counts = lax.fori_loop(0, nblk, count, jnp.zeros((N_EXPERTS, 1), F32))
    tiles_per = jnp.floor((counts + (tm - 1)) * (1.0 / tm))
    ei = lax.broadcasted_iota(jnp.int32, (N_EXPERTS, N_EXPERTS), 0)
    ej = lax.broadcasted_iota(jnp.int32, (N_EXPERTS, N_EXPERTS), 1)
    upto = (ej <= ei).astype(BF)
    tile_end = _dot(upto, jnp.broadcast_to(tiles_per, (N_EXPERTS, LANES)).astype(BF))[:, 0:1]
    starts = (tile_end - tiles_per) * tm

    def place(b, c):
        h1, h2, c0 = onehots(b)
        dest = pre_ref[:, pl.ds(c0, LANES)] + starts
        pos_ref[0:1, pl.ds(c0, LANES)] = jnp.sum(h1 * dest, axis=0, keepdims=True).astype(jnp.int32)
        pos_ref[1:2, pl.ds(c0, LANES)] = jnp.sum(h2 * dest, axis=0, keepdims=True).astype(jnp.int32)
        return c

    pos_ref[...] = jnp.zeros_like(pos_ref)
    lax.fori_loop(0, nblk, place, 0)

    tile = lax.broadcasted_iota(jnp.int32, (N_EXPERTS, max_tiles), 1).astype(F32)
    t_exp = jnp.sum((tile_end <= tile).astype(F32), axis=0, keepdims=True)
    t_exp = jnp.minimum(t_exp, N_EXPERTS - 1.0).astype(jnp.int32)
    n_used = jnp.broadcast_to(tile_end[N_EXPERTS - 1:N_EXPERTS, :], (1, max_tiles)).astype(jnp.int32)
    row = lax.broadcasted_iota(jnp.int32, (SUBLANES, max_tiles), 0)
    tinfo_ref[...] = jnp.where(row == 0, t_exp, jnp.where(row == 1, n_used, 0))


def _plan(slab_p, slab_s, tm, max_tiles):
    n_tok = slab_p.shape[0] + slab_s.shape[0]
    assert slab_p.shape[0] % LANES == 0 and slab_s.shape[0] % LANES == 0
    vm = pl.BlockSpec(memory_space=pltpu.VMEM)
    return pl.pallas_call(
        functools.partial(_plan_kernel, tm=tm, n_tok=n_tok, max_tiles=max_tiles),
        in_specs=[vm, vm],
        out_specs=[vm, vm],
        out_shape=[jax.ShapeDtypeStruct((SUBLANES, n_tok), jnp.int32),
                   jax.ShapeDtypeStruct((SUBLANES, max_tiles), jnp.int32)],
        scratch_shapes=[pltpu.VMEM((SUBLANES, n_tok), F32), pltpu.VMEM((N_EXPERTS, n_tok), F32)],
        compiler_params=pltpu.CompilerParams(vmem_limit_bytes=VMEM_LIMIT),
        name="route_plan",
    )(slab_p, slab_s)


def _dispatch_kernel(pos_ref, h_ref, xs_in_ref, xs_ref, sem, *, tb, tok0):
    del xs_in_ref
    i = pl.program_id(0)

    def copies(r):
        base = 2 * (tok0 + i * tb + r)
        src = h_ref.at[pl.ds(r, 1), :]
        return (pltpu.make_async_copy(src, xs_ref.at[pl.ds(pos_ref[base], 1), :], sem),
                pltpu.make_async_copy(src, xs_ref.at[pl.ds(pos_ref[base + 1], 1), :], sem))

    def issue(r, c):
        c0, c1 = copies(r)
        c0.start()
        c1.start()
        return c

    def drain(r, c):
        c0, c1 = copies(r)
        c0.wait()
        c1.wait()
        return c

    lax.fori_loop(0, tb, issue, 0, unroll=8)
    lax.fori_loop(0, tb, drain, 0, unroll=8)


def _dispatch(pos_flat, hpk, xs, tb, tok0):
    n = hpk.shape[0]
    grid_spec = pltpu.PrefetchScalarGridSpec(
        num_scalar_prefetch=1,
        grid=(n // tb,),
        in_specs=[pl.BlockSpec((tb, ROW_W), lambda i, p: (i, 0)),
                  pl.BlockSpec(memory_space=pl.ANY)],
        out_specs=pl.BlockSpec(memory_space=pl.ANY),
        scratch_shapes=[pltpu.SemaphoreType.DMA(())])
    return pl.pallas_call(
        functools.partial(_dispatch_kernel, tb=tb, tok0=tok0),
        grid_spec=grid_spec,
        out_shape=jax.ShapeDtypeStruct(xs.shape, xs.dtype),
        input_output_aliases={2: 0},
        compiler_params=pltpu.CompilerParams(dimension_semantics=("arbitrary",), vmem_limit_bytes=VMEM_LIMIT),
        name="dispatch",
    )(pos_flat, hpk, xs)


def _expert_kernel(te_ref, nt_ref, xs_ref, wg_ref, wu_ref, wd_ref, y_ref, wg_b, wu_b, wd_b):
    i = pl.program_id(0)

    @pl.when(i < nt_ref[0])
    def _():
        changed = jnp.logical_or(i == 0, te_ref[i] != te_ref[jnp.maximum(i - 1, 0)])

        @pl.when(changed)
        def _():
            wg_b[...] = wg_ref[0].astype(BF)
            wu_b[...] = wu_ref[0].astype(BF)
            wd_b[...] = wd_ref[0].astype(BF)

        xb = xs_ref[...].astype(BF)
        gate = _dot(xb, wg_b[...])
        up = _dot(xb, wu_b[...])
        y_ref[...] = _dot((_silu(gate) * up).astype(BF), wd_b[...])

    @pl.when(i >= nt_ref[0])
    def _():
        y_ref[...] = jnp.zeros_like(y_ref)


def _experts(xs, tile_expert, n_tiles, wg, wu, wd, tm, max_tiles):
    grid_spec = pltpu.PrefetchScalarGridSpec(
        num_scalar_prefetch=2,
        grid=(max_tiles,),
        in_specs=[pl.BlockSpec((tm, ROW_W), lambda i, te, nt: (jnp.minimum(i, nt[0] - 1), 0)),
                  pl.BlockSpec((1, D_MODEL, EXPERT_FF), lambda i, te, nt: (te[i], 0, 0)),
                  pl.BlockSpec((1, D_MODEL, EXPERT_FF), lambda i, te, nt: (te[i], 0, 0)),
                  pl.BlockSpec((1, EXPERT_FF, D_MODEL), lambda i, te, nt: (te[i], 0, 0))],
        out_specs=pl.BlockSpec((tm, D_MODEL), lambda i, te, nt: (i, 0)),
        scratch_shapes=[pltpu.VMEM((D_MODEL, EXPERT_FF), BF), pltpu.VMEM((D_MODEL, EXPERT_FF), BF),
                        pltpu.VMEM((EXPERT_FF, D_MODEL), BF)])
    return pl.pallas_call(
        _expert_kernel,
        grid_spec=grid_spec,
        out_shape=jax.ShapeDtypeStruct((max_tiles * tm, D_MODEL), F32),
        compiler_params=pltpu.CompilerParams(dimension_semantics=("arbitrary",), vmem_limit_bytes=VMEM_LIMIT),
        name="experts",
    )(tile_expert, n_tiles, xs, wg, wu, wd)


def _combine_kernel(pos_ref, x1_ref, route_ref, gfin_ref, ys_ref, y_ref, buf0, buf1, sems, *, tb, tok0):
    i = pl.program_id(0)
    n_steps = pl.num_programs(0)
    slot = i % 2

    def copies(step, half, r):
        base = 2 * (tok0 + step * tb + r)
        return (pltpu.make_async_copy(ys_ref.at[pl.ds(pos_ref[base], 1), :],
                                      buf0.at[half, pl.ds(r, 1), :], sems.at[half]),
                pltpu.make_async_copy(ys_ref.at[pl.ds(pos_ref[base + 1], 1), :],
                                      buf1.at[half, pl.ds(r, 1), :], sems.at[half]))

    def request(step, half):
        def body(r, c):
            c0, c1 = copies(step, half, r)
            c0.start()
            c1.start()
            return c

        lax.fori_loop(0, tb, body, 0, unroll=8)

    @pl.when(i == 0)
    def _():
        request(0, 0)

    @pl.when(i + 1 < n_steps)
    def _():
        request(i + 1, 1 - slot)

    def receive(r, c):
        c0, c1 = copies(i, slot, r)
        c0.wait()
        c1.wait()
        return c

    lax.fori_loop(0, tb, receive, 0, unroll=8)
    route = route_ref[...]
    out = x1_ref[...] + (route[:, 2:3] * buf0[slot] + route[:, 3:4] * buf1[slot])
    y_ref[...] = _rms(out, gfin_ref[...])


def _combine(pos_flat, x1, route, gfin, ys, tb, tok0):
    n = x1.shape[0]
    grid_spec = pltpu.PrefetchScalarGridSpec(
        num_scalar_prefetch=1,
        grid=(n // tb,),
        in_specs=[pl.BlockSpec((tb, D_MODEL), lambda i, p: (i, 0)),
                  pl.BlockSpec((tb, ROUTE_LANES), lambda i, p: (i, 0)),
                  pl.BlockSpec((1, D_MODEL), lambda i, p: (0, 0)),
                  pl.BlockSpec(memory_space=pl.ANY)],
        out_specs=pl.BlockSpec((tb, D_MODEL), lambda i, p: (i, 0)),
        scratch_shapes=[pltpu.VMEM((2, tb, D_MODEL), F32), pltpu.VMEM((2, tb, D_MODEL), F32),
                        pltpu.SemaphoreType.DMA((2,))])
    return pl.pallas_call(
        functools.partial(_combine_kernel, tb=tb, tok0=tok0),
        grid_spec=grid_spec,
        out_shape=jax.ShapeDtypeStruct((n, D_MODEL), F32),
        compiler_params=pltpu.CompilerParams(dimension_semantics=("arbitrary",), vmem_limit_bytes=VMEM_LIMIT),
        name="combine",
    )(pos_flat, x1, route, gfin, ys)


def kernel(x_prompt, x_sample, state_conv, state_hgrn, norm_mix_g, w_in, b_gates, conv_dw_w, conv_dw_b,
           conv_ln_g, conv_ln_b, w_conv_out, hgrn_lb_param, hgrn_norm_g, w_hgrn_out, w_out, norm_ffn_g,
           w_router_group, w_router_expert, w_expert_gate, w_expert_up, w_expert_down, norm_final_g):
    batch, seq, _ = x_prompt.shape
    dec_batch = x_sample.shape[0]
    assert x_sample.shape[1] == 1 and w_in.shape[0] == 1
    tb = min(256, seq)
    tm = 256
    bs = min(8, dec_batch)
    n_p = batch * seq
    n_all = n_p + dec_batch
    tbd = min(1024, n_p)
    tbc = min(512, n_p)
    assert seq % tb == 0 and tb % CHUNK == 0 and dec_batch % bs == 0 and n_p % tbd == 0 and n_p % tbc == 0

    w_router = jnp.concatenate(
        [w_router_group[0], w_router_expert[0],
         jnp.zeros((D_MODEL, ROUTE_LANES - N_GROUPS - N_EXPERTS), F32)], axis=1).astype(BF)
    wts = dict(gmix=norm_mix_g, w_in=w_in[0].astype(BF), b_gates=b_gates, conv_w=conv_dw_w[0], conv_b=conv_dw_b,
               ln_g=conv_ln_g, ln_b=conv_ln_b, w_conv_out=w_conv_out[0].astype(BF), lb_param=hgrn_lb_param,
               hn_g=hgrn_norm_g, w_hgrn_out=w_hgrn_out[0].astype(BF), w_out=w_out[0].astype(BF),
               gffn=norm_ffn_g, w_router=w_router)

    x1_p, hpk_p, route_p, nconv_p, nhgrn_p = _prompt_mixer(x_prompt.reshape(n_p, D_MODEL), batch, seq, tb, wts)

    xs_tok = x_sample.reshape(dec_batch, D_MODEL)
    proj_s = _sample_proj(xs_tok, wts)
    nhgrn_s, o_s = _sample_state(proj_s, state_hgrn[0], wts, bs)
    conv2 = state_conv[0].reshape(dec_batch, (CONV_WIDTH - 1) * CONV_DIM)
    x1_s, hpk_s, route_s, nconv_s = _sample_tail(xs_tok, proj_s, o_s, conv2, wts)

    max_tiles = -(-((2 * n_all) // tm + N_EXPERTS) // SUBLANES) * SUBLANES
    pos, tinfo = _plan(route_p, route_s, tm, max_tiles)
    pos_flat = pos[0:2].T.reshape(-1)
    tile_expert, n_tiles = tinfo[0], tinfo[1, 0:1]

    xs = jnp.zeros((max_tiles * tm, ROW_W), F32)
    xs = _dispatch(pos_flat, hpk_p, xs, tbd, 0)
    xs = _dispatch(pos_flat, hpk_s, xs, dec_batch, n_p)
    ys = _experts(xs, tile_expert, n_tiles, w_expert_gate[0], w_expert_up[0], w_expert_down[0], tm, max_tiles)

    gfin = norm_final_g.reshape(1, D_MODEL)
    y_p = _combine(pos_flat, x1_p, route_p, gfin, ys, tbc, 0)
    y_s = _combine(pos_flat, x1_s, route_s, gfin, ys, dec_batch, n_p)

    return (y_p.reshape(batch, seq, D_MODEL), y_s.reshape(dec_batch, 1, D_MODEL),
            nconv_p[None, :, HIST_PAD:, :], nhgrn_p[None],
            nconv_s.reshape(1, dec_batch, CONV_WIDTH - 1, CONV_DIM), nhgrn_s[None])
```

```python
import functools

import jax
import jax.numpy as jnp
from jax import lax
from jax.experimental import pallas as pl
from jax.experimental.pallas import tpu as pltpu

D_MODEL = 1024
CONV_DIM = D_MODEL // 2
CONV_WIDTH = 31
HG_HEADS = 8
HG_DK = 128
HG_DV = 128
HG_DIM = HG_HEADS * HG_DK
N_GROUPS = 4
EXPERTS_PER_GROUP = 8
N_EXPERTS = N_GROUPS * EXPERTS_PER_GROUP
EXPERT_FF = D_MODEL // 2
EPS = 1e-6
IN_COLS = 2 * CONV_DIM + 4 * HG_DIM + 2 * D_MODEL

OFF_CV, OFF_CG = 0, CONV_DIM
OFF_Q = 2 * CONV_DIM
OFF_F = OFF_Q + HG_DIM
OFF_V = OFF_F + HG_DIM
OFF_OG = OFF_V + HG_DIM
OFF_GA = OFF_OG + HG_DIM
OFF_GB = OFF_GA + D_MODEL

SUBLANES = 8
LANES = 128
HIST_ROWS = 32
HIST_PAD = HIST_ROWS - (CONV_WIDTH - 1)
CHUNK = 64
NBLK = CHUNK // SUBLANES
CONV_ROWS = 32
PROJ_COLS = 256
ROUTE_ROWS = 128
ROW_W = D_MODEL
VMEM_LIMIT = 56 * 1024 * 1024

BF = jnp.bfloat16
F32 = jnp.float32


def _dot(a, b):
    return jnp.dot(a, b, preferred_element_type=F32)


def _dot_nt(a, b):
    return lax.dot_general(a, b, (((1,), (1,)), ((), ())), preferred_element_type=F32)


def _dot_tn(a, b, precision=None):
    return lax.dot_general(a, b, (((0,), (0,)), ((), ())), preferred_element_type=F32,
                           precision=precision)


NEG_LOG2E = -1.4426950408889634


def _sigmoid(x):
    return 1.0 / (1.0 + jnp.exp2(x * NEG_LOG2E))


def _silu(x):
    return x * _sigmoid(x)


def _rms(xf, g):
    return xf * lax.rsqrt(jnp.mean(xf * xf, axis=-1, keepdims=True) + EPS) * g


def _lower_bound(lb_param):
    m = jnp.max(lb_param, axis=0, keepdims=True)
    e = jnp.exp(lb_param - m)
    return e[0:1] / jnp.sum(e, axis=0, keepdims=True)


def _conv_post(c, conv_b, ln_g, ln_b):
    c = c + conv_b
    mu = jnp.mean(c, axis=-1, keepdims=True)
    d = c - mu
    var = jnp.mean(d * d, axis=-1, keepdims=True)
    return _silu(d * lax.rsqrt(var + EPS) * ln_g + ln_b)


def _head_norm(o, g):
    return o * lax.rsqrt(jnp.mean(o * o, axis=-1, keepdims=True) + EPS) * g


def _route(logits_t):
    used = -(-(N_GROUPS + N_EXPERTS) // SUBLANES) * SUBLANES
    logits = logits_t[0:used, :]
    row = lax.broadcasted_iota(jnp.int32, logits.shape, 0).astype(F32)
    big = jnp.float32(1 << 20)
    neg = jnp.float32(-jnp.inf)
    gmask = row < N_GROUPS
    lg = jnp.where(gmask, logits, neg)
    gmax = jnp.max(lg, axis=0, keepdims=True)
    gsum = jnp.sum(jnp.where(gmask, jnp.exp(lg - gmax), 0.0), axis=0, keepdims=True)
    gval = 1.0 / gsum
    gidx = jnp.min(jnp.where(lg == gmax, row, big), axis=0, keepdims=True)
    lo = N_GROUPS + EXPERTS_PER_GROUP * gidx
    emask = (row >= lo) & (row < lo + EXPERTS_PER_GROUP)
    el = jnp.where(emask, logits, neg)
    m1 = jnp.max(el, axis=0, keepdims=True)
    i1 = jnp.min(jnp.where(el == m1, row, big), axis=0, keepdims=True)
    el2 = jnp.where(row == i1, neg, el)
    m2 = jnp.max(el2, axis=0, keepdims=True)
    i2 = jnp.min(jnp.where(el2 == m2, row, big), axis=0, keepdims=True)
    r = jnp.exp(m2 - m1)
    w1 = gval / (1.0 + r)
    w2 = gval * r / (1.0 + r)
    out_row = lax.broadcasted_iota(jnp.int32, (SUBLANES, logits.shape[1]), 0)
    return jnp.where(out_row == 0, i1 - N_GROUPS,
                     jnp.where(out_row == 1, i2 - N_GROUPS, jnp.where(out_row == 2, w1, jnp.where(out_row == 3, w2, 0.0))))


LEVELS = (NBLK // 2, NBLK // 4, NBLK // 8)
ROW_TOT, ROW_QE, ROW_KD, ROW_LEVEL = 0, NBLK, 2 * NBLK, 3 * NBLK
TABLE_ROWS = ROW_LEVEL + 2 * NBLK * len(LEVELS)


def _row_bcast(ref, row):
    return jnp.broadcast_to(ref[pl.ds(row, 1), :], (SUBLANES, LANES))


def _per_block(ref, base):
    return jnp.concatenate([_row_bcast(ref, base + j) for j in range(NBLK)], axis=0)


def _block_rows(ref, c):
    return jnp.concatenate([_row_bcast(ref, j * SUBLANES + c) for j in range(NBLK)], axis=0)


def _sparse_tile(x, ref, base, blocks):
    groups = []
    for g in range(0, NBLK, 2):
        if g not in blocks and g + 1 not in blocks:
            groups.append(jnp.zeros((2 * SUBLANES, LANES), BF))
            continue
        halves = [x[j * SUBLANES:(j + 1) * SUBLANES, :] * _row_bcast(ref, base + j) if j in blocks
                  else jnp.zeros((SUBLANES, LANES), F32) for j in (g, g + 1)]
        groups.append(jnp.concatenate(halves, axis=0).astype(BF))
    return jnp.concatenate(groups, axis=0)


def _chunk_cumsum(l2, bl_ref):
    rin = lax.broadcasted_iota(jnp.int32, (CHUNK, LANES), 0) % SUBLANES
    bl = l2
    for s in (1, 2, 4):
        bl = bl + jnp.where(rin >= s, pltpu.roll(bl, s, axis=0), 0.0)
    bl_ref[...] = bl
    return bl


def _chunk_table(bl_ref, tab_ref):
    tot = bl_ref[pl.ds(SUBLANES - 1, NBLK, stride=SUBLANES), :]
    brow = lax.broadcasted_iota(jnp.int32, (NBLK, LANES), 0)
    rb = tot
    for s in (1, 2, 4):
        rb = rb + jnp.where(brow >= s, pltpu.roll(rb, s, axis=0), 0.0)
    rb_prev = rb - tot
    total = rb[NBLK - 1:NBLK, :]

    tab_ref[pl.ds(ROW_TOT, NBLK), :] = tot
    tab_ref[pl.ds(ROW_QE, NBLK), :] = jnp.exp2(rb_prev)
    tab_ref[pl.ds(ROW_KD, NBLK), :] = jnp.exp2(total - rb)
    for lv, cb in enumerate(LEVELS):
        edges = [(j // (2 * cb)) * (2 * cb) + cb - 1 for j in range(NBLK)]
        rb_edge = jnp.concatenate([rb[e:e + 1, :] for e in edges], axis=0)
        base = ROW_LEVEL + 2 * NBLK * lv
        tab_ref[pl.ds(base, NBLK), :] = jnp.exp2(jnp.minimum(rb_prev - rb_edge, 0.0))
        tab_ref[pl.ds(base + NBLK, NBLK), :] = jnp.exp2(jnp.minimum(rb_edge - rb, 0.0))
    return total


def _chunk_scaled(q, k, bl, tab_ref):
    return q * jnp.exp2(bl), k * jnp.exp2(_per_block(tab_ref, ROW_TOT) - bl)


def _chunk_inter_operands(qp, kp, tab_ref):
    return (qp * _per_block(tab_ref, ROW_QE)).astype(BF), (kp * _per_block(tab_ref, ROW_KD)).astype(BF)


def _chunk_pair_operands(qp, kp, tab_ref):
    q_tiles, k_tiles = [], []
    for lv, cb in enumerate(LEVELS):
        base = ROW_LEVEL + 2 * NBLK * lv
        for p0 in range(0, NBLK, 2 * cb):
            q_tiles.append(_sparse_tile(qp, tab_ref, base, range(p0 + cb, p0 + 2 * cb)))
            k_tiles.append(_sparse_tile(kp, tab_ref, base + NBLK, range(p0, p0 + cb)))
    return jnp.concatenate(q_tiles, axis=1), jnp.concatenate(k_tiles, axis=1)


def _chunk_block_operands(q, k, bl, row_masks, bl_ref):
    k_b = k.astype(BF)
    lhs = [(q * jnp.exp2(jnp.minimum(bl - _block_rows(bl_ref, c), 0.0))).astype(BF) for c in range(SUBLANES)]
    rhs = [k_b * row_masks[c] for c in range(SUBLANES)]
    return jnp.concatenate(lhs, axis=1), jnp.concatenate(rhs, axis=1)


def _chunk_products(qe, kdec, v_b, pair_ops, block_ops, st):
    return (_dot_nt(qe, st.astype(BF)), _dot_tn(v_b, kdec), _dot_nt(*pair_ops), _dot_nt(*block_ops))


def _chunk_output(o_inter, off_diag, diag, v_b):
    ti = lax.broadcasted_iota(jnp.int32, (CHUNK, CHUNK), 0)
    si = lax.broadcasted_iota(jnp.int32, (CHUNK, CHUNK), 1)
    scores = jnp.where((ti // SUBLANES == si // SUBLANES) & (si <= ti), diag, off_diag)
    return o_inter + _dot(scores.astype(BF), v_b)


def _mixer_tail(x, m, w_out, g_ffn, w_router_t):
    x1 = x + _dot(m.astype(BF), w_out)
    h2 = _rms(x1, g_ffn).astype(BF)
    route = _route(_dot_nt(w_router_t, h2))
    return x1, h2.astype(F32), route


def _prompt_mixer_kernel(x_ref, gmix_ref, win_ref, bg_ref, cw_ref, cb_ref, lng_ref, lnb_ref, wco_ref,
                         lbp_ref, hng_ref, who_ref, wout_ref, gffn_ref, wr_ref, masks_ref,
                         x1_ref, h2_ref, route_ref, nconv_ref, nhgrn_ref,
                         hist_ref, phase_ref, conv_ref, st_ref, q_s, k_s, v_s, lf_s, o_s, og_s, ga_s, gb_s, *head_scr, tb):
    t = pl.program_id(1)

    @pl.when(t == 0)
    def _():
        hist_ref[pl.ds(0, HIST_ROWS), :] = jnp.zeros((HIST_ROWS, CONV_DIM), F32)
        st_ref[...] = jnp.zeros_like(st_ref)

    x = x_ref[...]
    h = _rms(x, gmix_ref[...]).astype(BF)

    cv = _dot(h, win_ref[:, OFF_CV:OFF_CV + CONV_DIM])
    cg = _dot(h, win_ref[:, OFF_CG:OFF_CG + CONV_DIM])
    hist_ref[pl.ds(HIST_ROWS, tb), :] = cv * _sigmoid(cg)

    lb = _lower_bound(lbp_ref[...])

    def project(seg, c0):
        z = _dot(h, win_ref[:, seg + c0:seg + c0 + PROJ_COLS])
        cols = slice(c0, c0 + PROJ_COLS)
        if seg == OFF_Q:
            q_s[:, cols] = _silu(z) * (HG_DK ** -0.5)
        elif seg == OFF_F:
            fg = lb[:, cols] + (1.0 - lb[:, cols]) * _sigmoid(z)
            lf_s[:, cols] = jnp.log2(fg)
            k_s[:, cols] = 1.0 - fg
        elif seg == OFF_V:
            v_s[:, cols] = z
        elif seg == OFF_OG:
            og_s[:, cols] = _silu(z)
        elif seg == OFF_GA:
            ga_s[:, cols] = _sigmoid(z + bg_ref[:, c0:c0 + PROJ_COLS])
        else:
            gb_s[:, cols] = _sigmoid(z + bg_ref[:, D_MODEL + c0:D_MODEL + c0 + PROJ_COLS])

    proj_jobs = [functools.partial(project, seg, c0)
                 for seg in (OFF_Q, OFF_F, OFF_V, OFF_OG, OFF_GA, OFF_GB) for c0 in range(0, HG_DIM, PROJ_COLS)]

    span = tb + SUBLANES * (-(-CONV_WIDTH // SUBLANES) - 1)

    def phase_copy(r):
        n = min(span, HIST_ROWS + tb - HIST_PAD - r)
        phase_ref[r, pl.ds(0, n), :] = hist_ref[pl.ds(HIST_PAD + r, n), :]

    def conv_rows(r0):
        acc = jnp.zeros((CONV_ROWS, CONV_DIM), F32)
        for j in range(CONV_WIDTH):
            w_j = jnp.broadcast_to(cw_ref[pl.ds(j, 1), :], (SUBLANES, CONV_DIM))
            acc = acc + (phase_ref[j % SUBLANES, pl.ds(r0 + j - j % SUBLANES, CONV_ROWS), :]
                         * jnp.concatenate([w_j] * (CONV_ROWS // SUBLANES), axis=0))
        conv_ref[pl.ds(r0, CONV_ROWS), :] = acc

    conv_jobs = ([functools.partial(phase_copy, r) for r in range(SUBLANES)]
                 + [functools.partial(conv_rows, r0) for r0 in range(0, tb, CONV_ROWS)])
    per_conv_job = -(-len(proj_jobs) // len(conv_jobs))
    for i, job in enumerate(conv_jobs):
        job()
        for pj in proj_jobs[i * per_conv_job:(i + 1) * per_conv_job]:
            pj()
    for pj in proj_jobs[len(conv_jobs) * per_conv_job:]:
        pj()

    tail = hist_ref[pl.ds(tb, HIST_ROWS), :]
    hist_ref[pl.ds(0, HIST_ROWS), :] = tail
    nconv_ref[0] = tail
    c = _conv_post(conv_ref[...], cb_ref[...], lng_ref[...], lnb_ref[...])
    y_a = _dot(c.astype(BF), wco_ref[...])

    row_masks = masks_ref[...]
    cols = [slice(hd * HG_DK, (hd + 1) * HG_DK) for hd in range(HG_HEADS)]
    bl_refs, tab_refs = head_scr[:HG_HEADS], head_scr[HG_HEADS:]
    heads = range(HG_HEADS)
    for r0 in range(0, tb, CHUNK):
        rows = pl.ds(r0, CHUNK)
        q = [q_s[rows, cols[hd]] for hd in heads]
        k = [k_s[rows, cols[hd]] for hd in heads]
        bl = [_chunk_cumsum(lf_s[rows, cols[hd]], bl_refs[hd]) for hd in heads]
        total = [_chunk_table(bl_refs[hd], tab_refs[hd]) for hd in heads]
        scaled = [_chunk_scaled(q[hd], k[hd], bl[hd], tab_refs[hd]) for hd in heads]
        inter = [_chunk_inter_operands(*scaled[hd], tab_refs[hd]) for hd in heads]
        pair_ops = [_chunk_pair_operands(*scaled[hd], tab_refs[hd]) for hd in heads]
        block_ops = [_chunk_block_operands(q[hd], k[hd], bl[hd], row_masks, bl_refs[hd]) for hd in heads]
        v_b = [v_s[rows, cols[hd]].astype(BF) for hd in heads]
        prods = [_chunk_products(*inter[hd], v_b[hd], pair_ops[hd], block_ops[hd], st_ref[hd]) for hd in heads]
        for hd in heads:
            o_inter, update, off_diag, diag = prods[hd]
            st_ref[hd] = st_ref[hd] * jnp.exp2(total[hd]) + update
            o = _chunk_output(o_inter, off_diag, diag, v_b[hd])
            o_s[rows, cols[hd]] = _head_norm(o, hng_ref[:, cols[hd]])
    for hd in range(HG_HEADS):
        nhgrn_ref[0, hd] = st_ref[hd].T

    y_b = _dot((o_s[...] * og_s[...]).astype(BF), who_ref[...])
    x1, h2, route = _mixer_tail(x, ga_s[...] * y_a + gb_s[...] * y_b, wout_ref[...], gffn_ref[...], wr_ref[...])
    x1_ref[...] = x1
    h2_ref[...] = h2
    route_ref[...] = route


def _const_spec(shape):
    nd = len(shape)
    return pl.BlockSpec(shape, lambda *_: (0,) * nd, pipeline_mode=pl.Buffered(1))


def _row_masks():
    c = jnp.arange(SUBLANES)[:, None, None]
    r = jnp.arange(CHUNK)[None, :, None] % SUBLANES
    return jnp.broadcast_to(r == c, (SUBLANES, CHUNK, LANES)).astype(BF)


def _prompt_mixer(x2, batch, seq, tb, wts):
    nt = seq // tb
    n = batch * seq
    row_spec = lambda w: pl.BlockSpec((tb, w), lambda b, t: (b * nt + t, 0))
    consts = [wts['gmix'], wts['w_in'], wts['b_gates'], wts['conv_w'], wts['conv_b'], wts['ln_g'], wts['ln_b'],
              wts['w_conv_out'], wts['lb_param'], wts['hn_g'], wts['w_hgrn_out'], wts['w_out'], wts['gffn'],
              wts['w_router'], _row_masks()]
    span = tb + SUBLANES * (-(-CONV_WIDTH // SUBLANES) - 1)
    return pl.pallas_call(
        functools.partial(_prompt_mixer_kernel, tb=tb),
        grid=(batch, nt),
        in_specs=[row_spec(D_MODEL)] + [_const_spec(c.shape) for c in consts],
        out_specs=[row_spec(D_MODEL), row_spec(ROW_W),
                   pl.BlockSpec((SUBLANES, tb), lambda b, t: (0, b * nt + t)),
                   pl.BlockSpec((1, HIST_ROWS, CONV_DIM), lambda b, t: (b, 0, 0)),
                   pl.BlockSpec((1, HG_HEADS, HG_DK, HG_DV), lambda b, t: (b, 0, 0, 0))],
        out_shape=[jax.ShapeDtypeStruct((n, D_MODEL), F32), jax.ShapeDtypeStruct((n, ROW_W), F32),
                   jax.ShapeDtypeStruct((SUBLANES, n), F32),
                   jax.ShapeDtypeStruct((batch, HIST_ROWS, CONV_DIM), F32),
                   jax.ShapeDtypeStruct((batch, HG_HEADS, HG_DK, HG_DV), F32)],
        scratch_shapes=[pltpu.VMEM((HIST_ROWS + tb, CONV_DIM), F32),
                        pltpu.VMEM((SUBLANES, span, CONV_DIM), F32),
                        pltpu.VMEM((tb, CONV_DIM), F32),
                        pltpu.VMEM((HG_HEADS, HG_DV, HG_DK), F32)]
                       + [pltpu.VMEM((tb, HG_DIM), F32)] * 8
                       + [pltpu.VMEM((CHUNK, LANES), F32)] * HG_HEADS
                       + [pltpu.VMEM((TABLE_ROWS, LANES), F32)] * HG_HEADS,
        compiler_params=pltpu.CompilerParams(dimension_semantics=("arbitrary", "arbitrary"),
                                             vmem_limit_bytes=VMEM_LIMIT),
        name="prompt_mixer",
    )(x2, *consts)


def _sample_proj_kernel(x_ref, gmix_ref, win_ref, proj_ref):
    h = _rms(x_ref[...], gmix_ref[...]).astype(BF)
    proj_ref[...] = _dot(h, win_ref[...])


def _sample_proj(xs, wts):
    n = xs.shape[0]
    return pl.pallas_call(
        _sample_proj_kernel,
        grid=(1,),
        in_specs=[_const_spec(xs.shape), _const_spec(wts['gmix'].shape), _const_spec(wts['w_in'].shape)],
        out_specs=pl.BlockSpec((n, IN_COLS), lambda i: (0, 0)),
        out_shape=jax.ShapeDtypeStruct((n, IN_COLS), F32),
        compiler_params=pltpu.CompilerParams(vmem_limit_bytes=VMEM_LIMIT),
        name="sample_proj",
    )(xs, wts['gmix'], wts['w_in'])


def _split3(x):
    hi = x.astype(BF).astype(F32)
    mid = (x - hi).astype(BF).astype(F32)
    lo = ((x - hi) - mid).astype(BF).astype(F32)
    return hi, mid, lo


def _sample_state_kernel(q_ref, f_ref, v_ref, lbp_ref, hng_ref, lane_masks_ref, s_ref, snew_ref, o_ref, *, bs):
    assert bs == SUBLANES
    lb = _lower_bound(lbp_ref[...])
    qf = _silu(q_ref[...]) * (HG_DK ** -0.5)
    fg = lb + (1.0 - lb) * _sigmoid(f_ref[...])
    kf = 1.0 - fg
    v = v_ref[...]
    ones = jnp.ones((bs, HG_DV), F32)
    zeros = jnp.zeros((bs, HG_DV), F32)
    n_groups = 9
    pad = jnp.zeros((LANES - n_groups * bs, LANES), F32)
    for hd in range(HG_HEADS):
        cs = slice(hd * HG_DK, (hd + 1) * HG_DK)
        f3, k3, q3, v3 = _split3(fg[:, cs]), _split3(kf[:, cs]), _split3(qf[:, cs]), _split3(v[:, cs])
        left = jnp.concatenate([f3[0], f3[1], f3[2], k3[0], k3[1], k3[0], q3[0], q3[1], q3[2], pad], axis=0)
        left_t = left.T.astype(BF)
        right = jnp.concatenate(
            [jnp.concatenate(blk, axis=1) for blk in
             [(ones, zeros, zeros)] * 3 + [(zeros, v3[0], zeros), (zeros, v3[0], zeros), (zeros, v3[1], zeros)]
             + [(zeros, zeros, ones)] * 3] + [jnp.concatenate((pad, pad, pad), axis=1)], axis=0).astype(BF)
        o_rows = []
        for r in range(bs):
            prod = _dot(left_t * lane_masks_ref[r], right)
            s_new = prod[:, :HG_DV] * s_ref[r, hd] + prod[:, HG_DV:2 * HG_DV]
            snew_ref[r, hd] = s_new
            o_rows.append(jnp.sum(prod[:, 2 * HG_DV:] * s_new, axis=0, keepdims=True))
        o_ref[:, cs] = _head_norm(jnp.concatenate(o_rows, axis=0), hng_ref[:, cs])


def _lane_masks():
    r = jnp.arange(SUBLANES)[:, None, None]
    lane = jnp.arange(LANES)[None, None, :] % SUBLANES
    return jnp.broadcast_to(lane == r, (SUBLANES, LANES, LANES)).astype(BF)


def _sample_state(proj, state, wts, bs):
    n = proj.shape[0]
    col_spec = lambda off: pl.BlockSpec((bs, HG_DIM), lambda i, off=off: (i, off // HG_DIM))
    st_spec = pl.BlockSpec((bs, HG_HEADS, HG_DK, HG_DV), lambda i: (i, 0, 0, 0))
    return pl.pallas_call(
        functools.partial(_sample_state_kernel, bs=bs),
        grid=(n // bs,),
        in_specs=[col_spec(OFF_Q), col_spec(OFF_F), col_spec(OFF_V),
                  _const_spec(wts['lb_param'].shape), _const_spec(wts['hn_g'].shape),
                  _const_spec((SUBLANES, LANES, LANES)), st_spec],
        out_specs=[st_spec, pl.BlockSpec((bs, HG_DIM), lambda i: (i, 0))],
        out_shape=[jax.ShapeDtypeStruct(state.shape, F32), jax.ShapeDtypeStruct((n, HG_DIM), F32)],
        compiler_params=pltpu.CompilerParams(dimension_semantics=("arbitrary",), vmem_limit_bytes=VMEM_LIMIT),
        name="sample_state",
    )(proj, proj, proj, wts['lb_param'], wts['hn_g'], _lane_masks(), state)


def _sample_tail_kernel(x_ref, proj_ref, o_ref, cst_ref, bg_ref, cw_ref, cb_ref, lng_ref, lnb_ref, wco_ref,
                        who_ref, wout_ref, gffn_ref, wr_ref, x1_ref, h2_ref, route_ref, nconv_ref):
    keep = (CONV_WIDTH - 2) * CONV_DIM
    u = proj_ref[:, OFF_CV:OFF_CV + CONV_DIM] * _sigmoid(proj_ref[:, OFF_CG:OFF_CG + CONV_DIM])
    acc = u * cw_ref[pl.ds(CONV_WIDTH - 1, 1), :]
    for j in range(CONV_WIDTH - 1):
        acc = acc + cst_ref[:, j * CONV_DIM:(j + 1) * CONV_DIM] * cw_ref[pl.ds(j, 1), :]
    nconv_ref[:, :keep] = cst_ref[:, CONV_DIM:]
    nconv_ref[:, keep:] = u
    c = _conv_post(acc, cb_ref[...], lng_ref[...], lnb_ref[...])
    y_a = _dot(c.astype(BF), wco_ref[...])
    og = proj_ref[:, OFF_OG:OFF_OG + HG_DIM]
    y_b = _dot((o_ref[...] * _silu(og)).astype(BF), who_ref[...])
    m = (_sigmoid(proj_ref[:, OFF_GA:OFF_GA + D_MODEL] + bg_ref[:, :D_MODEL]) * y_a
         + _sigmoid(proj_ref[:, OFF_GB:OFF_GB + D_MODEL] + bg_ref[:, D_MODEL:]) * y_b)
    x1, h2, route = _mixer_tail(x_ref[...], m, wout_ref[...], gffn_ref[...], wr_ref[...])
    x1_ref[...] = x1
    h2_ref[...] = h2
    route_ref[...] = route


def _sample_tail(xs, proj, o, conv_state2, wts):
    n = xs.shape[0]
    ins = [xs, proj, o, conv_state2, wts['b_gates'], wts['conv_w'], wts['conv_b'], wts['ln_g'], wts['ln_b'],
           wts['w_conv_out'], wts['w_hgrn_out'], wts['w_out'], wts['gffn'], wts['w_router']]
    full = lambda shape: pl.BlockSpec(shape, lambda i: (0,) * len(shape))
    return pl.pallas_call(
        _sample_tail_kernel,
        grid=(1,),
        in_specs=[_const_spec(a.shape) for a in ins],
        out_specs=[full((n, D_MODEL)), full((n, ROW_W)), full((SUBLANES, n)), full(conv_state2.shape)],
        out_shape=[jax.ShapeDtypeStruct((n, D_MODEL), F32), jax.ShapeDtypeStruct((n, ROW_W), F32),
                   jax.ShapeDtypeStruct((SUBLANES, n), F32), jax.ShapeDtypeStruct(conv_state2.shape, F32)],
        compiler_params=pltpu.CompilerParams(vmem_limit_bytes=VMEM_LIMIT),
        name="sample_tail",
    )(*ins)


def _plan_kernel(route_p_ref, route_s_ref, pos_ref, tinfo_ref, e_ref, pre_ref, *, tm, n_tok, max_tiles):
    n_p = route_p_ref.shape[1]
    nblk = n_tok // LANES
    e_ref[:, pl.ds(0, n_p)] = route_p_ref[...]
    e_ref[:, pl.ds(n_p, n_tok - n_p)] = route_s_ref[...]
    eidx = lax.broadcasted_iota(jnp.int32, (N_EXPERTS, LANES), 0).astype(F32)
    ti = lax.broadcasted_iota(jnp.int32, (LANES, LANES), 0)
    si = lax.broadcasted_iota(jnp.int32, (LANES, LANES), 1)
    before = (ti < si).astype(BF)

    def onehots(b):
        c0 = pl.multiple_of(b * LANES, LANES)
        e1 = e_ref[0:1, pl.ds(c0, LANES)]
        e2 = e_ref[1:2, pl.ds(c0, LANES)]
        return (e1 == eidx).astype(F32), (e2 == eidx).astype(F32), c0

    def count(b, carry):
        h1, h2, c0 = onehots(b)
        h = h1 + h2
        pre_ref[:, pl.ds(c0, LANES)] = _dot(h.astype(BF), before) + carry
        return carry + jnp.sum(h, axis=1, keepdims=True)

    counts = lax.fori_loop(0, nblk, count, jnp.zeros((N_EXPERTS, 1), F32))
    tiles_per = jnp.floor((counts + (tm - 1)) * (1.0 / tm))
    ei = lax.broadcasted_iota(jnp.int32, (N_EXPERTS, N_EXPERTS), 0)
    ej = lax.broadcasted_iota(jnp.int32, (N_EXPERTS, N_EXPERTS), 1)
    upto = (ej <= ei).astype(BF)
    tile_end = _dot(upto, jnp.broadcast_to(tiles_per, (N_EXPERTS, LANES)).astype(BF))[:, 0:1]
    starts = (tile_end - tiles_per) * tm

    def place(b, c):
        h1, h2, c0 = onehots(b)
        dest = pre_ref[:, pl.ds(c0, LANES)] + starts
        pos_ref[0:1, pl.ds(c0, LANES)] = jnp.sum(h1 * dest, axis=0, keepdims=True).astype(jnp.int32)
        pos_ref[1:2, pl.ds(c0, LANES)] = jnp.sum(h2 * dest, axis=0, keepdims=True).astype(jnp.int32)
        return c

    pos_ref[...] = jnp.zeros_like(pos_ref)
    lax.fori_loop(0, nblk, place, 0)

    tile = lax.broadcasted_iota(jnp.int32, (N_EXPERTS, max_tiles), 1).astype(F32)
    t_exp = jnp.sum((tile_end <= tile).astype(F32), axis=0, keepdims=True)
    t_exp = jnp.minimum(t_exp, N_EXPERTS - 1.0).astype(jnp.int32)
    n_used = jnp.broadcast_to(tile_end[N_EXPERTS - 1:N_EXPERTS, :], (1, max_tiles)).astype(jnp.int32)
    row = lax.broadcasted_iota(jnp.int32, (SUBLANES, max_tiles), 0)
    tinfo_ref[...] = jnp.where(row == 0, t_exp, jnp.where(row == 1, n_used, 0))


def _plan(route_p, route_s, tm, max_tiles):
    n_tok = route_p.shape[1] + route_s.shape[1]
    assert route_p.shape[1] % LANES == 0 and route_s.shape[1] % LANES == 0
    vm = pl.BlockSpec(memory_space=pltpu.VMEM)
    return pl.pallas_call(
        functools.partial(_plan_kernel, tm=tm, n_tok=n_tok, max_tiles=max_tiles),
        in_specs=[vm, vm],
        out_specs=[vm, vm],
        out_shape=[jax.ShapeDtypeStruct((SUBLANES, n_tok), jnp.int32),
                   jax.ShapeDtypeStruct((SUBLANES, max_tiles), jnp.int32)],
        scratch_shapes=[pltpu.VMEM((SUBLANES, n_tok), F32), pltpu.VMEM((N_EXPERTS, n_tok), F32)],
        compiler_params=pltpu.CompilerParams(vmem_limit_bytes=VMEM_LIMIT),
        name="route_plan",
    )(route_p, route_s)


def _dispatch_kernel(pos_ref, h_ref, xs_in_ref, xs_ref, sem, *, tb, tok0):
    del xs_in_ref
    i = pl.program_id(0)

    def copies(r):
        base = 2 * (tok0 + i * tb + r)
        src = h_ref.at[pl.ds(r, 1), :]
        return (pltpu.make_async_copy(src, xs_ref.at[pl.ds(pos_ref[base], 1), :], sem),
                pltpu.make_async_copy(src, xs_ref.at[pl.ds(pos_ref[base + 1], 1), :], sem))

    def issue(r, c):
        c0, c1 = copies(r)
        c0.start()
        c1.start()
        return c

    def drain(r, c):
        c0, c1 = copies(r)
        c0.wait()
        c1.wait()
        return c

    lax.fori_loop(0, tb, issue, 0, unroll=8)
    lax.fori_loop(0, tb, drain, 0, unroll=8)


def _dispatch(pos_flat, hpk, xs, tb, tok0):
    n = hpk.shape[0]
    grid_spec = pltpu.PrefetchScalarGridSpec(
        num_scalar_prefetch=1,
        grid=(n // tb,),
        in_specs=[pl.BlockSpec((tb, ROW_W), lambda i, p: (i, 0)),
                  pl.BlockSpec(memory_space=pl.ANY)],
        out_specs=pl.BlockSpec(memory_space=pl.ANY),
        scratch_shapes=[pltpu.SemaphoreType.DMA(())])
    return pl.pallas_call(
        functools.partial(_dispatch_kernel, tb=tb, tok0=tok0),
        grid_spec=grid_spec,
        out_shape=jax.ShapeDtypeStruct(xs.shape, xs.dtype),
        input_output_aliases={2: 0},
        compiler_params=pltpu.CompilerParams(dimension_semantics=("arbitrary",), vmem_limit_bytes=VMEM_LIMIT),
        name="dispatch",
    )(pos_flat, hpk, xs)


def _expert_kernel(te_ref, nt_ref, xs_ref, wg_ref, wu_ref, wd_ref, y_ref, wg_b, wu_b, wd_b):
    i = pl.program_id(0)

    @pl.when(i < nt_ref[0])
    def _():
        changed = jnp.logical_or(i == 0, te_ref[i] != te_ref[jnp.maximum(i - 1, 0)])

        @pl.when(changed)
        def _():
            wg_b[...] = wg_ref[0].astype(BF)
            wu_b[...] = wu_ref[0].astype(BF)
            wd_b[...] = wd_ref[0].astype(BF)

        xb = xs_ref[...].astype(BF)
        gate = _dot(xb, wg_b[...])
        up = _dot(xb, wu_b[...])
        y_ref[...] = _dot((_silu(gate) * up).astype(BF), wd_b[...])

    @pl.when(i >= nt_ref[0])
    def _():
        y_ref[...] = jnp.zeros_like(y_ref)


def _experts(xs, tile_expert, n_tiles, wg, wu, wd, tm, max_tiles):
    grid_spec = pltpu.PrefetchScalarGridSpec(
        num_scalar_prefetch=2,
        grid=(max_tiles,),
        in_specs=[pl.BlockSpec((tm, ROW_W), lambda i, te, nt: (jnp.minimum(i, nt[0] - 1), 0)),
                  pl.BlockSpec((1, D_MODEL, EXPERT_FF), lambda i, te, nt: (te[i], 0, 0)),
                  pl.BlockSpec((1, D_MODEL, EXPERT_FF), lambda i, te, nt: (te[i], 0, 0)),
                  pl.BlockSpec((1, EXPERT_FF, D_MODEL), lambda i, te, nt: (te[i], 0, 0))],
        out_specs=pl.BlockSpec((tm, D_MODEL), lambda i, te, nt: (i, 0)),
        scratch_shapes=[pltpu.VMEM((D_MODEL, EXPERT_FF), BF), pltpu.VMEM((D_MODEL, EXPERT_FF), BF),
                        pltpu.VMEM((EXPERT_FF, D_MODEL), BF)])
    return pl.pallas_call(
        _expert_kernel,
        grid_spec=grid_spec,
        out_shape=jax.ShapeDtypeStruct((max_tiles * tm, D_MODEL), F32),
        compiler_params=pltpu.CompilerParams(dimension_semantics=("arbitrary",), vmem_limit_bytes=VMEM_LIMIT),
        name="experts",
    )(tile_expert, n_tiles, xs, wg, wu, wd)


def _combine_kernel(pos_ref, x1_ref, route_ref, gfin_ref, ys_ref, y_ref, buf0, buf1, sems, *, tb, tok0):
    i = pl.program_id(0)
    n_steps = pl.num_programs(0)
    slot = i % 2

    def copies(step, half, r):
        base = 2 * (tok0 + step * tb + r)
        return (pltpu.make_async_copy(ys_ref.at[pl.ds(pos_ref[base], 1), :],
                                      buf0.at[half, pl.ds(r, 1), :], sems.at[half]),
                pltpu.make_async_copy(ys_ref.at[pl.ds(pos_ref[base + 1], 1), :],
                                      buf1.at[half, pl.ds(r, 1), :], sems.at[half]))

    def request(step, half):
        def body(r, c):
            c0, c1 = copies(step, half, r)
            c0.start()
            c1.start()
            return c

        lax.fori_loop(0, tb, body, 0, unroll=8)

    @pl.when(i == 0)
    def _():
        request(0, 0)

    @pl.when(i + 1 < n_steps)
    def _():
        request(i + 1, 1 - slot)

    def receive(r, c):
        c0, c1 = copies(i, slot, r)
        c0.wait()
        c1.wait()
        return c

    lax.fori_loop(0, tb, receive, 0, unroll=8)
    pad = jnp.zeros((LANES - SUBLANES, LANES), F32)
    cols = jnp.concatenate([jnp.concatenate([route_ref[:, j:j + LANES], pad], axis=0).T
                            for j in range(0, tb, LANES)], axis=0)
    out = x1_ref[...] + (cols[:, 2:3] * buf0[slot] + cols[:, 3:4] * buf1[slot])
    y_ref[...] = _rms(out, gfin_ref[...])


def _combine(pos_flat, x1, route, gfin, ys, tb, tok0):
    n = x1.shape[0]
    grid_spec = pltpu.PrefetchScalarGridSpec(
        num_scalar_prefetch=1,
        grid=(n // tb,),
        in_specs=[pl.BlockSpec((tb, D_MODEL), lambda i, p: (i, 0)),
                  pl.BlockSpec((SUBLANES, tb), lambda i, p: (0, i)),
                  pl.BlockSpec((1, D_MODEL), lambda i, p: (0, 0)),
                  pl.BlockSpec(memory_space=pl.ANY)],
        out_specs=pl.BlockSpec((tb, D_MODEL), lambda i, p: (i, 0)),
        scratch_shapes=[pltpu.VMEM((2, tb, D_MODEL), F32), pltpu.VMEM((2, tb, D_MODEL), F32),
                        pltpu.SemaphoreType.DMA((2,))])
    return pl.pallas_call(
        functools.partial(_combine_kernel, tb=tb, tok0=tok0),
        grid_spec=grid_spec,
        out_shape=jax.ShapeDtypeStruct((n, D_MODEL), F32),
        compiler_params=pltpu.CompilerParams(dimension_semantics=("arbitrary",), vmem_limit_bytes=VMEM_LIMIT),
        name="combine",
    )(pos_flat, x1, route, gfin, ys)


def kernel(x_prompt, x_sample, state_conv, state_hgrn, norm_mix_g, w_in, b_gates, conv_dw_w, conv_dw_b,
           conv_ln_g, conv_ln_b, w_conv_out, hgrn_lb_param, hgrn_norm_g, w_hgrn_out, w_out, norm_ffn_g,
           w_router_group, w_router_expert, w_expert_gate, w_expert_up, w_expert_down, norm_final_g):
    batch, seq, _ = x_prompt.shape
    dec_batch = x_sample.shape[0]
    assert x_sample.shape[1] == 1 and w_in.shape[0] == 1
    tb = min(256, seq)
    tm = 256
    bs = min(8, dec_batch)
    n_p = batch * seq
    n_all = n_p + dec_batch
    tbd = min(1024, n_p)
    tbc = min(512, n_p)
    assert seq % tb == 0 and tb % CHUNK == 0 and dec_batch % bs == 0 and n_p % tbd == 0 and n_p % tbc == 0

    w_router = jnp.concatenate(
        [w_router_group[0].T, w_router_expert[0].T,
         jnp.zeros((ROUTE_ROWS - N_GROUPS - N_EXPERTS, D_MODEL), F32)], axis=0).astype(BF)
    wts = dict(gmix=norm_mix_g, w_in=w_in[0].astype(BF), b_gates=b_gates, conv_w=conv_dw_w[0], conv_b=conv_dw_b,
               ln_g=conv_ln_g, ln_b=conv_ln_b, w_conv_out=w_conv_out[0].astype(BF), lb_param=hgrn_lb_param,
               hn_g=hgrn_norm_g, w_hgrn_out=w_hgrn_out[0].astype(BF), w_out=w_out[0].astype(BF),
               gffn=norm_ffn_g, w_router=w_router)

    x1_p, hpk_p, route_p, nconv_p, nhgrn_p = _prompt_mixer(x_prompt.reshape(n_p, D_MODEL), batch, seq, tb, wts)

    xs_tok = x_sample.reshape(dec_batch, D_MODEL)
    proj_s = _sample_proj(xs_tok, wts)
    nhgrn_s, o_s = _sample_state(proj_s, state_hgrn[0], wts, bs)
    conv2 = state_conv[0].reshape(dec_batch, (CONV_WIDTH - 1) * CONV_DIM)
    x1_s, hpk_s, route_s, nconv_s = _sample_tail(xs_tok, proj_s, o_s, conv2, wts)

    max_tiles = -(-((2 * n_all) // tm + N_EXPERTS) // SUBLANES) * SUBLANES
    pos, tinfo = _plan(route_p, route_s, tm, max_tiles)
    pos_flat = pos[0:2].T.reshape(-1)
    tile_expert, n_tiles = tinfo[0], tinfo[1, 0:1]

    xs = jnp.zeros((max_tiles * tm, ROW_W), F32)
    xs = _dispatch(pos_flat, hpk_p, xs, tbd, 0)
    xs = _dispatch(pos_flat, hpk_s, xs, dec_batch, n_p)
    ys = _experts(xs, tile_expert, n_tiles, w_expert_gate[0], w_expert_up[0], w_expert_down[0], tm, max_tiles)

    gfin = norm_final_g.reshape(1, D_MODEL)
    y_p = _combine(pos_flat, x1_p, route_p, gfin, ys, tbc, 0)
    y_s = _combine(pos_flat, x1_s, route_s, gfin, ys, dec_batch, n_p)

    return (y_p.reshape(batch, seq, D_MODEL), y_s.reshape(dec_batch, 1, D_MODEL),
            nconv_p[None, :, HIST_PAD:, :], nhgrn_p[None],
            nconv_s.reshape(1, dec_batch, CONV_WIDTH - 1, CONV_DIM), nhgrn_s[None])
```

```python
import functools

import jax
import jax.numpy as jnp
from jax import lax
from jax.experimental import pallas as pl
from jax.experimental.pallas import tpu as pltpu

D_MODEL = 1024
CONV_DIM = D_MODEL // 2
CONV_WIDTH = 31
HG_HEADS = 8
HG_DK = 128
HG_DV = 128
HG_DIM = HG_HEADS * HG_DK
N_GROUPS = 4
EXPERTS_PER_GROUP = 8
N_EXPERTS = N_GROUPS * EXPERTS_PER_GROUP
EXPERT_FF = D_MODEL // 2
EPS = 1e-6
IN_COLS = 2 * CONV_DIM + 4 * HG_DIM + 2 * D_MODEL

OFF_CV, OFF_CG = 0, CONV_DIM
OFF_Q = 2 * CONV_DIM
OFF_F = OFF_Q + HG_DIM
OFF_V = OFF_F + HG_DIM
OFF_OG = OFF_V + HG_DIM
OFF_GA = OFF_OG + HG_DIM
OFF_GB = OFF_GA + D_MODEL

SUBLANES = 8
LANES = 128
HIST_ROWS = 32
HIST_PAD = HIST_ROWS - (CONV_WIDTH - 1)
CHUNK = 64
NBLK = CHUNK // SUBLANES
CONV_ROWS = 32
PROJ_COLS = 256
ROUTE_ROWS = 128
VMEM_LIMIT = 56 * 1024 * 1024

BF = jnp.bfloat16
F32 = jnp.float32


def _dot(a, b):
    return jnp.dot(a, b, preferred_element_type=F32)


def _dot_nt(a, b):
    return lax.dot_general(a, b, (((1,), (1,)), ((), ())), preferred_element_type=F32)


def _dot_tn(a, b, precision=None):
    return lax.dot_general(a, b, (((0,), (0,)), ((), ())), preferred_element_type=F32,
                           precision=precision)


NEG_LOG2E = -1.4426950408889634


def _sigmoid(x):
    return 1.0 / (1.0 + jnp.exp2(x * NEG_LOG2E))


def _silu(x):
    return x * _sigmoid(x)


def _rms(xf, g):
    return xf * lax.rsqrt(jnp.mean(xf * xf, axis=-1, keepdims=True) + EPS) * g


def _lower_bound(lb_param):
    m = jnp.max(lb_param, axis=0, keepdims=True)
    e = jnp.exp(lb_param - m)
    return e[0:1] / jnp.sum(e, axis=0, keepdims=True)


def _conv_post(c, conv_b, ln_g, ln_b):
    c = c + conv_b
    mu = jnp.mean(c, axis=-1, keepdims=True)
    d = c - mu
    var = jnp.mean(d * d, axis=-1, keepdims=True)
    return _silu(d * lax.rsqrt(var + EPS) * ln_g + ln_b)


def _head_norm(o, g):
    return o * lax.rsqrt(jnp.mean(o * o, axis=-1, keepdims=True) + EPS) * g


def _route(logits_t):
    used = -(-(N_GROUPS + N_EXPERTS) // SUBLANES) * SUBLANES
    logits = logits_t[0:used, :]
    row = lax.broadcasted_iota(jnp.int32, logits.shape, 0).astype(F32)
    big = jnp.float32(1 << 20)
    neg = jnp.float32(-jnp.inf)
    gmask = row < N_GROUPS
    lg = jnp.where(gmask, logits, neg)
    gmax = jnp.max(lg, axis=0, keepdims=True)
    gsum = jnp.sum(jnp.where(gmask, jnp.exp(lg - gmax), 0.0), axis=0, keepdims=True)
    gval = 1.0 / gsum
    gidx = jnp.min(jnp.where(lg == gmax, row, big), axis=0, keepdims=True)
    lo = N_GROUPS + EXPERTS_PER_GROUP * gidx
    emask = (row >= lo) & (row < lo + EXPERTS_PER_GROUP)
    el = jnp.where(emask, logits, neg)
    m1 = jnp.max(el, axis=0, keepdims=True)
    i1 = jnp.min(jnp.where(el == m1, row, big), axis=0, keepdims=True)
    el2 = jnp.where(row == i1, neg, el)
    m2 = jnp.max(el2, axis=0, keepdims=True)
    i2 = jnp.min(jnp.where(el2 == m2, row, big), axis=0, keepdims=True)
    r = jnp.exp(m2 - m1)
    w1 = gval / (1.0 + r)
    w2 = gval * r / (1.0 + r)
    out_row = lax.broadcasted_iota(jnp.int32, (SUBLANES, logits.shape[1]), 0)
    return jnp.where(out_row == 0, i1 - N_GROUPS,
                     jnp.where(out_row == 1, i2 - N_GROUPS, jnp.where(out_row == 2, w1, jnp.where(out_row == 3, w2, 0.0))))


LEVELS = (NBLK // 2, NBLK // 4, NBLK // 8)
ROW_TOT, ROW_QE, ROW_KD, ROW_LEVEL = 0, NBLK, 2 * NBLK, 3 * NBLK
TABLE_ROWS = ROW_LEVEL + 2 * NBLK * len(LEVELS)


def _row_bcast(ref, row):
    return jnp.broadcast_to(ref[pl.ds(row, 1), :], (SUBLANES, LANES))


def _per_block(ref, base):
    return jnp.concatenate([_row_bcast(ref, base + j) for j in range(NBLK)], axis=0)


def _block_rows(ref, c):
    return jnp.concatenate([_row_bcast(ref, j * SUBLANES + c) for j in range(NBLK)], axis=0)


def _sparse_tile(x, ref, base, blocks):
    groups = []
    for g in range(0, NBLK, 2):
        if g not in blocks and g + 1 not in blocks:
            groups.append(jnp.zeros((2 * SUBLANES, LANES), BF))
            continue
        halves = [x[j * SUBLANES:(j + 1) * SUBLANES, :] * _row_bcast(ref, base + j) if j in blocks
                  else jnp.zeros((SUBLANES, LANES), F32) for j in (g, g + 1)]
        groups.append(jnp.concatenate(halves, axis=0).astype(BF))
    return jnp.concatenate(groups, axis=0)


def _chunk_cumsum(l2, bl_ref):
    rin = lax.broadcasted_iota(jnp.int32, (CHUNK, LANES), 0) % SUBLANES
    bl = l2
    for s in (1, 2, 4):
        bl = bl + jnp.where(rin >= s, pltpu.roll(bl, s, axis=0), 0.0)
    bl_ref[...] = bl
    return bl


def _chunk_table(bl_ref, tab_ref):
    tot = bl_ref[pl.ds(SUBLANES - 1, NBLK, stride=SUBLANES), :]
    brow = lax.broadcasted_iota(jnp.int32, (NBLK, LANES), 0)
    rb = tot
    for s in (1, 2, 4):
        rb = rb + jnp.where(brow >= s, pltpu.roll(rb, s, axis=0), 0.0)
    rb_prev = rb - tot
    total = rb[NBLK - 1:NBLK, :]

    tab_ref[pl.ds(ROW_TOT, NBLK), :] = tot
    tab_ref[pl.ds(ROW_QE, NBLK), :] = jnp.exp2(rb_prev)
    tab_ref[pl.ds(ROW_KD, NBLK), :] = jnp.exp2(total - rb)
    for lv, cb in enumerate(LEVELS):
        edges = [(j // (2 * cb)) * (2 * cb) + cb - 1 for j in range(NBLK)]
        rb_edge = jnp.concatenate([rb[e:e + 1, :] for e in edges], axis=0)
        base = ROW_LEVEL + 2 * NBLK * lv
        tab_ref[pl.ds(base, NBLK), :] = jnp.exp2(jnp.minimum(rb_prev - rb_edge, 0.0))
        tab_ref[pl.ds(base + NBLK, NBLK), :] = jnp.exp2(jnp.minimum(rb_edge - rb, 0.0))
    return total


def _chunk_scaled(q, k, bl, tab_ref):
    return q * jnp.exp2(bl), k * jnp.exp2(_per_block(tab_ref, ROW_TOT) - bl)


def _chunk_inter_operands(qp, kp, tab_ref):
    return (qp * _per_block(tab_ref, ROW_QE)).astype(BF), (kp * _per_block(tab_ref, ROW_KD)).astype(BF)


def _chunk_pair_operands(qp, kp, tab_ref):
    q_tiles, k_tiles = [], []
    for lv, cb in enumerate(LEVELS):
        base = ROW_LEVEL + 2 * NBLK * lv
        for p0 in range(0, NBLK, 2 * cb):
            q_tiles.append(_sparse_tile(qp, tab_ref, base, range(p0 + cb, p0 + 2 * cb)))
            k_tiles.append(_sparse_tile(kp, tab_ref, base + NBLK, range(p0, p0 + cb)))
    return jnp.concatenate(q_tiles, axis=1), jnp.concatenate(k_tiles, axis=1)


def _chunk_block_operands(q, k, bl, row_masks, bl_ref):
    k_b = k.astype(BF)
    lhs = [(q * jnp.exp2(jnp.minimum(bl - _block_rows(bl_ref, c), 0.0))).astype(BF) for c in range(SUBLANES)]
    rhs = [k_b * row_masks[c] for c in range(SUBLANES)]
    return jnp.concatenate(lhs, axis=1), jnp.concatenate(rhs, axis=1)


def _chunk_products(qe, kdec, v_b, pair_ops, block_ops, st):
    return (_dot_nt(qe, st.astype(BF)), _dot_tn(v_b, kdec), _dot_nt(*pair_ops), _dot_nt(*block_ops))


def _chunk_output(o_inter, off_diag, diag, v_b):
    ti = lax.broadcasted_iota(jnp.int32, (CHUNK, CHUNK), 0)
    si = lax.broadcasted_iota(jnp.int32, (CHUNK, CHUNK), 1)
    scores = jnp.where((ti // SUBLANES == si // SUBLANES) & (si <= ti), diag, off_diag)
    return o_inter + _dot(scores.astype(BF), v_b)


def _mixer_tail(x, m, w_out, g_ffn, w_router_t):
    x1 = x + _dot(m.astype(BF), w_out)
    h2 = _rms(x1, g_ffn).astype(BF)
    route = _route(_dot_nt(w_router_t, h2))
    return x1, h2.astype(F32), route


def _prompt_mixer_kernel(x_ref, gmix_ref, win_ref, bg_ref, cw_ref, cb_ref, lng_ref, lnb_ref, wco_ref,
                         lbp_ref, hng_ref, who_ref, wout_ref, gffn_ref, wr_ref, masks_ref,
                         x1_ref, h2_ref, route_ref, nconv_ref, nhgrn_ref,
                         hist_ref, phase_ref, conv_ref, st_ref, q_s, k_s, v_s, lf_s, o_s, og_s, ga_s, gb_s, *head_scr, tb):
    t = pl.program_id(1)

    @pl.when(t == 0)
    def _():
        hist_ref[pl.ds(0, HIST_ROWS), :] = jnp.zeros((HIST_ROWS, CONV_DIM), F32)
        st_ref[...] = jnp.zeros_like(st_ref)

    x = x_ref[...]
    h = _rms(x, gmix_ref[...]).astype(BF)

    cv = _dot(h, win_ref[:, OFF_CV:OFF_CV + CONV_DIM])
    cg = _dot(h, win_ref[:, OFF_CG:OFF_CG + CONV_DIM])
    hist_ref[pl.ds(HIST_ROWS, tb), :] = cv * _sigmoid(cg)

    lb = _lower_bound(lbp_ref[...])

    def project(seg, c0):
        z = _dot(h, win_ref[:, seg + c0:seg + c0 + PROJ_COLS])
        cols = slice(c0, c0 + PROJ_COLS)
        if seg == OFF_Q:
            q_s[:, cols] = _silu(z) * (HG_DK ** -0.5)
        elif seg == OFF_F:
            fg = lb[:, cols] + (1.0 - lb[:, cols]) * _sigmoid(z)
            lf_s[:, cols] = jnp.log2(fg)
            k_s[:, cols] = 1.0 - fg
        elif seg == OFF_V:
            v_s[:, cols] = z
        elif seg == OFF_OG:
            og_s[:, cols] = _silu(z)
        elif seg == OFF_GA:
            ga_s[:, cols] = _sigmoid(z + bg_ref[:, c0:c0 + PROJ_COLS])
        else:
            gb_s[:, cols] = _sigmoid(z + bg_ref[:, D_MODEL + c0:D_MODEL + c0 + PROJ_COLS])

    proj_jobs = [functools.partial(project, seg, c0)
                 for seg in (OFF_Q, OFF_F, OFF_V, OFF_OG, OFF_GA, OFF_GB) for c0 in range(0, HG_DIM, PROJ_COLS)]

    span = tb + SUBLANES * (-(-CONV_WIDTH // SUBLANES) - 1)

    def phase_copy(r):
        n = min(span, HIST_ROWS + tb - HIST_PAD - r)
        phase_ref[r, pl.ds(0, n), :] = hist_ref[pl.ds(HIST_PAD + r, n), :]

    def conv_rows(r0):
        acc = jnp.zeros((CONV_ROWS, CONV_DIM), F32)
        for j in range(CONV_WIDTH):
            w_j = jnp.broadcast_to(cw_ref[pl.ds(j, 1), :], (SUBLANES, CONV_DIM))
            acc = acc + (phase_ref[j % SUBLANES, pl.ds(r0 + j - j % SUBLANES, CONV_ROWS), :]
                         * jnp.concatenate([w_j] * (CONV_ROWS // SUBLANES), axis=0))
        conv_ref[pl.ds(r0, CONV_ROWS), :] = acc

    conv_jobs = ([functools.partial(phase_copy, r) for r in range(SUBLANES)]
                 + [functools.partial(conv_rows, r0) for r0 in range(0, tb, CONV_ROWS)])
    per_conv_job = -(-len(proj_jobs) // len(conv_jobs))
    for i, job in enumerate(conv_jobs):
        job()
        for pj in proj_jobs[i * per_conv_job:(i + 1) * per_conv_job]:
            pj()
    for pj in proj_jobs[len(conv_jobs) * per_conv_job:]:
        pj()

    tail = hist_ref[pl.ds(tb, HIST_ROWS), :]
    hist_ref[pl.ds(0, HIST_ROWS), :] = tail
    nconv_ref[0] = tail
    c = _conv_post(conv_ref[...], cb_ref[...], lng_ref[...], lnb_ref[...])
    y_a = _dot(c.astype(BF), wco_ref[...])

    row_masks = masks_ref[...]
    cols = [slice(hd * HG_DK, (hd + 1) * HG_DK) for hd in range(HG_HEADS)]
    bl_refs, tab_refs = head_scr[:HG_HEADS], head_scr[HG_HEADS:]
    heads = range(HG_HEADS)
    for r0 in range(0, tb, CHUNK):
        rows = pl.ds(r0, CHUNK)
        q = [q_s[rows, cols[hd]] for hd in heads]
        k = [k_s[rows, cols[hd]] for hd in heads]
        bl = [_chunk_cumsum(lf_s[rows, cols[hd]], bl_refs[hd]) for hd in heads]
        total = [_chunk_table(bl_refs[hd], tab_refs[hd]) for hd in heads]
        scaled = [_chunk_scaled(q[hd], k[hd], bl[hd], tab_refs[hd]) for hd in heads]
        inter = [_chunk_inter_operands(*scaled[hd], tab_refs[hd]) for hd in heads]
        pair_ops = [_chunk_pair_operands(*scaled[hd], tab_refs[hd]) for hd in heads]
        block_ops = [_chunk_block_operands(q[hd], k[hd], bl[hd], row_masks, bl_refs[hd]) for hd in heads]
        v_b = [v_s[rows, cols[hd]].astype(BF) for hd in heads]
        prods = [_chunk_products(*inter[hd], v_b[hd], pair_ops[hd], block_ops[hd], st_ref[hd]) for hd in heads]
        for hd in heads:
            o_inter, update, off_diag, diag = prods[hd]
            st_ref[hd] = st_ref[hd] * jnp.exp2(total[hd]) + update
            o = _chunk_output(o_inter, off_diag, diag, v_b[hd])
            o_s[rows, cols[hd]] = _head_norm(o, hng_ref[:, cols[hd]])
    for hd in range(HG_HEADS):
        nhgrn_ref[0, hd] = st_ref[hd].T

    y_b = _dot((o_s[...] * og_s[...]).astype(BF), who_ref[...])
    x1, h2, route = _mixer_tail(x, ga_s[...] * y_a + gb_s[...] * y_b, wout_ref[...], gffn_ref[...], wr_ref[...])
    x1_ref[...] = x1
    _store_row_tiles(h2_ref, h2)
    route_ref[...] = route


def _const_spec(shape):
    nd = len(shape)
    return pl.BlockSpec(shape, lambda *_: (0,) * nd, pipeline_mode=pl.Buffered(1))


def _row_masks():
    c = jnp.arange(SUBLANES)[:, None, None]
    r = jnp.arange(CHUNK)[None, :, None] % SUBLANES
    return jnp.broadcast_to(r == c, (SUBLANES, CHUNK, LANES)).astype(BF)


def _prompt_mixer(x2, batch, seq, tb, wts):
    nt = seq // tb
    n = batch * seq
    row_spec = lambda w: pl.BlockSpec((tb, w), lambda b, t: (b * nt + t, 0))
    consts = [wts['gmix'], wts['w_in'], wts['b_gates'], wts['conv_w'], wts['conv_b'], wts['ln_g'], wts['ln_b'],
              wts['w_conv_out'], wts['lb_param'], wts['hn_g'], wts['w_hgrn_out'], wts['w_out'], wts['gffn'],
              wts['w_router'], _row_masks()]
    span = tb + SUBLANES * (-(-CONV_WIDTH // SUBLANES) - 1)
    return pl.pallas_call(
        functools.partial(_prompt_mixer_kernel, tb=tb),
        grid=(batch, nt),
        in_specs=[row_spec(D_MODEL)] + [_const_spec(c.shape) for c in consts],
        out_specs=[row_spec(D_MODEL), pl.BlockSpec((tb * SUBLANES, LANES), lambda b, t: (b * nt + t, 0)),
                   pl.BlockSpec((SUBLANES, tb), lambda b, t: (0, b * nt + t)),
                   pl.BlockSpec((1, HIST_ROWS, CONV_DIM), lambda b, t: (b, 0, 0)),
                   pl.BlockSpec((1, HG_HEADS, HG_DK, HG_DV), lambda b, t: (b, 0, 0, 0))],
        out_shape=[jax.ShapeDtypeStruct((n, D_MODEL), F32), jax.ShapeDtypeStruct((n * SUBLANES, LANES), F32),
                   jax.ShapeDtypeStruct((SUBLANES, n), F32),
                   jax.ShapeDtypeStruct((batch, HIST_ROWS, CONV_DIM), F32),
                   jax.ShapeDtypeStruct((batch, HG_HEADS, HG_DK, HG_DV), F32)],
        scratch_shapes=[pltpu.VMEM((HIST_ROWS + tb, CONV_DIM), F32),
                        pltpu.VMEM((SUBLANES, span, CONV_DIM), F32),
                        pltpu.VMEM((tb, CONV_DIM), F32),
                        pltpu.VMEM((HG_HEADS, HG_DV, HG_DK), F32)]
                       + [pltpu.VMEM((tb, HG_DIM), F32)] * 8
                       + [pltpu.VMEM((CHUNK, LANES), F32)] * HG_HEADS
                       + [pltpu.VMEM((TABLE_ROWS, LANES), F32)] * HG_HEADS,
        compiler_params=pltpu.CompilerParams(dimension_semantics=("arbitrary", "arbitrary"),
                                             vmem_limit_bytes=VMEM_LIMIT),
        name="prompt_mixer",
    )(x2, *consts)


def _sample_proj_kernel(x_ref, gmix_ref, win_ref, proj_ref):
    h = _rms(x_ref[...], gmix_ref[...]).astype(BF)
    proj_ref[...] = _dot(h, win_ref[...])


def _sample_proj(xs, wts):
    n = xs.shape[0]
    return pl.pallas_call(
        _sample_proj_kernel,
        grid=(1,),
        in_specs=[_const_spec(xs.shape), _const_spec(wts['gmix'].shape), _const_spec(wts['w_in'].shape)],
        out_specs=pl.BlockSpec((n, IN_COLS), lambda i: (0, 0)),
        out_shape=jax.ShapeDtypeStruct((n, IN_COLS), F32),
        compiler_params=pltpu.CompilerParams(vmem_limit_bytes=VMEM_LIMIT),
        name="sample_proj",
    )(xs, wts['gmix'], wts['w_in'])


def _split3(x):
    hi = x.astype(BF).astype(F32)
    mid = (x - hi).astype(BF).astype(F32)
    lo = ((x - hi) - mid).astype(BF).astype(F32)
    return hi, mid, lo


def _sample_state_kernel(q_ref, f_ref, v_ref, lbp_ref, hng_ref, lane_masks_ref, s_ref, snew_ref, o_ref, *, bs):
    assert bs == SUBLANES
    lb = _lower_bound(lbp_ref[...])
    qf = _silu(q_ref[...]) * (HG_DK ** -0.5)
    fg = lb + (1.0 - lb) * _sigmoid(f_ref[...])
    kf = 1.0 - fg
    v = v_ref[...]
    ones = jnp.ones((bs, HG_DV), F32)
    zeros = jnp.zeros((bs, HG_DV), F32)
    n_groups = 9
    pad = jnp.zeros((LANES - n_groups * bs, LANES), F32)
    for hd in range(HG_HEADS):
        cs = slice(hd * HG_DK, (hd + 1) * HG_DK)
        f3, k3, q3, v3 = _split3(fg[:, cs]), _split3(kf[:, cs]), _split3(qf[:, cs]), _split3(v[:, cs])
        left = jnp.concatenate([f3[0], f3[1], f3[2], k3[0], k3[1], k3[0], q3[0], q3[1], q3[2], pad], axis=0)
        left_t = left.T.astype(BF)
        right = jnp.concatenate(
            [jnp.concatenate(blk, axis=1) for blk in
             [(ones, zeros, zeros)] * 3 + [(zeros, v3[0], zeros), (zeros, v3[0], zeros), (zeros, v3[1], zeros)]
             + [(zeros, zeros, ones)] * 3] + [jnp.concatenate((pad, pad, pad), axis=1)], axis=0).astype(BF)
        o_rows = []
        for r in range(bs):
            prod = _dot(left_t * lane_masks_ref[r], right)
            s_new = prod[:, :HG_DV] * s_ref[r, hd] + prod[:, HG_DV:2 * HG_DV]
            snew_ref[r, hd] = s_new
            o_rows.append(jnp.sum(prod[:, 2 * HG_DV:] * s_new, axis=0, keepdims=True))
        o_ref[:, cs] = _head_norm(jnp.concatenate(o_rows, axis=0), hng_ref[:, cs])


def _lane_masks():
    r = jnp.arange(SUBLANES)[:, None, None]
    lane = jnp.arange(LANES)[None, None, :] % SUBLANES
    return jnp.broadcast_to(lane == r, (SUBLANES, LANES, LANES)).astype(BF)


def _sample_state(proj, state, wts, bs):
    n = proj.shape[0]
    col_spec = lambda off: pl.BlockSpec((bs, HG_DIM), lambda i, off=off: (i, off // HG_DIM))
    st_spec = pl.BlockSpec((bs, HG_HEADS, HG_DK, HG_DV), lambda i: (i, 0, 0, 0))
    return pl.pallas_call(
        functools.partial(_sample_state_kernel, bs=bs),
        grid=(n // bs,),
        in_specs=[col_spec(OFF_Q), col_spec(OFF_F), col_spec(OFF_V),
                  _const_spec(wts['lb_param'].shape), _const_spec(wts['hn_g'].shape),
                  _const_spec((SUBLANES, LANES, LANES)), st_spec],
        out_specs=[st_spec, pl.BlockSpec((bs, HG_DIM), lambda i: (i, 0))],
        out_shape=[jax.ShapeDtypeStruct(state.shape, F32), jax.ShapeDtypeStruct((n, HG_DIM), F32)],
        compiler_params=pltpu.CompilerParams(dimension_semantics=("arbitrary",), vmem_limit_bytes=VMEM_LIMIT),
        name="sample_state",
    )(proj, proj, proj, wts['lb_param'], wts['hn_g'], _lane_masks(), state)


def _sample_tail_kernel(x_ref, proj_ref, o_ref, cst_ref, bg_ref, cw_ref, cb_ref, lng_ref, lnb_ref, wco_ref,
                        who_ref, wout_ref, gffn_ref, wr_ref, x1_ref, h2_ref, route_ref, nconv_ref):
    keep = (CONV_WIDTH - 2) * CONV_DIM
    u = proj_ref[:, OFF_CV:OFF_CV + CONV_DIM] * _sigmoid(proj_ref[:, OFF_CG:OFF_CG + CONV_DIM])
    acc = u * cw_ref[pl.ds(CONV_WIDTH - 1, 1), :]
    for j in range(CONV_WIDTH - 1):
        acc = acc + cst_ref[:, j * CONV_DIM:(j + 1) * CONV_DIM] * cw_ref[pl.ds(j, 1), :]
    nconv_ref[:, :keep] = cst_ref[:, CONV_DIM:]
    nconv_ref[:, keep:] = u
    c = _conv_post(acc, cb_ref[...], lng_ref[...], lnb_ref[...])
    y_a = _dot(c.astype(BF), wco_ref[...])
    og = proj_ref[:, OFF_OG:OFF_OG + HG_DIM]
    y_b = _dot((o_ref[...] * _silu(og)).astype(BF), who_ref[...])
    m = (_sigmoid(proj_ref[:, OFF_GA:OFF_GA + D_MODEL] + bg_ref[:, :D_MODEL]) * y_a
         + _sigmoid(proj_ref[:, OFF_GB:OFF_GB + D_MODEL] + bg_ref[:, D_MODEL:]) * y_b)
    x1, h2, route = _mixer_tail(x_ref[...], m, wout_ref[...], gffn_ref[...], wr_ref[...])
    x1_ref[...] = x1
    _store_row_tiles(h2_ref, h2)
    route_ref[...] = route


def _sample_tail(xs, proj, o, conv_state2, wts):
    n = xs.shape[0]
    ins = [xs, proj, o, conv_state2, wts['b_gates'], wts['conv_w'], wts['conv_b'], wts['ln_g'], wts['ln_b'],
           wts['w_conv_out'], wts['w_hgrn_out'], wts['w_out'], wts['gffn'], wts['w_router']]
    full = lambda shape: pl.BlockSpec(shape, lambda i: (0,) * len(shape))
    return pl.pallas_call(
        _sample_tail_kernel,
        grid=(1,),
        in_specs=[_const_spec(a.shape) for a in ins],
        out_specs=[full((n, D_MODEL)), full((n * SUBLANES, LANES)), full((SUBLANES, n)), full(conv_state2.shape)],
        out_shape=[jax.ShapeDtypeStruct((n, D_MODEL), F32), jax.ShapeDtypeStruct((n * SUBLANES, LANES), F32),
                   jax.ShapeDtypeStruct((SUBLANES, n), F32), jax.ShapeDtypeStruct(conv_state2.shape, F32)],
        compiler_params=pltpu.CompilerParams(vmem_limit_bytes=VMEM_LIMIT),
        name="sample_tail",
    )(*ins)


def _plan_kernel(route_p_ref, route_s_ref, pos_ref, tinfo_ref, e_ref, pre_ref, *, tm, n_tok, max_tiles):
    n_p = route_p_ref.shape[1]
    nblk = n_tok // LANES
    e_ref[:, pl.ds(0, n_p)] = route_p_ref[...]
    e_ref[:, pl.ds(n_p, n_tok - n_p)] = route_s_ref[...]
    eidx = lax.broadcasted_iota(jnp.int32, (N_EXPERTS, LANES), 0).astype(F32)
    ti = lax.broadcasted_iota(jnp.int32, (LANES, LANES), 0)
    si = lax.broadcasted_iota(jnp.int32, (LANES, LANES), 1)
    before = (ti < si).astype(BF)

    def onehots(b):
        c0 = pl.multiple_of(b * LANES, LANES)
        e1 = e_ref[0:1, pl.ds(c0, LANES)]
        e2 = e_ref[1:2, pl.ds(c0, LANES)]
        return (e1 == eidx).astype(F32), (e2 == eidx).astype(F32), c0

    def count(b, carry):
        h1, h2, c0 = onehots(b)
        h = h1 + h2
        pre_ref[:, pl.ds(c0, LANES)] = _dot(h.astype(BF), before) + carry
        return carry + jnp.sum(h, axis=1, keepdims=True)

    counts = lax.fori_loop(0, nblk, count, jnp.zeros((N_EXPERTS, 1), F32))
    tiles_per = jnp.floor((counts + (tm - 1)) * (1.0 / tm))
    ei = lax.broadcasted_iota(jnp.int32, (N_EXPERTS, N_EXPERTS), 0)
    ej = lax.broadcasted_iota(jnp.int32, (N_EXPERTS, N_EXPERTS), 1)
    upto = (ej <= ei).astype(BF)
    tile_end = _dot(upto, jnp.broadcast_to(tiles_per, (N_EXPERTS, LANES)).astype(BF))[:, 0:1]
    starts = (tile_end - tiles_per) * tm

    def place(b, c):
        h1, h2, c0 = onehots(b)
        dest = pre_ref[:, pl.ds(c0, LANES)] + starts
        pos_ref[0:1, pl.ds(c0, LANES)] = jnp.sum(h1 * dest, axis=0, keepdims=True).astype(jnp.int32)
        pos_ref[1:2, pl.ds(c0, LANES)] = jnp.sum(h2 * dest, axis=0, keepdims=True).astype(jnp.int32)
        return c

    pos_ref[...] = jnp.zeros_like(pos_ref)
    lax.fori_loop(0, nblk, place, 0)

    tile = lax.broadcasted_iota(jnp.int32, (N_EXPERTS, max_tiles), 1).astype(F32)
    t_exp = jnp.sum((tile_end <= tile).astype(F32), axis=0, keepdims=True)
    t_exp = jnp.minimum(t_exp, N_EXPERTS - 1.0).astype(jnp.int32)
    n_used = jnp.broadcast_to(tile_end[N_EXPERTS - 1:N_EXPERTS, :], (1, max_tiles)).astype(jnp.int32)
    row = lax.broadcasted_iota(jnp.int32, (SUBLANES, max_tiles), 0)
    tinfo_ref[...] = jnp.where(row == 0, t_exp, jnp.where(row == 1, n_used, 0))


def _plan(route_p, route_s, tm, max_tiles):
    n_tok = route_p.shape[1] + route_s.shape[1]
    assert route_p.shape[1] % LANES == 0 and route_s.shape[1] % LANES == 0
    vm = pl.BlockSpec(memory_space=pltpu.VMEM)
    return pl.pallas_call(
        functools.partial(_plan_kernel, tm=tm, n_tok=n_tok, max_tiles=max_tiles),
        in_specs=[vm, vm],
        out_specs=[vm, vm],
        out_shape=[jax.ShapeDtypeStruct((SUBLANES, n_tok), jnp.int32),
                   jax.ShapeDtypeStruct((SUBLANES, max_tiles), jnp.int32)],
        scratch_shapes=[pltpu.VMEM((SUBLANES, n_tok), F32), pltpu.VMEM((N_EXPERTS, n_tok), F32)],
        compiler_params=pltpu.CompilerParams(vmem_limit_bytes=VMEM_LIMIT),
        name="route_plan",
    )(route_p, route_s)


def _dispatch_kernel(pos_ref, h_ref, xs_in_ref, xs_ref, sem, *, tb, tok0):
    del xs_in_ref
    i = pl.program_id(0)

    def copies(r):
        base = 2 * (tok0 + i * tb + r)
        tile = lambda ref, row: ref.at[pl.ds(pl.multiple_of(row * SUBLANES, SUBLANES), SUBLANES), :]
        return (pltpu.make_async_copy(tile(h_ref, r), tile(xs_ref, pos_ref[base]), sem),
                pltpu.make_async_copy(tile(h_ref, r), tile(xs_ref, pos_ref[base + 1]), sem))

    def issue(r, c):
        c0, c1 = copies(r)
        c0.start()
        c1.start()
        return c

    def drain(r, c):
        c0, c1 = copies(r)
        c0.wait()
        c1.wait()
        return c

    lax.fori_loop(0, tb, issue, 0, unroll=8)
    lax.fori_loop(0, tb, drain, 0, unroll=8)


def _dispatch(pos_flat, hpk, xs, tb, tok0):
    n = hpk.shape[0] // SUBLANES
    grid_spec = pltpu.PrefetchScalarGridSpec(
        num_scalar_prefetch=1,
        grid=(n // tb,),
        in_specs=[pl.BlockSpec((tb * SUBLANES, LANES), lambda i, p: (i, 0)),
                  pl.BlockSpec(memory_space=pl.ANY)],
        out_specs=pl.BlockSpec(memory_space=pl.ANY),
        scratch_shapes=[pltpu.SemaphoreType.DMA(())])
    return pl.pallas_call(
        functools.partial(_dispatch_kernel, tb=tb, tok0=tok0),
        grid_spec=grid_spec,
        out_shape=jax.ShapeDtypeStruct(xs.shape, xs.dtype),
        input_output_aliases={2: 0},
        compiler_params=pltpu.CompilerParams(dimension_semantics=("arbitrary",), vmem_limit_bytes=VMEM_LIMIT),
        name="dispatch",
    )(pos_flat, hpk, xs)


def _store_row_tiles(ref, rows):
    n = rows.shape[0]
    for s in range(SUBLANES):
        ref[pl.ds(s, n, stride=SUBLANES), :] = rows[:, s * LANES:(s + 1) * LANES]


def _load_row_tiles(ref):
    n = ref.shape[0] // SUBLANES
    return jnp.concatenate([ref[pl.ds(s, n, stride=SUBLANES), :] for s in range(SUBLANES)], axis=1)


def _expert_kernel(te_ref, nt_ref, xs_ref, wg_ref, wu_ref, wd_ref, y_ref, wg_b, wu_b, wd_b):
    i = pl.program_id(0)

    @pl.when(i < nt_ref[0])
    def _():
        changed = jnp.logical_or(i == 0, te_ref[i] != te_ref[jnp.maximum(i - 1, 0)])

        @pl.when(changed)
        def _():
            wg_b[...] = wg_ref[0].astype(BF)
            wu_b[...] = wu_ref[0].astype(BF)
            wd_b[...] = wd_ref[0].astype(BF)

        xb = _load_row_tiles(xs_ref).astype(BF)
        gate = _dot(xb, wg_b[...])
        up = _dot(xb, wu_b[...])
        _store_row_tiles(y_ref, _dot((_silu(gate) * up).astype(BF), wd_b[...]))

    @pl.when(i >= nt_ref[0])
    def _():
        y_ref[...] = jnp.zeros_like(y_ref)


def _experts(xs, tile_expert, n_tiles, wg, wu, wd, tm, max_tiles):
    grid_spec = pltpu.PrefetchScalarGridSpec(
        num_scalar_prefetch=2,
        grid=(max_tiles,),
        in_specs=[pl.BlockSpec((tm * SUBLANES, LANES),
                               lambda i, te, nt: (jnp.minimum(i, nt[0] - 1), 0)),
                  pl.BlockSpec((1, D_MODEL, EXPERT_FF), lambda i, te, nt: (te[i], 0, 0)),
                  pl.BlockSpec((1, D_MODEL, EXPERT_FF), lambda i, te, nt: (te[i], 0, 0)),
                  pl.BlockSpec((1, EXPERT_FF, D_MODEL), lambda i, te, nt: (te[i], 0, 0))],
        out_specs=pl.BlockSpec((tm * SUBLANES, LANES), lambda i, te, nt: (i, 0)),
        scratch_shapes=[pltpu.VMEM((D_MODEL, EXPERT_FF), BF), pltpu.VMEM((D_MODEL, EXPERT_FF), BF),
                        pltpu.VMEM((EXPERT_FF, D_MODEL), BF)])
    return pl.pallas_call(
        _expert_kernel,
        grid_spec=grid_spec,
        out_shape=jax.ShapeDtypeStruct((max_tiles * tm * SUBLANES, LANES), F32),
        compiler_params=pltpu.CompilerParams(dimension_semantics=("arbitrary",), vmem_limit_bytes=VMEM_LIMIT),
        name="experts",
    )(tile_expert, n_tiles, xs, wg, wu, wd)


def _combine_kernel(pos_ref, x1_ref, route_ref, gfin_ref, ys_ref, y_ref, buf0, buf1, sems, *, tb, tok0):
    i = pl.program_id(0)
    n_steps = pl.num_programs(0)
    slot = i % 2

    def copies(step, half, r):
        base = 2 * (tok0 + step * tb + r)
        tile = lambda ref, row: ref.at[pl.ds(pl.multiple_of(row * SUBLANES, SUBLANES), SUBLANES), :]
        return (pltpu.make_async_copy(tile(ys_ref, pos_ref[base]), tile(buf0.at[half], r), sems.at[half]),
                pltpu.make_async_copy(tile(ys_ref, pos_ref[base + 1]), tile(buf1.at[half], r), sems.at[half]))

    def request(step, half):
        def body(r, c):
            c0, c1 = copies(step, half, r)
            c0.start()
            c1.start()
            return c

        lax.fori_loop(0, tb, body, 0, unroll=8)

    @pl.when(i == 0)
    def _():
        request(0, 0)

    @pl.when(i + 1 < n_steps)
    def _():
        request(i + 1, 1 - slot)

    def receive(r, c):
        c0, c1 = copies(i, slot, r)
        c0.wait()
        c1.wait()
        return c

    lax.fori_loop(0, tb, receive, 0, unroll=8)
    pad = jnp.zeros((LANES - SUBLANES, LANES), F32)
    cols = jnp.concatenate([jnp.concatenate([route_ref[:, j:j + LANES], pad], axis=0).T
                            for j in range(0, tb, LANES)], axis=0)
    out = x1_ref[...] + (cols[:, 2:3] * _load_row_tiles(buf0.at[slot]) + cols[:, 3:4] * _load_row_tiles(buf1.at[slot]))
    y_ref[...] = _rms(out, gfin_ref[...])


def _combine(pos_flat, x1, route, gfin, ys, tb, tok0):
    n = x1.shape[0]
    grid_spec = pltpu.PrefetchScalarGridSpec(
        num_scalar_prefetch=1,
        grid=(n // tb,),
        in_specs=[pl.BlockSpec((tb, D_MODEL), lambda i, p: (i, 0)),
                  pl.BlockSpec((SUBLANES, tb), lambda i, p: (0, i)),
                  pl.BlockSpec((1, D_MODEL), lambda i, p: (0, 0)),
                  pl.BlockSpec(memory_space=pl.ANY)],
        out_specs=pl.BlockSpec((tb, D_MODEL), lambda i, p: (i, 0)),
        scratch_shapes=[pltpu.VMEM((2, tb * SUBLANES, LANES), F32), pltpu.VMEM((2, tb * SUBLANES, LANES), F32),
                        pltpu.SemaphoreType.DMA((2,))])
    return pl.pallas_call(
        functools.partial(_combine_kernel, tb=tb, tok0=tok0),
        grid_spec=grid_spec,
        out_shape=jax.ShapeDtypeStruct((n, D_MODEL), F32),
        compiler_params=pltpu.CompilerParams(dimension_semantics=("arbitrary",), vmem_limit_bytes=VMEM_LIMIT),
        name="combine",
    )(pos_flat, x1, route, gfin, ys)


def kernel(x_prompt, x_sample, state_conv, state_hgrn, norm_mix_g, w_in, b_gates, conv_dw_w, conv_dw_b,
           conv_ln_g, conv_ln_b, w_conv_out, hgrn_lb_param, hgrn_norm_g, w_hgrn_out, w_out, norm_ffn_g,
           w_router_group, w_router_expert, w_expert_gate, w_expert_up, w_expert_down, norm_final_g):
    batch, seq, _ = x_prompt.shape
    dec_batch = x_sample.shape[0]
    assert x_sample.shape[1] == 1 and w_in.shape[0] == 1
    tb = min(256, seq)
    tm = 256
    bs = min(8, dec_batch)
    n_p = batch * seq
    n_all = n_p + dec_batch
    tbd = min(1024, n_p)
    tbc = min(512, n_p)
    assert seq % tb == 0 and tb % CHUNK == 0 and dec_batch % bs == 0 and n_p % tbd == 0 and n_p % tbc == 0

    w_router = jnp.concatenate(
        [w_router_group[0].T, w_router_expert[0].T,
         jnp.zeros((ROUTE_ROWS - N_GROUPS - N_EXPERTS, D_MODEL), F32)], axis=0).astype(BF)
    wts = dict(gmix=norm_mix_g, w_in=w_in[0].astype(BF), b_gates=b_gates, conv_w=conv_dw_w[0], conv_b=conv_dw_b,
               ln_g=conv_ln_g, ln_b=conv_ln_b, w_conv_out=w_conv_out[0].astype(BF), lb_param=hgrn_lb_param,
               hn_g=hgrn_norm_g, w_hgrn_out=w_hgrn_out[0].astype(BF), w_out=w_out[0].astype(BF),
               gffn=norm_ffn_g, w_router=w_router)

    x1_p, hpk_p, route_p, nconv_p, nhgrn_p = _prompt_mixer(x_prompt.reshape(n_p, D_MODEL), batch, seq, tb, wts)

    xs_tok = x_sample.reshape(dec_batch, D_MODEL)
    proj_s = _sample_proj(xs_tok, wts)
    nhgrn_s, o_s = _sample_state(proj_s, state_hgrn[0], wts, bs)
    conv2 = state_conv[0].reshape(dec_batch, (CONV_WIDTH - 1) * CONV_DIM)
    x1_s, hpk_s, route_s, nconv_s = _sample_tail(xs_tok, proj_s, o_s, conv2, wts)

    max_tiles = -(-((2 * n_all) // tm + N_EXPERTS) // SUBLANES) * SUBLANES
    pos, tinfo = _plan(route_p, route_s, tm, max_tiles)
    pos_flat = pos[0:2].T.reshape(-1)
    tile_expert, n_tiles = tinfo[0], tinfo[1, 0:1]

    xs = jnp.zeros((max_tiles * tm * SUBLANES, LANES), F32)
    xs = _dispatch(pos_flat, hpk_p, xs, tbd, 0)
    xs = _dispatch(pos_flat, hpk_s, xs, dec_batch, n_p)
    ys = _experts(xs, tile_expert, n_tiles, w_expert_gate[0], w_expert_up[0], w_expert_down[0], tm, max_tiles)

    gfin = norm_final_g.reshape(1, D_MODEL)
    y_p = _combine(pos_flat, x1_p, route_p, gfin, ys, tbc, 0)
    y_s = _combine(pos_flat, x1_s, route_s, gfin, ys, dec_batch, n_p)

    return (y_p.reshape(batch, seq, D_MODEL), y_s.reshape(dec_batch, 1, D_MODEL),
            nconv_p[None, :, HIST_PAD:, :], nhgrn_p[None],
            nconv_s.reshape(1, dec_batch, CONV_WIDTH - 1, CONV_DIM), nhgrn_s[None])
```

```python
import functools

import jax
import jax.numpy as jnp
from jax import lax
from jax.experimental import pallas as pl
from jax.experimental.pallas import tpu as pltpu

D_MODEL = 1024
CONV_DIM = D_MODEL // 2
CONV_WIDTH = 31
HG_HEADS = 8
HG_DK = 128
HG_DV = 128
HG_DIM = HG_HEADS * HG_DK
N_GROUPS = 4
EXPERTS_PER_GROUP = 8
N_EXPERTS = N_GROUPS * EXPERTS_PER_GROUP
EXPERT_FF = D_MODEL // 2
EPS = 1e-6
IN_COLS = 2 * CONV_DIM + 4 * HG_DIM + 2 * D_MODEL

OFF_CV, OFF_CG = 0, CONV_DIM
OFF_Q = 2 * CONV_DIM
OFF_F = OFF_Q + HG_DIM
OFF_V = OFF_F + HG_DIM
OFF_OG = OFF_V + HG_DIM
OFF_GA = OFF_OG + HG_DIM
OFF_GB = OFF_GA + D_MODEL

SUBLANES = 8
LANES = 128
HIST_ROWS = 32
HIST_PAD = HIST_ROWS - (CONV_WIDTH - 1)
CHUNK = 64
NBLK = CHUNK // SUBLANES
CONV_ROWS = 32
PROJ_COLS = 256
ROUTE_ROWS = 128
VMEM_LIMIT = 56 * 1024 * 1024

BF = jnp.bfloat16
F32 = jnp.float32


def _dot(a, b):
    return jnp.dot(a, b, preferred_element_type=F32)


def _dot_nt(a, b):
    return lax.dot_general(a, b, (((1,), (1,)), ((), ())), preferred_element_type=F32)


def _dot_tn(a, b, precision=None):
    return lax.dot_general(a, b, (((0,), (0,)), ((), ())), preferred_element_type=F32,
                           precision=precision)


NEG_LOG2E = -1.4426950408889634


def _sigmoid(x):
    return 1.0 / (1.0 + jnp.exp2(x * NEG_LOG2E))


def _silu(x):
    return x * _sigmoid(x)


def _rms(xf, g):
    return xf * lax.rsqrt(jnp.mean(xf * xf, axis=-1, keepdims=True) + EPS) * g


def _lower_bound(lb_param):
    m = jnp.max(lb_param, axis=0, keepdims=True)
    e = jnp.exp(lb_param - m)
    return e[0:1] / jnp.sum(e, axis=0, keepdims=True)


def _conv_post(c, conv_b, ln_g, ln_b):
    c = c + conv_b
    mu = jnp.mean(c, axis=-1, keepdims=True)
    d = c - mu
    var = jnp.mean(d * d, axis=-1, keepdims=True)
    return _silu(d * lax.rsqrt(var + EPS) * ln_g + ln_b)


def _head_norm(o, g):
    return o * lax.rsqrt(jnp.mean(o * o, axis=-1, keepdims=True) + EPS) * g


def _route(logits_t):
    used = -(-(N_GROUPS + N_EXPERTS) // SUBLANES) * SUBLANES
    logits = logits_t[0:used, :]
    row = lax.broadcasted_iota(jnp.int32, logits.shape, 0).astype(F32)
    big = jnp.float32(1 << 20)
    neg = jnp.float32(-jnp.inf)
    gmask = row < N_GROUPS
    lg = jnp.where(gmask, logits, neg)
    gmax = jnp.max(lg, axis=0, keepdims=True)
    gsum = jnp.sum(jnp.where(gmask, jnp.exp(lg - gmax), 0.0), axis=0, keepdims=True)
    gval = 1.0 / gsum
    gidx = jnp.min(jnp.where(lg == gmax, row, big), axis=0, keepdims=True)
    lo = N_GROUPS + EXPERTS_PER_GROUP * gidx
    emask = (row >= lo) & (row < lo + EXPERTS_PER_GROUP)
    el = jnp.where(emask, logits, neg)
    m1 = jnp.max(el, axis=0, keepdims=True)
    i1 = jnp.min(jnp.where(el == m1, row, big), axis=0, keepdims=True)
    el2 = jnp.where(row == i1, neg, el)
    m2 = jnp.max(el2, axis=0, keepdims=True)
    i2 = jnp.min(jnp.where(el2 == m2, row, big), axis=0, keepdims=True)
    r = jnp.exp(m2 - m1)
    w1 = gval / (1.0 + r)
    w2 = gval * r / (1.0 + r)
    out_row = lax.broadcasted_iota(jnp.int32, (SUBLANES, logits.shape[1]), 0)
    return jnp.where(out_row == 0, i1 - N_GROUPS,
                     jnp.where(out_row == 1, i2 - N_GROUPS, jnp.where(out_row == 2, w1, jnp.where(out_row == 3, w2, 0.0))))


LEVELS = (NBLK // 2, NBLK // 4, NBLK // 8)
ROW_TOT, ROW_QE, ROW_KD, ROW_LEVEL = 0, NBLK, 2 * NBLK, 3 * NBLK
TABLE_ROWS = ROW_LEVEL + 2 * NBLK * len(LEVELS)


def _row_bcast(ref, row):
    return jnp.broadcast_to(ref[pl.ds(row, 1), :], (SUBLANES, LANES))


def _per_block(ref, base):
    return jnp.concatenate([_row_bcast(ref, base + j) for j in range(NBLK)], axis=0)


def _block_rows(ref, c):
    return jnp.concatenate([_row_bcast(ref, j * SUBLANES + c) for j in range(NBLK)], axis=0)


def _sparse_tile(x, ref, base, blocks):
    groups = []
    for g in range(0, NBLK, 2):
        if g not in blocks and g + 1 not in blocks:
            groups.append(jnp.zeros((2 * SUBLANES, LANES), BF))
            continue
        halves = [x[j * SUBLANES:(j + 1) * SUBLANES, :] * _row_bcast(ref, base + j) if j in blocks
                  else jnp.zeros((SUBLANES, LANES), F32) for j in (g, g + 1)]
        groups.append(jnp.concatenate(halves, axis=0).astype(BF))
    return jnp.concatenate(groups, axis=0)


def _chunk_cumsum(l2, bl_ref):
    rin = lax.broadcasted_iota(jnp.int32, (CHUNK, LANES), 0) % SUBLANES
    bl = l2
    for s in (1, 2, 4):
        bl = bl + jnp.where(rin >= s, pltpu.roll(bl, s, axis=0), 0.0)
    bl_ref[...] = bl
    return bl


def _chunk_table(bl_ref, tab_ref):
    tot = bl_ref[pl.ds(SUBLANES - 1, NBLK, stride=SUBLANES), :]
    brow = lax.broadcasted_iota(jnp.int32, (NBLK, LANES), 0)
    rb = tot
    for s in (1, 2, 4):
        rb = rb + jnp.where(brow >= s, pltpu.roll(rb, s, axis=0), 0.0)
    rb_prev = rb - tot
    total = rb[NBLK - 1:NBLK, :]

    tab_ref[pl.ds(ROW_TOT, NBLK), :] = tot
    tab_ref[pl.ds(ROW_QE, NBLK), :] = jnp.exp2(rb_prev)
    tab_ref[pl.ds(ROW_KD, NBLK), :] = jnp.exp2(total - rb)
    for lv, cb in enumerate(LEVELS):
        edges = [(j // (2 * cb)) * (2 * cb) + cb - 1 for j in range(NBLK)]
        rb_edge = jnp.concatenate([rb[e:e + 1, :] for e in edges], axis=0)
        base = ROW_LEVEL + 2 * NBLK * lv
        tab_ref[pl.ds(base, NBLK), :] = jnp.exp2(jnp.minimum(rb_prev - rb_edge, 0.0))
        tab_ref[pl.ds(base + NBLK, NBLK), :] = jnp.exp2(jnp.minimum(rb_edge - rb, 0.0))
    return total


def _chunk_scaled(q, k, bl, tab_ref):
    return q * jnp.exp2(bl), k * jnp.exp2(_per_block(tab_ref, ROW_TOT) - bl)


def _chunk_inter_operands(qp, kp, tab_ref):
    return (qp * _per_block(tab_ref, ROW_QE)).astype(BF), (kp * _per_block(tab_ref, ROW_KD)).astype(BF)


def _chunk_pair_operands(qp, kp, tab_ref):
    q_tiles, k_tiles = [], []
    for lv, cb in enumerate(LEVELS):
        base = ROW_LEVEL + 2 * NBLK * lv
        for p0 in range(0, NBLK, 2 * cb):
            q_tiles.append(_sparse_tile(qp, tab_ref, base, range(p0 + cb, p0 + 2 * cb)))
            k_tiles.append(_sparse_tile(kp, tab_ref, base + NBLK, range(p0, p0 + cb)))
    return jnp.concatenate(q_tiles, axis=1), jnp.concatenate(k_tiles, axis=1)


def _chunk_block_operands(q, k, bl, row_masks, bl_ref):
    k_b = k.astype(BF)
    lhs = [(q * jnp.exp2(jnp.minimum(bl - _block_rows(bl_ref, c), 0.0))).astype(BF) for c in range(SUBLANES)]
    rhs = [k_b * row_masks[c] for c in range(SUBLANES)]
    return jnp.concatenate(lhs, axis=1), jnp.concatenate(rhs, axis=1)


def _chunk_products(qe, kdec, v_b, pair_ops, block_ops, st):
    return (_dot_nt(qe, st.astype(BF)), _dot_tn(v_b, kdec), _dot_nt(*pair_ops), _dot_nt(*block_ops))


def _chunk_output(o_inter, off_diag, diag, v_b):
    ti = lax.broadcasted_iota(jnp.int32, (CHUNK, CHUNK), 0)
    si = lax.broadcasted_iota(jnp.int32, (CHUNK, CHUNK), 1)
    scores = jnp.where((ti // SUBLANES == si // SUBLANES) & (si <= ti), diag, off_diag)
    return o_inter + _dot(scores.astype(BF), v_b)


def _mixer_tail(x, m, w_out, g_ffn, w_router_t):
    x1 = x + _dot(m.astype(BF), w_out)
    h2 = _rms(x1, g_ffn).astype(BF)
    route = _route(_dot_nt(w_router_t, h2))
    return x1, h2.astype(F32), route


def _prompt_mixer_kernel(x_ref, gmix_ref, win_ref, bg_ref, cw_ref, cb_ref, lng_ref, lnb_ref, wco_ref,
                         lbp_ref, hng_ref, who_ref, wout_ref, gffn_ref, wr_ref, masks_ref,
                         x1_ref, h2_ref, route_ref, nconv_ref, nhgrn_ref,
                         hist_ref, phase_ref, conv_ref, st_ref, q_s, k_s, v_s, lf_s, o_s, og_s, ga_s, gb_s, *head_scr, tb):
    t = pl.program_id(1)

    @pl.when(t == 0)
    def _():
        hist_ref[pl.ds(0, HIST_ROWS), :] = jnp.zeros((HIST_ROWS, CONV_DIM), F32)
        st_ref[...] = jnp.zeros_like(st_ref)

    x = x_ref[...]
    h = _rms(x, gmix_ref[...]).astype(BF)

    cv = _dot(h, win_ref[:, OFF_CV:OFF_CV + CONV_DIM])
    cg = _dot(h, win_ref[:, OFF_CG:OFF_CG + CONV_DIM])
    hist_ref[pl.ds(HIST_ROWS, tb), :] = cv * _sigmoid(cg)

    lb = _lower_bound(lbp_ref[...])

    def project(seg, c0):
        z = _dot(h, win_ref[:, seg + c0:seg + c0 + PROJ_COLS])
        cols = slice(c0, c0 + PROJ_COLS)
        if seg == OFF_Q:
            q_s[:, cols] = _silu(z) * (HG_DK ** -0.5)
        elif seg == OFF_F:
            fg = lb[:, cols] + (1.0 - lb[:, cols]) * _sigmoid(z)
            lf_s[:, cols] = jnp.log2(fg)
            k_s[:, cols] = 1.0 - fg
        elif seg == OFF_V:
            v_s[:, cols] = z
        elif seg == OFF_OG:
            og_s[:, cols] = _silu(z)
        elif seg == OFF_GA:
            ga_s[:, cols] = _sigmoid(z + bg_ref[:, c0:c0 + PROJ_COLS])
        else:
            gb_s[:, cols] = _sigmoid(z + bg_ref[:, D_MODEL + c0:D_MODEL + c0 + PROJ_COLS])

    proj_jobs = [functools.partial(project, seg, c0)
                 for seg in (OFF_Q, OFF_F, OFF_V, OFF_OG, OFF_GA, OFF_GB) for c0 in range(0, HG_DIM, PROJ_COLS)]

    span = tb + SUBLANES * (-(-CONV_WIDTH // SUBLANES) - 1)

    def phase_copy(r):
        n = min(span, HIST_ROWS + tb - HIST_PAD - r)
        phase_ref[r, pl.ds(0, n), :] = hist_ref[pl.ds(HIST_PAD + r, n), :]

    def conv_rows(r0):
        acc = jnp.zeros((CONV_ROWS, CONV_DIM), F32)
        for j in range(CONV_WIDTH):
            w_j = jnp.broadcast_to(cw_ref[pl.ds(j, 1), :], (SUBLANES, CONV_DIM))
            acc = acc + (phase_ref[j % SUBLANES, pl.ds(r0 + j - j % SUBLANES, CONV_ROWS), :]
                         * jnp.concatenate([w_j] * (CONV_ROWS // SUBLANES), axis=0))
        conv_ref[pl.ds(r0, CONV_ROWS), :] = acc

    conv_jobs = ([functools.partial(phase_copy, r) for r in range(SUBLANES)]
                 + [functools.partial(conv_rows, r0) for r0 in range(0, tb, CONV_ROWS)])
    per_conv_job = -(-len(proj_jobs) // len(conv_jobs))
    for i, job in enumerate(conv_jobs):
        job()
        for pj in proj_jobs[i * per_conv_job:(i + 1) * per_conv_job]:
            pj()
    for pj in proj_jobs[len(conv_jobs) * per_conv_job:]:
        pj()

    tail = hist_ref[pl.ds(tb, HIST_ROWS), :]
    hist_ref[pl.ds(0, HIST_ROWS), :] = tail
    nconv_ref[0] = tail
    c = _conv_post(conv_ref[...], cb_ref[...], lng_ref[...], lnb_ref[...])
    y_a = _dot(c.astype(BF), wco_ref[...])

    row_masks = masks_ref[...]
    cols = [slice(hd * HG_DK, (hd + 1) * HG_DK) for hd in range(HG_HEADS)]
    bl_refs, tab_refs = head_scr[:HG_HEADS], head_scr[HG_HEADS:]
    heads = range(HG_HEADS)
    for r0 in range(0, tb, CHUNK):
        rows = pl.ds(r0, CHUNK)
        q = [q_s[rows, cols[hd]] for hd in heads]
        k = [k_s[rows, cols[hd]] for hd in heads]
        bl = [_chunk_cumsum(lf_s[rows, cols[hd]], bl_refs[hd]) for hd in heads]
        total = [_chunk_table(bl_refs[hd], tab_refs[hd]) for hd in heads]
        scaled = [_chunk_scaled(q[hd], k[hd], bl[hd], tab_refs[hd]) for hd in heads]
        inter = [_chunk_inter_operands(*scaled[hd], tab_refs[hd]) for hd in heads]
        pair_ops = [_chunk_pair_operands(*scaled[hd], tab_refs[hd]) for hd in heads]
        block_ops = [_chunk_block_operands(q[hd], k[hd], bl[hd], row_masks, bl_refs[hd]) for hd in heads]
        v_b = [v_s[rows, cols[hd]].astype(BF) for hd in heads]
        prods = [_chunk_products(*inter[hd], v_b[hd], pair_ops[hd], block_ops[hd], st_ref[hd]) for hd in heads]
        for hd in heads:
            o_inter, update, off_diag, diag = prods[hd]
            st_ref[hd] = st_ref[hd] * jnp.exp2(total[hd]) + update
            o = _chunk_output(o_inter, off_diag, diag, v_b[hd])
            o_s[rows, cols[hd]] = _head_norm(o, hng_ref[:, cols[hd]])
    for hd in range(HG_HEADS):
        nhgrn_ref[0, hd] = st_ref[hd].T

    y_b = _dot((o_s[...] * og_s[...]).astype(BF), who_ref[...])
    x1, h2, route = _mixer_tail(x, ga_s[...] * y_a + gb_s[...] * y_b, wout_ref[...], gffn_ref[...], wr_ref[...])
    x1_ref[...] = x1
    _store_row_tiles(h2_ref, h2)
    route_ref[...] = route


def _const_spec(shape):
    nd = len(shape)
    return pl.BlockSpec(shape, lambda *_: (0,) * nd, pipeline_mode=pl.Buffered(1))


def _row_masks():
    c = jnp.arange(SUBLANES)[:, None, None]
    r = jnp.arange(CHUNK)[None, :, None] % SUBLANES
    return jnp.broadcast_to(r == c, (SUBLANES, CHUNK, LANES)).astype(BF)


def _prompt_mixer(x2, batch, seq, tb, wts):
    nt = seq // tb
    n = batch * seq
    row_spec = lambda w: pl.BlockSpec((tb, w), lambda b, t: (b * nt + t, 0))
    consts = [wts['gmix'], wts['w_in'], wts['b_gates'], wts['conv_w'], wts['conv_b'], wts['ln_g'], wts['ln_b'],
              wts['w_conv_out'], wts['lb_param'], wts['hn_g'], wts['w_hgrn_out'], wts['w_out'], wts['gffn'],
              wts['w_router'], _row_masks()]
    span = tb + SUBLANES * (-(-CONV_WIDTH // SUBLANES) - 1)
    return pl.pallas_call(
        functools.partial(_prompt_mixer_kernel, tb=tb),
        grid=(batch, nt),
        in_specs=[row_spec(D_MODEL)] + [_const_spec(c.shape) for c in consts],
        out_specs=[row_spec(D_MODEL), pl.BlockSpec((tb * SUBLANES, LANES), lambda b, t: (b * nt + t, 0)),
                   pl.BlockSpec((SUBLANES, tb), lambda b, t: (0, b * nt + t)),
                   pl.BlockSpec((1, HIST_ROWS, CONV_DIM), lambda b, t: (b, 0, 0)),
                   pl.BlockSpec((1, HG_HEADS, HG_DK, HG_DV), lambda b, t: (b, 0, 0, 0))],
        out_shape=[jax.ShapeDtypeStruct((n, D_MODEL), F32), jax.ShapeDtypeStruct((n * SUBLANES, LANES), F32),
                   jax.ShapeDtypeStruct((SUBLANES, n), F32),
                   jax.ShapeDtypeStruct((batch, HIST_ROWS, CONV_DIM), F32),
                   jax.ShapeDtypeStruct((batch, HG_HEADS, HG_DK, HG_DV), F32)],
        scratch_shapes=[pltpu.VMEM((HIST_ROWS + tb, CONV_DIM), F32),
                        pltpu.VMEM((SUBLANES, span, CONV_DIM), F32),
                        pltpu.VMEM((tb, CONV_DIM), F32),
                        pltpu.VMEM((HG_HEADS, HG_DV, HG_DK), F32)]
                       + [pltpu.VMEM((tb, HG_DIM), F32)] * 8
                       + [pltpu.VMEM((CHUNK, LANES), F32)] * HG_HEADS
                       + [pltpu.VMEM((TABLE_ROWS, LANES), F32)] * HG_HEADS,
        compiler_params=pltpu.CompilerParams(dimension_semantics=("arbitrary", "arbitrary"),
                                             vmem_limit_bytes=VMEM_LIMIT),
        name="prompt_mixer",
    )(x2, *consts)


def _sample_proj_kernel(x_ref, gmix_ref, win_ref, proj_ref):
    h = _rms(x_ref[...], gmix_ref[...]).astype(BF)
    proj_ref[...] = _dot(h, win_ref[...])


def _sample_proj(xs, wts):
    n = xs.shape[0]
    return pl.pallas_call(
        _sample_proj_kernel,
        grid=(1,),
        in_specs=[_const_spec(xs.shape), _const_spec(wts['gmix'].shape), _const_spec(wts['w_in'].shape)],
        out_specs=pl.BlockSpec((n, IN_COLS), lambda i: (0, 0)),
        out_shape=jax.ShapeDtypeStruct((n, IN_COLS), F32),
        compiler_params=pltpu.CompilerParams(vmem_limit_bytes=VMEM_LIMIT),
        name="sample_proj",
    )(xs, wts['gmix'], wts['w_in'])


def _split3(x):
    hi = x.astype(BF).astype(F32)
    mid = (x - hi).astype(BF).astype(F32)
    lo = ((x - hi) - mid).astype(BF).astype(F32)
    return hi, mid, lo


def _sample_state_kernel(q_ref, f_ref, v_ref, lbp_ref, hng_ref, lane_masks_ref, s_ref, snew_ref, o_ref, *, bs):
    assert bs == SUBLANES
    lb = _lower_bound(lbp_ref[...])
    qf = _silu(q_ref[...]) * (HG_DK ** -0.5)
    fg = lb + (1.0 - lb) * _sigmoid(f_ref[...])
    kf = 1.0 - fg
    v = v_ref[...]
    ones = jnp.ones((bs, HG_DV), F32)
    zeros = jnp.zeros((bs, HG_DV), F32)
    n_groups = 9
    pad = jnp.zeros((LANES - n_groups * bs, LANES), F32)
    for hd in range(HG_HEADS):
        cs = slice(hd * HG_DK, (hd + 1) * HG_DK)
        f3, k3, q3, v3 = _split3(fg[:, cs]), _split3(kf[:, cs]), _split3(qf[:, cs]), _split3(v[:, cs])
        left = jnp.concatenate([f3[0], f3[1], f3[2], k3[0], k3[1], k3[0], q3[0], q3[1], q3[2], pad], axis=0)
        left_t = left.T.astype(BF)
        right = jnp.concatenate(
            [jnp.concatenate(blk, axis=1) for blk in
             [(ones, zeros, zeros)] * 3 + [(zeros, v3[0], zeros), (zeros, v3[0], zeros), (zeros, v3[1], zeros)]
             + [(zeros, zeros, ones)] * 3] + [jnp.concatenate((pad, pad, pad), axis=1)], axis=0).astype(BF)
        o_rows = []
        for r in range(bs):
            prod = _dot(left_t * lane_masks_ref[r], right)
            s_new = prod[:, :HG_DV] * s_ref[r, hd] + prod[:, HG_DV:2 * HG_DV]
            snew_ref[r, hd] = s_new
            o_rows.append(jnp.sum(prod[:, 2 * HG_DV:] * s_new, axis=0, keepdims=True))
        o_ref[:, cs] = _head_norm(jnp.concatenate(o_rows, axis=0), hng_ref[:, cs])


def _lane_masks():
    r = jnp.arange(SUBLANES)[:, None, None]
    lane = jnp.arange(LANES)[None, None, :] % SUBLANES
    return jnp.broadcast_to(lane == r, (SUBLANES, LANES, LANES)).astype(BF)


def _sample_state(proj, state, wts, bs):
    n = proj.shape[0]
    col_spec = lambda off: pl.BlockSpec((bs, HG_DIM), lambda i, off=off: (i, off // HG_DIM))
    st_spec = pl.BlockSpec((bs, HG_HEADS, HG_DK, HG_DV), lambda i: (i, 0, 0, 0))
    return pl.pallas_call(
        functools.partial(_sample_state_kernel, bs=bs),
        grid=(n // bs,),
        in_specs=[col_spec(OFF_Q), col_spec(OFF_F), col_spec(OFF_V),
                  _const_spec(wts['lb_param'].shape), _const_spec(wts['hn_g'].shape),
                  _const_spec((SUBLANES, LANES, LANES)), st_spec],
        out_specs=[st_spec, pl.BlockSpec((bs, HG_DIM), lambda i: (i, 0))],
        out_shape=[jax.ShapeDtypeStruct(state.shape, F32), jax.ShapeDtypeStruct((n, HG_DIM), F32)],
        compiler_params=pltpu.CompilerParams(dimension_semantics=("arbitrary",), vmem_limit_bytes=VMEM_LIMIT),
        name="sample_state",
    )(proj, proj, proj, wts['lb_param'], wts['hn_g'], _lane_masks(), state)


def _sample_tail_kernel(x_ref, proj_ref, o_ref, cst_ref, bg_ref, cw_ref, cb_ref, lng_ref, lnb_ref, wco_ref,
                        who_ref, wout_ref, gffn_ref, wr_ref, x1_ref, h2_ref, route_ref, nconv_ref):
    keep = (CONV_WIDTH - 2) * CONV_DIM
    u = proj_ref[:, OFF_CV:OFF_CV + CONV_DIM] * _sigmoid(proj_ref[:, OFF_CG:OFF_CG + CONV_DIM])
    acc = u * cw_ref[pl.ds(CONV_WIDTH - 1, 1), :]
    for j in range(CONV_WIDTH - 1):
        acc = acc + cst_ref[:, j * CONV_DIM:(j + 1) * CONV_DIM] * cw_ref[pl.ds(j, 1), :]
    nconv_ref[:, :keep] = cst_ref[:, CONV_DIM:]
    nconv_ref[:, keep:] = u
    c = _conv_post(acc, cb_ref[...], lng_ref[...], lnb_ref[...])
    y_a = _dot(c.astype(BF), wco_ref[...])
    og = proj_ref[:, OFF_OG:OFF_OG + HG_DIM]
    y_b = _dot((o_ref[...] * _silu(og)).astype(BF), who_ref[...])
    m = (_sigmoid(proj_ref[:, OFF_GA:OFF_GA + D_MODEL] + bg_ref[:, :D_MODEL]) * y_a
         + _sigmoid(proj_ref[:, OFF_GB:OFF_GB + D_MODEL] + bg_ref[:, D_MODEL:]) * y_b)
    x1, h2, route = _mixer_tail(x_ref[...], m, wout_ref[...], gffn_ref[...], wr_ref[...])
    x1_ref[...] = x1
    _store_row_tiles(h2_ref, h2)
    route_ref[...] = route


def _sample_tail(xs, proj, o, conv_state2, wts):
    n = xs.shape[0]
    ins = [xs, proj, o, conv_state2, wts['b_gates'], wts['conv_w'], wts['conv_b'], wts['ln_g'], wts['ln_b'],
           wts['w_conv_out'], wts['w_hgrn_out'], wts['w_out'], wts['gffn'], wts['w_router']]
    full = lambda shape: pl.BlockSpec(shape, lambda i: (0,) * len(shape))
    return pl.pallas_call(
        _sample_tail_kernel,
        grid=(1,),
        in_specs=[_const_spec(a.shape) for a in ins],
        out_specs=[full((n, D_MODEL)), full((n * SUBLANES, LANES)), full((SUBLANES, n)), full(conv_state2.shape)],
        out_shape=[jax.ShapeDtypeStruct((n, D_MODEL), F32), jax.ShapeDtypeStruct((n * SUBLANES, LANES), F32),
                   jax.ShapeDtypeStruct((SUBLANES, n), F32), jax.ShapeDtypeStruct(conv_state2.shape, F32)],
        compiler_params=pltpu.CompilerParams(vmem_limit_bytes=VMEM_LIMIT),
        name="sample_tail",
    )(*ins)


def _plan_kernel(route_p_ref, route_s_ref, pos_ref, tinfo_ref, e_ref, pre_ref, *, tm, n_tok, max_tiles):
    n_p = route_p_ref.shape[1]
    nblk = n_tok // LANES
    e_ref[:, pl.ds(0, n_p)] = route_p_ref[...]
    e_ref[:, pl.ds(n_p, n_tok - n_p)] = route_s_ref[...]
    eidx = lax.broadcasted_iota(jnp.int32, (N_EXPERTS, LANES), 0).astype(F32)
    ti = lax.broadcasted_iota(jnp.int32, (LANES, LANES), 0)
    si = lax.broadcasted_iota(jnp.int32, (LANES, LANES), 1)
    before = (ti < si).astype(BF)

    def onehots(b):
        c0 = pl.multiple_of(b * LANES, LANES)
        e1 = e_ref[0:1, pl.ds(c0, LANES)]
        e2 = e_ref[1:2, pl.ds(c0, LANES)]
        return (e1 == eidx).astype(F32), (e2 == eidx).astype(F32), c0

    def count(b, carry):
        h1, h2, c0 = onehots(b)
        h = h1 + h2
        pre_ref[:, pl.ds(c0, LANES)] = _dot(h.astype(BF), before) + carry
        return carry + jnp.sum(h, axis=1, keepdims=True)

    counts = lax.fori_loop(0, nblk, count, jnp.zeros((N_EXPERTS, 1), F32))
    tiles_per = jnp.floor((counts + (tm - 1)) * (1.0 / tm))
    ei = lax.broadcasted_iota(jnp.int32, (N_EXPERTS, N_EXPERTS), 0)
    ej = lax.broadcasted_iota(jnp.int32, (N_EXPERTS, N_EXPERTS), 1)
    upto = (ej <= ei).astype(BF)
    tile_end = _dot(upto, jnp.broadcast_to(tiles_per, (N_EXPERTS, LANES)).astype(BF))[:, 0:1]
    starts = (tile_end - tiles_per) * tm

    def place(b, c):
        h1, h2, c0 = onehots(b)
        dest = pre_ref[:, pl.ds(c0, LANES)] + starts
        pos_ref[0:1, pl.ds(c0, LANES)] = jnp.sum(h1 * dest, axis=0, keepdims=True).astype(jnp.int32)
        pos_ref[1:2, pl.ds(c0, LANES)] = jnp.sum(h2 * dest, axis=0, keepdims=True).astype(jnp.int32)
        return c

    pos_ref[...] = jnp.zeros_like(pos_ref)
    lax.fori_loop(0, nblk, place, 0)

    tile = lax.broadcasted_iota(jnp.int32, (N_EXPERTS, max_tiles), 1).astype(F32)
    t_exp = jnp.sum((tile_end <= tile).astype(F32), axis=0, keepdims=True)
    t_exp = jnp.minimum(t_exp, N_EXPERTS - 1.0).astype(jnp.int32)
    n_used = jnp.broadcast_to(tile_end[N_EXPERTS - 1:N_EXPERTS, :], (1, max_tiles)).astype(jnp.int32)
    row = lax.broadcasted_iota(jnp.int32, (SUBLANES, max_tiles), 0)
    tinfo_ref[...] = jnp.where(row == 0, t_exp, jnp.where(row == 1, n_used, 0))


def _plan(route_p, route_s, tm, max_tiles):
    n_tok = route_p.shape[1] + route_s.shape[1]
    assert route_p.shape[1] % LANES == 0 and route_s.shape[1] % LANES == 0
    vm = pl.BlockSpec(memory_space=pltpu.VMEM)
    return pl.pallas_call(
        functools.partial(_plan_kernel, tm=tm, n_tok=n_tok, max_tiles=max_tiles),
        in_specs=[vm, vm],
        out_specs=[vm, vm],
        out_shape=[jax.ShapeDtypeStruct((SUBLANES, n_tok), jnp.int32),
                   jax.ShapeDtypeStruct((SUBLANES, max_tiles), jnp.int32)],
        scratch_shapes=[pltpu.VMEM((SUBLANES, n_tok), F32), pltpu.VMEM((N_EXPERTS, n_tok), F32)],
        compiler_params=pltpu.CompilerParams(vmem_limit_bytes=VMEM_LIMIT),
        name="route_plan",
    )(route_p, route_s)


def _dispatch_kernel(pos_ref, h_ref, xs_in_ref, xs_ref, sem, *, tb, tok0):
    del xs_in_ref
    i = pl.program_id(0)

    def copies(r):
        base = 2 * (tok0 + i * tb + r)
        tile = lambda ref, row: ref.at[pl.ds(pl.multiple_of(row * SUBLANES, SUBLANES), SUBLANES), :]
        return (pltpu.make_async_copy(tile(h_ref, r), tile(xs_ref, pos_ref[base]), sem),
                pltpu.make_async_copy(tile(h_ref, r), tile(xs_ref, pos_ref[base + 1]), sem))

    def issue(r, c):
        c0, c1 = copies(r)
        c0.start(priority=0)
        c1.start(priority=1)
        return c

    def drain(r, c):
        c0, c1 = copies(r)
        c0.wait()
        c1.wait()
        return c

    lax.fori_loop(0, tb, issue, 0, unroll=8)
    lax.fori_loop(0, tb, drain, 0, unroll=8)


def _dispatch(pos_flat, hpk, xs, tb, tok0):
    n = hpk.shape[0] // SUBLANES
    grid_spec = pltpu.PrefetchScalarGridSpec(
        num_scalar_prefetch=1,
        grid=(n // tb,),
        in_specs=[pl.BlockSpec((tb * SUBLANES, LANES), lambda i, p: (i, 0)),
                  pl.BlockSpec(memory_space=pl.ANY)],
        out_specs=pl.BlockSpec(memory_space=pl.ANY),
        scratch_shapes=[pltpu.SemaphoreType.DMA(())])
    return pl.pallas_call(
        functools.partial(_dispatch_kernel, tb=tb, tok0=tok0),
        grid_spec=grid_spec,
        out_shape=jax.ShapeDtypeStruct(xs.shape, xs.dtype),
        input_output_aliases={2: 0},
        compiler_params=pltpu.CompilerParams(dimension_semantics=("arbitrary",), vmem_limit_bytes=VMEM_LIMIT),
        name="dispatch",
    )(pos_flat, hpk, xs)


def _store_row_tiles(ref, rows):
    n = rows.shape[0]
    for s in range(SUBLANES):
        ref[pl.ds(s, n, stride=SUBLANES), :] = rows[:, s * LANES:(s + 1) * LANES]


def _load_row_tiles(ref):
    n = ref.shape[0] // SUBLANES
    return jnp.concatenate([ref[pl.ds(s, n, stride=SUBLANES), :] for s in range(SUBLANES)], axis=1)


def _expert_kernel(te_ref, nt_ref, xs_ref, wg_ref, wu_ref, wd_ref, y_ref, wg_b, wu_b, wd_b):
    i = pl.program_id(0)

    @pl.when(i < nt_ref[0])
    def _():
        changed = jnp.logical_or(i == 0, te_ref[i] != te_ref[jnp.maximum(i - 1, 0)])

        @pl.when(changed)
        def _():
            wg_b[...] = wg_ref[0].astype(BF)
            wu_b[...] = wu_ref[0].astype(BF)
            wd_b[...] = wd_ref[0].astype(BF)

        xb = _load_row_tiles(xs_ref).astype(BF)
        gate = _dot(xb, wg_b[...])
        up = _dot(xb, wu_b[...])
        _store_row_tiles(y_ref, _dot((_silu(gate) * up).astype(BF), wd_b[...]))

    @pl.when(i >= nt_ref[0])
    def _():
        y_ref[...] = jnp.zeros_like(y_ref)


def _experts(xs, tile_expert, n_tiles, wg, wu, wd, tm, max_tiles):
    grid_spec = pltpu.PrefetchScalarGridSpec(
        num_scalar_prefetch=2,
        grid=(max_tiles,),
        in_specs=[pl.BlockSpec((tm * SUBLANES, LANES),
                               lambda i, te, nt: (jnp.minimum(i, nt[0] - 1), 0)),
                  pl.BlockSpec((1, D_MODEL, EXPERT_FF), lambda i, te, nt: (te[i], 0, 0)),
                  pl.BlockSpec((1, D_MODEL, EXPERT_FF), lambda i, te, nt: (te[i], 0, 0)),
                  pl.BlockSpec((1, EXPERT_FF, D_MODEL), lambda i, te, nt: (te[i], 0, 0))],
        out_specs=pl.BlockSpec((tm * SUBLANES, LANES), lambda i, te, nt: (i, 0)),
        scratch_shapes=[pltpu.VMEM((D_MODEL, EXPERT_FF), BF), pltpu.VMEM((D_MODEL, EXPERT_FF), BF),
                        pltpu.VMEM((EXPERT_FF, D_MODEL), BF)])
    return pl.pallas_call(
        _expert_kernel,
        grid_spec=grid_spec,
        out_shape=jax.ShapeDtypeStruct((max_tiles * tm * SUBLANES, LANES), F32),
        compiler_params=pltpu.CompilerParams(dimension_semantics=("arbitrary",), vmem_limit_bytes=VMEM_LIMIT),
        name="experts",
    )(tile_expert, n_tiles, xs, wg, wu, wd)


def _combine_kernel(pos_ref, x1_ref, route_ref, gfin_ref, ys_ref, y_ref, buf0, buf1, sems, *, tb, tok0):
    i = pl.program_id(0)
    n_steps = pl.num_programs(0)
    slot = i % 2

    def copies(step, half, r):
        base = 2 * (tok0 + step * tb + r)
        tile = lambda ref, row: ref.at[pl.ds(pl.multiple_of(row * SUBLANES, SUBLANES), SUBLANES), :]
        return (pltpu.make_async_copy(tile(ys_ref, pos_ref[base]), tile(buf0.at[half], r), sems.at[half]),
                pltpu.make_async_copy(tile(ys_ref, pos_ref[base + 1]), tile(buf1.at[half], r), sems.at[half]))

    def request(step, half):
        def body(r, c):
            c0, c1 = copies(step, half, r)
            c0.start(priority=0)
            c1.start(priority=1)
            return c

        lax.fori_loop(0, tb, body, 0, unroll=8)

    @pl.when(i == 0)
    def _():
        request(0, 0)

    @pl.when(i + 1 < n_steps)
    def _():
        request(i + 1, 1 - slot)

    def receive(r, c):
        c0, c1 = copies(i, slot, r)
        c0.wait()
        c1.wait()
        return c

    lax.fori_loop(0, tb, receive, 0, unroll=8)
    pad = jnp.zeros((LANES - SUBLANES, LANES), F32)
    cols = jnp.concatenate([jnp.concatenate([route_ref[:, j:j + LANES], pad], axis=0).T
                            for j in range(0, tb, LANES)], axis=0)
    out = x1_ref[...] + (cols[:, 2:3] * _load_row_tiles(buf0.at[slot]) + cols[:, 3:4] * _load_row_tiles(buf1.at[slot]))
    y_ref[...] = _rms(out, gfin_ref[...])


def _combine(pos_flat, x1, route, gfin, ys, tb, tok0):
    n = x1.shape[0]
    grid_spec = pltpu.PrefetchScalarGridSpec(
        num_scalar_prefetch=1,
        grid=(n // tb,),
        in_specs=[pl.BlockSpec((tb, D_MODEL), lambda i, p: (i, 0)),
                  pl.BlockSpec((SUBLANES, tb), lambda i, p: (0, i)),
                  pl.BlockSpec((1, D_MODEL), lambda i, p: (0, 0)),
                  pl.BlockSpec(memory_space=pl.ANY)],
        out_specs=pl.BlockSpec((tb, D_MODEL), lambda i, p: (i, 0)),
        scratch_shapes=[pltpu.VMEM((2, tb * SUBLANES, LANES), F32), pltpu.VMEM((2, tb * SUBLANES, LANES), F32),
                        pltpu.SemaphoreType.DMA((2,))])
    return pl.pallas_call(
        functools.partial(_combine_kernel, tb=tb, tok0=tok0),
        grid_spec=grid_spec,
        out_shape=jax.ShapeDtypeStruct((n, D_MODEL), F32),
        compiler_params=pltpu.CompilerParams(dimension_semantics=("arbitrary",), vmem_limit_bytes=VMEM_LIMIT),
        name="combine",
    )(pos_flat, x1, route, gfin, ys)


def kernel(x_prompt, x_sample, state_conv, state_hgrn, norm_mix_g, w_in, b_gates, conv_dw_w, conv_dw_b,
           conv_ln_g, conv_ln_b, w_conv_out, hgrn_lb_param, hgrn_norm_g, w_hgrn_out, w_out, norm_ffn_g,
           w_router_group, w_router_expert, w_expert_gate, w_expert_up, w_expert_down, norm_final_g):
    batch, seq, _ = x_prompt.shape
    dec_batch = x_sample.shape[0]
    assert x_sample.shape[1] == 1 and w_in.shape[0] == 1
    tb = min(256, seq)
    tm = 256
    bs = min(8, dec_batch)
    n_p = batch * seq
    n_all = n_p + dec_batch
    tbd = min(1024, n_p)
    tbc = min(512, n_p)
    assert seq % tb == 0 and tb % CHUNK == 0 and dec_batch % bs == 0 and n_p % tbd == 0 and n_p % tbc == 0

    w_router = jnp.concatenate(
        [w_router_group[0].T, w_router_expert[0].T,
         jnp.zeros((ROUTE_ROWS - N_GROUPS - N_EXPERTS, D_MODEL), F32)], axis=0).astype(BF)
    wts = dict(gmix=norm_mix_g, w_in=w_in[0].astype(BF), b_gates=b_gates, conv_w=conv_dw_w[0], conv_b=conv_dw_b,
               ln_g=conv_ln_g, ln_b=conv_ln_b, w_conv_out=w_conv_out[0].astype(BF), lb_param=hgrn_lb_param,
               hn_g=hgrn_norm_g, w_hgrn_out=w_hgrn_out[0].astype(BF), w_out=w_out[0].astype(BF),
               gffn=norm_ffn_g, w_router=w_router)

    x1_p, hpk_p, route_p, nconv_p, nhgrn_p = _prompt_mixer(x_prompt.reshape(n_p, D_MODEL), batch, seq, tb, wts)

    xs_tok = x_sample.reshape(dec_batch, D_MODEL)
    proj_s = _sample_proj(xs_tok, wts)
    nhgrn_s, o_s = _sample_state(proj_s, state_hgrn[0], wts, bs)
    conv2 = state_conv[0].reshape(dec_batch, (CONV_WIDTH - 1) * CONV_DIM)
    x1_s, hpk_s, route_s, nconv_s = _sample_tail(xs_tok, proj_s, o_s, conv2, wts)

    max_tiles = -(-((2 * n_all) // tm + N_EXPERTS) // SUBLANES) * SUBLANES
    pos, tinfo = _plan(route_p, route_s, tm, max_tiles)
    pos_flat = pos[0:2].T.reshape(-1)
    tile_expert, n_tiles = tinfo[0], tinfo[1, 0:1]

    xs = jnp.zeros((max_tiles * tm * SUBLANES, LANES), F32)
    xs = _dispatch(pos_flat, hpk_p, xs, tbd, 0)
    xs = _dispatch(pos_flat, hpk_s, xs, dec_batch, n_p)
    ys = _experts(xs, tile_expert, n_tiles, w_expert_gate[0], w_expert_up[0], w_expert_down[0], tm, max_tiles)

    gfin = norm_final_g.reshape(1, D_MODEL)
    y_p = _combine(pos_flat, x1_p, route_p, gfin, ys, tbc, 0)
    y_s = _combine(pos_flat, x1_s, route_s, gfin, ys, dec_batch, n_p)

    return (y_p.reshape(batch, seq, D_MODEL), y_s.reshape(dec_batch, 1, D_MODEL),
            nconv_p[None, :, HIST_PAD:, :], nhgrn_p[None],
            nconv_s.reshape(1, dec_batch, CONV_WIDTH - 1, CONV_DIM), nhgrn_s[None])
```

```python
import functools

import jax
import jax.numpy as jnp
from jax import lax
from jax.experimental import pallas as pl
from jax.experimental.pallas import tpu as pltpu

D_MODEL = 1024
CONV_DIM = D_MODEL // 2
CONV_WIDTH = 31
HG_HEADS = 8
HG_DK = 128
HG_DV = 128
HG_DIM = HG_HEADS * HG_DK
N_GROUPS = 4
EXPERTS_PER_GROUP = 8
N_EXPERTS = N_GROUPS * EXPERTS_PER_GROUP
EXPERT_FF = D_MODEL // 2
EPS = 1e-6
IN_COLS = 2 * CONV_DIM + 4 * HG_DIM + 2 * D_MODEL

OFF_CV, OFF_CG = 0, CONV_DIM
OFF_Q = 2 * CONV_DIM
OFF_F = OFF_Q + HG_DIM
OFF_V = OFF_F + HG_DIM
OFF_OG = OFF_V + HG_DIM
OFF_GA = OFF_OG + HG_DIM
OFF_GB = OFF_GA + D_MODEL

SUBLANES = 8
LANES = 128
HIST_ROWS = 32
HIST_PAD = HIST_ROWS - (CONV_WIDTH - 1)
CHUNK = 64
NBLK = CHUNK // SUBLANES
CONV_ROWS = 32
PROJ_COLS = 256
ROUTE_ROWS = 128
VMEM_LIMIT = 56 * 1024 * 1024

BF = jnp.bfloat16
F32 = jnp.float32


def _dot(a, b):
    return jnp.dot(a, b, preferred_element_type=F32)


def _dot_nt(a, b):
    return lax.dot_general(a, b, (((1,), (1,)), ((), ())), preferred_element_type=F32)


def _dot_tn(a, b, precision=None):
    return lax.dot_general(a, b, (((0,), (0,)), ((), ())), preferred_element_type=F32,
                           precision=precision)


NEG_LOG2E = -1.4426950408889634


def _sigmoid(x):
    return 1.0 / (1.0 + jnp.exp2(x * NEG_LOG2E))


def _silu(x):
    return x * _sigmoid(x)


def _rms(xf, g):
    return xf * lax.rsqrt(jnp.mean(xf * xf, axis=-1, keepdims=True) + EPS) * g


def _lower_bound(lb_param):
    m = jnp.max(lb_param, axis=0, keepdims=True)
    e = jnp.exp(lb_param - m)
    return e[0:1] / jnp.sum(e, axis=0, keepdims=True)


def _conv_post(c, conv_b, ln_g, ln_b):
    c = c + conv_b
    mu = jnp.mean(c, axis=-1, keepdims=True)
    d = c - mu
    var = jnp.mean(d * d, axis=-1, keepdims=True)
    return _silu(d * lax.rsqrt(var + EPS) * ln_g + ln_b)


def _head_norm(o, g):
    return o * lax.rsqrt(jnp.mean(o * o, axis=-1, keepdims=True) + EPS) * g


def _route(logits_t):
    used = -(-(N_GROUPS + N_EXPERTS) // SUBLANES) * SUBLANES
    logits = logits_t[0:used, :]
    row = lax.broadcasted_iota(jnp.int32, logits.shape, 0).astype(F32)
    big = jnp.float32(1 << 20)
    neg = jnp.float32(-jnp.inf)
    gmask = row < N_GROUPS
    lg = jnp.where(gmask, logits, neg)
    gmax = jnp.max(lg, axis=0, keepdims=True)
    gsum = jnp.sum(jnp.where(gmask, jnp.exp(lg - gmax), 0.0), axis=0, keepdims=True)
    gval = 1.0 / gsum
    gidx = jnp.min(jnp.where(lg == gmax, row, big), axis=0, keepdims=True)
    lo = N_GROUPS + EXPERTS_PER_GROUP * gidx
    emask = (row >= lo) & (row < lo + EXPERTS_PER_GROUP)
    el = jnp.where(emask, logits, neg)
    m1 = jnp.max(el, axis=0, keepdims=True)
    i1 = jnp.min(jnp.where(el == m1, row, big), axis=0, keepdims=True)
    el2 = jnp.where(row == i1, neg, el)
    m2 = jnp.max(el2, axis=0, keepdims=True)
    i2 = jnp.min(jnp.where(el2 == m2, row, big), axis=0, keepdims=True)
    r = jnp.exp(m2 - m1)
    w1 = gval / (1.0 + r)
    w2 = gval * r / (1.0 + r)
    out_row = lax.broadcasted_iota(jnp.int32, (SUBLANES, logits.shape[1]), 0)
    return jnp.where(out_row == 0, i1 - N_GROUPS,
                     jnp.where(out_row == 1, i2 - N_GROUPS, jnp.where(out_row == 2, w1, jnp.where(out_row == 3, w2, 0.0))))


LEVELS = (NBLK // 2, NBLK // 4, NBLK // 8)
ROW_TOT, ROW_QE, ROW_KD, ROW_LEVEL = 0, NBLK, 2 * NBLK, 3 * NBLK
TABLE_ROWS = ROW_LEVEL + 2 * NBLK * len(LEVELS)


def _row_bcast(ref, row):
    return jnp.broadcast_to(ref[pl.ds(row, 1), :], (SUBLANES, LANES))


def _per_block(ref, base):
    return jnp.concatenate([_row_bcast(ref, base + j) for j in range(NBLK)], axis=0)


def _block_rows(ref, c):
    return jnp.concatenate([_row_bcast(ref, j * SUBLANES + c) for j in range(NBLK)], axis=0)


def _sparse_tile(x, ref, base, blocks):
    groups = []
    for g in range(0, NBLK, 2):
        if g not in blocks and g + 1 not in blocks:
            groups.append(jnp.zeros((2 * SUBLANES, LANES), BF))
            continue
        halves = [x[j * SUBLANES:(j + 1) * SUBLANES, :] * _row_bcast(ref, base + j) if j in blocks
                  else jnp.zeros((SUBLANES, LANES), F32) for j in (g, g + 1)]
        groups.append(jnp.concatenate(halves, axis=0).astype(BF))
    return jnp.concatenate(groups, axis=0)


def _chunk_cumsum(l2, bl_ref):
    rin = lax.broadcasted_iota(jnp.int32, (CHUNK, LANES), 0) % SUBLANES
    bl = l2
    for s in (1, 2, 4):
        bl = bl + jnp.where(rin >= s, pltpu.roll(bl, s, axis=0), 0.0)
    bl_ref[...] = bl
    return bl


def _chunk_table(bl_ref, tab_ref):
    tot = bl_ref[pl.ds(SUBLANES - 1, NBLK, stride=SUBLANES), :]
    brow = lax.broadcasted_iota(jnp.int32, (NBLK, LANES), 0)
    rb = tot
    for s in (1, 2, 4):
        rb = rb + jnp.where(brow >= s, pltpu.roll(rb, s, axis=0), 0.0)
    rb_prev = rb - tot
    total = rb[NBLK - 1:NBLK, :]

    tab_ref[pl.ds(ROW_TOT, NBLK), :] = tot
    tab_ref[pl.ds(ROW_QE, NBLK), :] = jnp.exp2(rb_prev)
    tab_ref[pl.ds(ROW_KD, NBLK), :] = jnp.exp2(total - rb)
    for lv, cb in enumerate(LEVELS):
        edges = [(j // (2 * cb)) * (2 * cb) + cb - 1 for j in range(NBLK)]
        rb_edge = jnp.concatenate([rb[e:e + 1, :] for e in edges], axis=0)
        base = ROW_LEVEL + 2 * NBLK * lv
        tab_ref[pl.ds(base, NBLK), :] = jnp.exp2(jnp.minimum(rb_prev - rb_edge, 0.0))
        tab_ref[pl.ds(base + NBLK, NBLK), :] = jnp.exp2(jnp.minimum(rb_edge - rb, 0.0))
    return total


def _chunk_scaled(q, k, bl, tab_ref):
    return q * jnp.exp2(bl), k * jnp.exp2(_per_block(tab_ref, ROW_TOT) - bl)


def _chunk_inter_operands(qp, kp, tab_ref):
    return (qp * _per_block(tab_ref, ROW_QE)).astype(BF), (kp * _per_block(tab_ref, ROW_KD)).astype(BF)


def _chunk_pair_operands(qp, kp, tab_ref):
    q_tiles, k_tiles = [], []
    for lv, cb in enumerate(LEVELS):
        base = ROW_LEVEL + 2 * NBLK * lv
        for p0 in range(0, NBLK, 2 * cb):
            q_tiles.append(_sparse_tile(qp, tab_ref, base, range(p0 + cb, p0 + 2 * cb)))
            k_tiles.append(_sparse_tile(kp, tab_ref, base + NBLK, range(p0, p0 + cb)))
    return jnp.concatenate(q_tiles, axis=1), jnp.concatenate(k_tiles, axis=1)


def _chunk_block_operands(q, k, bl, row_masks, bl_ref):
    k_b = k.astype(BF)
    lhs = [(q * jnp.exp2(jnp.minimum(bl - _block_rows(bl_ref, c), 0.0))).astype(BF) for c in range(SUBLANES)]
    rhs = [k_b * row_masks[c] for c in range(SUBLANES)]
    return jnp.concatenate(lhs, axis=1), jnp.concatenate(rhs, axis=1)


def _chunk_products(qe, kdec, v_b, pair_ops, block_ops, st):
    return (_dot_nt(qe, st.astype(BF)), _dot_tn(v_b, kdec), _dot_nt(*pair_ops), _dot_nt(*block_ops))


def _chunk_output(o_inter, off_diag, diag, v_b):
    ti = lax.broadcasted_iota(jnp.int32, (CHUNK, CHUNK), 0)
    si = lax.broadcasted_iota(jnp.int32, (CHUNK, CHUNK), 1)
    scores = jnp.where((ti // SUBLANES == si // SUBLANES) & (si <= ti), diag, off_diag)
    return o_inter + _dot(scores.astype(BF), v_b)


def _mixer_tail(x, m, w_out, g_ffn, w_router_t):
    x1 = x + _dot(m.astype(BF), w_out)
    h2 = _rms(x1, g_ffn).astype(BF)
    route = _route(_dot_nt(w_router_t, h2))
    return x1, h2.astype(F32), route


def _prompt_mixer_kernel(x_ref, gmix_ref, win_ref, bg_ref, cw_ref, cb_ref, lng_ref, lnb_ref, wco_ref,
                         lbp_ref, hng_ref, who_ref, wout_ref, gffn_ref, wr_ref, masks_ref,
                         x1_ref, h2_ref, route_ref, nconv_ref, nhgrn_ref,
                         hist_ref, phase_ref, conv_ref, st_ref, q_s, k_s, v_s, lf_s, o_s, og_s, ga_s, gb_s, *head_scr, tb):
    t = pl.program_id(1)

    @pl.when(t == 0)
    def _():
        hist_ref[pl.ds(0, HIST_ROWS), :] = jnp.zeros((HIST_ROWS, CONV_DIM), F32)
        st_ref[...] = jnp.zeros_like(st_ref)

    x = x_ref[...]
    h = _rms(x, gmix_ref[...]).astype(BF)

    cv = _dot(h, win_ref[:, OFF_CV:OFF_CV + CONV_DIM])
    cg = _dot(h, win_ref[:, OFF_CG:OFF_CG + CONV_DIM])
    hist_ref[pl.ds(HIST_ROWS, tb), :] = cv * _sigmoid(cg)

    lb = _lower_bound(lbp_ref[...])

    def project(seg, c0):
        z = _dot(h, win_ref[:, seg + c0:seg + c0 + PROJ_COLS])
        cols = slice(c0, c0 + PROJ_COLS)
        if seg == OFF_Q:
            q_s[:, cols] = _silu(z) * (HG_DK ** -0.5)
        elif seg == OFF_F:
            fg = lb[:, cols] + (1.0 - lb[:, cols]) * _sigmoid(z)
            lf_s[:, cols] = jnp.log2(fg)
            k_s[:, cols] = 1.0 - fg
        elif seg == OFF_V:
            v_s[:, cols] = z
        elif seg == OFF_OG:
            og_s[:, cols] = _silu(z)
        elif seg == OFF_GA:
            ga_s[:, cols] = _sigmoid(z + bg_ref[:, c0:c0 + PROJ_COLS])
        else:
            gb_s[:, cols] = _sigmoid(z + bg_ref[:, D_MODEL + c0:D_MODEL + c0 + PROJ_COLS])

    proj_jobs = [functools.partial(project, seg, c0)
                 for seg in (OFF_Q, OFF_F, OFF_V, OFF_OG, OFF_GA, OFF_GB) for c0 in range(0, HG_DIM, PROJ_COLS)]

    span = tb + SUBLANES * (-(-CONV_WIDTH // SUBLANES) - 1)

    def phase_copy(r):
        n = min(span, HIST_ROWS + tb - HIST_PAD - r)
        phase_ref[r, pl.ds(0, n), :] = hist_ref[pl.ds(HIST_PAD + r, n), :]

    def conv_rows(r0):
        acc = jnp.zeros((CONV_ROWS, CONV_DIM), F32)
        for j in range(CONV_WIDTH):
            w_j = jnp.broadcast_to(cw_ref[pl.ds(j, 1), :], (SUBLANES, CONV_DIM))
            acc = acc + (phase_ref[j % SUBLANES, pl.ds(r0 + j - j % SUBLANES, CONV_ROWS), :]
                         * jnp.concatenate([w_j] * (CONV_ROWS // SUBLANES), axis=0))
        conv_ref[pl.ds(r0, CONV_ROWS), :] = acc

    conv_jobs = ([functools.partial(phase_copy, r) for r in range(SUBLANES)]
                 + [functools.partial(conv_rows, r0) for r0 in range(0, tb, CONV_ROWS)])
    per_conv_job = -(-len(proj_jobs) // len(conv_jobs))
    for i, job in enumerate(conv_jobs):
        job()
        for pj in proj_jobs[i * per_conv_job:(i + 1) * per_conv_job]:
            pj()
    for pj in proj_jobs[len(conv_jobs) * per_conv_job:]:
        pj()

    tail = hist_ref[pl.ds(tb, HIST_ROWS), :]
    hist_ref[pl.ds(0, HIST_ROWS), :] = tail
    nconv_ref[0] = tail
    c = _conv_post(conv_ref[...], cb_ref[...], lng_ref[...], lnb_ref[...])
    y_a = _dot(c.astype(BF), wco_ref[...])

    row_masks = masks_ref[...]
    cols = [slice(hd * HG_DK, (hd + 1) * HG_DK) for hd in range(HG_HEADS)]
    bl_refs, tab_refs = head_scr[:HG_HEADS], head_scr[HG_HEADS:]
    heads = range(HG_HEADS)
    for r0 in range(0, tb, CHUNK):
        rows = pl.ds(r0, CHUNK)
        q = [q_s[rows, cols[hd]] for hd in heads]
        k = [k_s[rows, cols[hd]] for hd in heads]
        bl = [_chunk_cumsum(lf_s[rows, cols[hd]], bl_refs[hd]) for hd in heads]
        total = [_chunk_table(bl_refs[hd], tab_refs[hd]) for hd in heads]
        scaled = [_chunk_scaled(q[hd], k[hd], bl[hd], tab_refs[hd]) for hd in heads]
        inter = [_chunk_inter_operands(*scaled[hd], tab_refs[hd]) for hd in heads]
        pair_ops = [_chunk_pair_operands(*scaled[hd], tab_refs[hd]) for hd in heads]
        block_ops = [_chunk_block_operands(q[hd], k[hd], bl[hd], row_masks, bl_refs[hd]) for hd in heads]
        v_b = [v_s[rows, cols[hd]].astype(BF) for hd in heads]
        prods = [_chunk_products(*inter[hd], v_b[hd], pair_ops[hd], block_ops[hd], st_ref[hd]) for hd in heads]
        for hd in heads:
            o_inter, update, off_diag, diag = prods[hd]
            st_ref[hd] = st_ref[hd] * jnp.exp2(total[hd]) + update
            o = _chunk_output(o_inter, off_diag, diag, v_b[hd])
            o_s[rows, cols[hd]] = _head_norm(o, hng_ref[:, cols[hd]])
    for hd in range(HG_HEADS):
        nhgrn_ref[0, hd] = st_ref[hd].T

    y_b = _dot((o_s[...] * og_s[...]).astype(BF), who_ref[...])
    x1, h2, route = _mixer_tail(x, ga_s[...] * y_a + gb_s[...] * y_b, wout_ref[...], gffn_ref[...], wr_ref[...])
    x1_ref[...] = x1
    _store_row_tiles(h2_ref, h2)
    route_ref[...] = route


def _const_spec(shape):
    nd = len(shape)
    return pl.BlockSpec(shape, lambda *_: (0,) * nd, pipeline_mode=pl.Buffered(1))


def _row_masks():
    c = jnp.arange(SUBLANES)[:, None, None]
    r = jnp.arange(CHUNK)[None, :, None] % SUBLANES
    return jnp.broadcast_to(r == c, (SUBLANES, CHUNK, LANES)).astype(BF)


def _prompt_mixer(x2, batch, seq, tb, wts):
    nt = seq // tb
    n = batch * seq
    row_spec = lambda w: pl.BlockSpec((tb, w), lambda b, t: (b * nt + t, 0))
    consts = [wts['gmix'], wts['w_in'], wts['b_gates'], wts['conv_w'], wts['conv_b'], wts['ln_g'], wts['ln_b'],
              wts['w_conv_out'], wts['lb_param'], wts['hn_g'], wts['w_hgrn_out'], wts['w_out'], wts['gffn'],
              wts['w_router'], _row_masks()]
    span = tb + SUBLANES * (-(-CONV_WIDTH // SUBLANES) - 1)
    return pl.pallas_call(
        functools.partial(_prompt_mixer_kernel, tb=tb),
        grid=(batch, nt),
        in_specs=[row_spec(D_MODEL)] + [_const_spec(c.shape) for c in consts],
        out_specs=[row_spec(D_MODEL), pl.BlockSpec((tb * SUBLANES, LANES), lambda b, t: (b * nt + t, 0)),
                   pl.BlockSpec((SUBLANES, tb), lambda b, t: (0, b * nt + t)),
                   pl.BlockSpec((1, HIST_ROWS, CONV_DIM), lambda b, t: (b, 0, 0)),
                   pl.BlockSpec((1, HG_HEADS, HG_DK, HG_DV), lambda b, t: (b, 0, 0, 0))],
        out_shape=[jax.ShapeDtypeStruct((n, D_MODEL), F32), jax.ShapeDtypeStruct((n * SUBLANES, LANES), F32),
                   jax.ShapeDtypeStruct((SUBLANES, n), F32),
                   jax.ShapeDtypeStruct((batch, HIST_ROWS, CONV_DIM), F32),
                   jax.ShapeDtypeStruct((batch, HG_HEADS, HG_DK, HG_DV), F32)],
        scratch_shapes=[pltpu.VMEM((HIST_ROWS + tb, CONV_DIM), F32),
                        pltpu.VMEM((SUBLANES, span, CONV_DIM), F32),
                        pltpu.VMEM((tb, CONV_DIM), F32),
                        pltpu.VMEM((HG_HEADS, HG_DV, HG_DK), F32)]
                       + [pltpu.VMEM((tb, HG_DIM), F32)] * 8
                       + [pltpu.VMEM((CHUNK, LANES), F32)] * HG_HEADS
                       + [pltpu.VMEM((TABLE_ROWS, LANES), F32)] * HG_HEADS,
        compiler_params=pltpu.CompilerParams(dimension_semantics=("arbitrary", "arbitrary"),
                                             vmem_limit_bytes=VMEM_LIMIT),
        name="prompt_mixer",
    )(x2, *consts)


def _sample_proj_kernel(x_ref, gmix_ref, win_ref, proj_ref):
    h = _rms(x_ref[...], gmix_ref[...]).astype(BF)
    proj_ref[...] = _dot(h, win_ref[...])


def _sample_proj(xs, wts):
    n = xs.shape[0]
    return pl.pallas_call(
        _sample_proj_kernel,
        grid=(1,),
        in_specs=[_const_spec(xs.shape), _const_spec(wts['gmix'].shape), _const_spec(wts['w_in'].shape)],
        out_specs=pl.BlockSpec((n, IN_COLS), lambda i: (0, 0)),
        out_shape=jax.ShapeDtypeStruct((n, IN_COLS), F32),
        compiler_params=pltpu.CompilerParams(vmem_limit_bytes=VMEM_LIMIT),
        name="sample_proj",
    )(xs, wts['gmix'], wts['w_in'])


def _split3(x):
    hi = x.astype(BF).astype(F32)
    mid = (x - hi).astype(BF).astype(F32)
    lo = ((x - hi) - mid).astype(BF).astype(F32)
    return hi, mid, lo


def _sample_state_kernel(q_ref, f_ref, v_ref, lbp_ref, hng_ref, lane_masks_ref, s_ref, snew_ref, o_ref, *, bs):
    assert bs == SUBLANES
    lb = _lower_bound(lbp_ref[...])
    qf = _silu(q_ref[...]) * (HG_DK ** -0.5)
    fg = lb + (1.0 - lb) * _sigmoid(f_ref[...])
    kf = 1.0 - fg
    v = v_ref[...]
    ones = jnp.ones((bs, HG_DV), F32)
    zeros = jnp.zeros((bs, HG_DV), F32)
    n_groups = 9
    pad = jnp.zeros((LANES - n_groups * bs, LANES), F32)
    for hd in range(HG_HEADS):
        cs = slice(hd * HG_DK, (hd + 1) * HG_DK)
        f3, k3, q3, v3 = _split3(fg[:, cs]), _split3(kf[:, cs]), _split3(qf[:, cs]), _split3(v[:, cs])
        left = jnp.concatenate([f3[0], f3[1], f3[2], k3[0], k3[1], k3[0], q3[0], q3[1], q3[2], pad], axis=0)
        left_t = left.T.astype(BF)
        right = jnp.concatenate(
            [jnp.concatenate(blk, axis=1) for blk in
             [(ones, zeros, zeros)] * 3 + [(zeros, v3[0], zeros), (zeros, v3[0], zeros), (zeros, v3[1], zeros)]
             + [(zeros, zeros, ones)] * 3] + [jnp.concatenate((pad, pad, pad), axis=1)], axis=0).astype(BF)
        o_rows = []
        for r in range(bs):
            prod = _dot(left_t * lane_masks_ref[r], right)
            s_new = prod[:, :HG_DV] * s_ref[r, hd] + prod[:, HG_DV:2 * HG_DV]
            snew_ref[r, hd] = s_new
            o_rows.append(jnp.sum(prod[:, 2 * HG_DV:] * s_new, axis=0, keepdims=True))
        o_ref[:, cs] = _head_norm(jnp.concatenate(o_rows, axis=0), hng_ref[:, cs])


def _lane_masks():
    r = jnp.arange(SUBLANES)[:, None, None]
    lane = jnp.arange(LANES)[None, None, :] % SUBLANES
    return jnp.broadcast_to(lane == r, (SUBLANES, LANES, LANES)).astype(BF)


def _sample_state(proj, state, wts, bs):
    n = proj.shape[0]
    col_spec = lambda off: pl.BlockSpec((bs, HG_DIM), lambda i, off=off: (i, off // HG_DIM))
    st_spec = pl.BlockSpec((bs, HG_HEADS, HG_DK, HG_DV), lambda i: (i, 0, 0, 0))
    return pl.pallas_call(
        functools.partial(_sample_state_kernel, bs=bs),
        grid=(n // bs,),
        in_specs=[col_spec(OFF_Q), col_spec(OFF_F), col_spec(OFF_V),
                  _const_spec(wts['lb_param'].shape), _const_spec(wts['hn_g'].shape),
                  _const_spec((SUBLANES, LANES, LANES)), st_spec],
        out_specs=[st_spec, pl.BlockSpec((bs, HG_DIM), lambda i: (i, 0))],
        out_shape=[jax.ShapeDtypeStruct(state.shape, F32), jax.ShapeDtypeStruct((n, HG_DIM), F32)],
        compiler_params=pltpu.CompilerParams(dimension_semantics=("arbitrary",), vmem_limit_bytes=VMEM_LIMIT),
        name="sample_state",
    )(proj, proj, proj, wts['lb_param'], wts['hn_g'], _lane_masks(), state)


def _sample_tail_kernel(x_ref, proj_ref, o_ref, cst_ref, bg_ref, cw_ref, cb_ref, lng_ref, lnb_ref, wco_ref,
                        who_ref, wout_ref, gffn_ref, wr_ref, x1_ref, h2_ref, route_ref, nconv_ref):
    keep = (CONV_WIDTH - 2) * CONV_DIM
    u = proj_ref[:, OFF_CV:OFF_CV + CONV_DIM] * _sigmoid(proj_ref[:, OFF_CG:OFF_CG + CONV_DIM])
    acc = u * cw_ref[pl.ds(CONV_WIDTH - 1, 1), :]
    for j in range(CONV_WIDTH - 1):
        acc = acc + cst_ref[:, j * CONV_DIM:(j + 1) * CONV_DIM] * cw_ref[pl.ds(j, 1), :]
    nconv_ref[:, :keep] = cst_ref[:, CONV_DIM:]
    nconv_ref[:, keep:] = u
    c = _conv_post(acc, cb_ref[...], lng_ref[...], lnb_ref[...])
    y_a = _dot(c.astype(BF), wco_ref[...])
    og = proj_ref[:, OFF_OG:OFF_OG + HG_DIM]
    y_b = _dot((o_ref[...] * _silu(og)).astype(BF), who_ref[...])
    m = (_sigmoid(proj_ref[:, OFF_GA:OFF_GA + D_MODEL] + bg_ref[:, :D_MODEL]) * y_a
         + _sigmoid(proj_ref[:, OFF_GB:OFF_GB + D_MODEL] + bg_ref[:, D_MODEL:]) * y_b)
    x1, h2, route = _mixer_tail(x_ref[...], m, wout_ref[...], gffn_ref[...], wr_ref[...])
    x1_ref[...] = x1
    _store_row_tiles(h2_ref, h2)
    route_ref[...] = route


def _sample_tail(xs, proj, o, conv_state2, wts):
    n = xs.shape[0]
    ins = [xs, proj, o, conv_state2, wts['b_gates'], wts['conv_w'], wts['conv_b'], wts['ln_g'], wts['ln_b'],
           wts['w_conv_out'], wts['w_hgrn_out'], wts['w_out'], wts['gffn'], wts['w_router']]
    full = lambda shape: pl.BlockSpec(shape, lambda i: (0,) * len(shape))
    return pl.pallas_call(
        _sample_tail_kernel,
        grid=(1,),
        in_specs=[_const_spec(a.shape) for a in ins],
        out_specs=[full((n, D_MODEL)), full((n * SUBLANES, LANES)), full((SUBLANES, n)), full(conv_state2.shape)],
        out_shape=[jax.ShapeDtypeStruct((n, D_MODEL), F32), jax.ShapeDtypeStruct((n * SUBLANES, LANES), F32),
                   jax.ShapeDtypeStruct((SUBLANES, n), F32), jax.ShapeDtypeStruct(conv_state2.shape, F32)],
        compiler_params=pltpu.CompilerParams(vmem_limit_bytes=VMEM_LIMIT),
        name="sample_tail",
    )(*ins)


def _plan_kernel(route_p_ref, route_s_ref, pos_ref, tinfo_ref, e_ref, pre_ref, *, tm, n_tok, max_tiles):
    n_p = route_p_ref.shape[1]
    nblk = n_tok // LANES
    e_ref[:, pl.ds(0, n_p)] = route_p_ref[...]
    e_ref[:, pl.ds(n_p, n_tok - n_p)] = route_s_ref[...]
    eidx = lax.broadcasted_iota(jnp.int32, (N_EXPERTS, LANES), 0).astype(F32)
    ti = lax.broadcasted_iota(jnp.int32, (LANES, LANES), 0)
    si = lax.broadcasted_iota(jnp.int32, (LANES, LANES), 1)
    before = (ti < si).astype(BF)

    def onehots(b):
        c0 = pl.multiple_of(b * LANES, LANES)
        e1 = e_ref[0:1, pl.ds(c0, LANES)]
        e2 = e_ref[1:2, pl.ds(c0, LANES)]
        return (e1 == eidx).astype(F32), (e2 == eidx).astype(F32), c0

    def count(b, carry):
        h1, h2, c0 = onehots(b)
        h = h1 + h2
        pre_ref[:, pl.ds(c0, LANES)] = _dot(h.astype(BF), before) + carry
        return carry + jnp.sum(h, axis=1, keepdims=True)

    counts = lax.fori_loop(0, nblk, count, jnp.zeros((N_EXPERTS, 1), F32))
    tiles_per = jnp.floor((counts + (tm - 1)) * (1.0 / tm))
    ei = lax.broadcasted_iota(jnp.int32, (N_EXPERTS, N_EXPERTS), 0)
    ej = lax.broadcasted_iota(jnp.int32, (N_EXPERTS, N_EXPERTS), 1)
    upto = (ej <= ei).astype(BF)
    tile_end = _dot(upto, jnp.broadcast_to(tiles_per, (N_EXPERTS, LANES)).astype(BF))[:, 0:1]
    starts = (tile_end - tiles_per) * tm

    def place(b, c):
        h1, h2, c0 = onehots(b)
        dest = pre_ref[:, pl.ds(c0, LANES)] + starts
        pos_ref[0:1, pl.ds(c0, LANES)] = jnp.sum(h1 * dest, axis=0, keepdims=True).astype(jnp.int32)
        pos_ref[1:2, pl.ds(c0, LANES)] = jnp.sum(h2 * dest, axis=0, keepdims=True).astype(jnp.int32)
        return c

    pos_ref[...] = jnp.zeros_like(pos_ref)
    lax.fori_loop(0, nblk, place, 0)

    tile = lax.broadcasted_iota(jnp.int32, (N_EXPERTS, max_tiles), 1).astype(F32)
    t_exp = jnp.sum((tile_end <= tile).astype(F32), axis=0, keepdims=True)
    t_exp = jnp.minimum(t_exp, N_EXPERTS - 1.0).astype(jnp.int32)
    n_used = jnp.broadcast_to(tile_end[N_EXPERTS - 1:N_EXPERTS, :], (1, max_tiles)).astype(jnp.int32)
    own = (lax.broadcasted_iota(jnp.int32, (N_EXPERTS, max_tiles), 0)
           == lax.broadcasted_iota(jnp.int32, (N_EXPERTS, max_tiles), 1)).astype(F32)
    last_tile = jnp.sum(own * (tile_end - 1.0), axis=0, keepdims=True).astype(jnp.int32)
    n_of = jnp.sum(own * tiles_per, axis=0, keepdims=True).astype(jnp.int32)
    row = lax.broadcasted_iota(jnp.int32, (SUBLANES, max_tiles), 0)
    tinfo_ref[...] = jnp.where(row == 0, t_exp, jnp.where(row == 1, n_used,
                                                           jnp.where(row == 2, last_tile, jnp.where(row == 3, n_of, 0))))


def _plan(route_p, route_s, tm, max_tiles):
    n_tok = route_p.shape[1] + route_s.shape[1]
    assert route_p.shape[1] % LANES == 0 and route_s.shape[1] % LANES == 0
    vm = pl.BlockSpec(memory_space=pltpu.VMEM)
    return pl.pallas_call(
        functools.partial(_plan_kernel, tm=tm, n_tok=n_tok, max_tiles=max_tiles),
        in_specs=[vm, vm],
        out_specs=[vm, vm],
        out_shape=[jax.ShapeDtypeStruct((SUBLANES, n_tok), jnp.int32),
                   jax.ShapeDtypeStruct((SUBLANES, max_tiles), jnp.int32)],
        scratch_shapes=[pltpu.VMEM((SUBLANES, n_tok), F32), pltpu.VMEM((N_EXPERTS, n_tok), F32)],
        compiler_params=pltpu.CompilerParams(vmem_limit_bytes=VMEM_LIMIT),
        name="route_plan",
    )(route_p, route_s)


def _scatter_rows(pos_ref, h_ref, xs_ref, sem, tb, tok0):
    i = pl.program_id(0)

    def copies(r):
        base = 2 * (tok0 + i * tb + r)
        tile = lambda ref, row: ref.at[pl.ds(pl.multiple_of(row * SUBLANES, SUBLANES), SUBLANES), :]
        return (pltpu.make_async_copy(tile(h_ref, r), tile(xs_ref, pos_ref[base]), sem),
                pltpu.make_async_copy(tile(h_ref, r), tile(xs_ref, pos_ref[base + 1]), sem))

    def issue(r, c):
        c0, c1 = copies(r)
        c0.start(priority=0)
        c1.start(priority=1)
        return c

    def drain(r, c):
        c0, c1 = copies(r)
        c0.wait()
        c1.wait()
        return c

    lax.fori_loop(0, tb, issue, 0, unroll=8)
    lax.fori_loop(0, tb, drain, 0, unroll=8)


def _dispatch_first_kernel(pos_ref, tinfo_ref, h_ref, xs_ref, zeros_ref, sem, zero_sem, *, tb, tm, max_tiles):
    @pl.when(pl.program_id(0) == 0)
    def _():
        zeros_ref[...] = jnp.zeros_like(zeros_ref)
        n_used = tinfo_ref[max_tiles]

        def fill(tile):
            rows = pl.ds(pl.multiple_of(tile * (tm * SUBLANES), tm * SUBLANES), tm * SUBLANES)
            return pltpu.make_async_copy(zeros_ref, xs_ref.at[rows, :], zero_sem)

        def each_tile(action):
            for e in range(N_EXPERTS):
                @pl.when(tinfo_ref[3 * max_tiles + e] > 0)
                def _():
                    action(fill(tinfo_ref[2 * max_tiles + e]))

            def unused(tile, c):
                action(fill(tile))
                return c

            lax.fori_loop(n_used, max_tiles, unused, 0)

        each_tile(lambda copy: copy.start())
        each_tile(lambda copy: copy.wait())

    _scatter_rows(pos_ref, h_ref, xs_ref, sem, tb, 0)


def _dispatch_more_kernel(pos_ref, h_ref, xs_in_ref, xs_ref, sem, *, tb, tok0):
    del xs_in_ref
    _scatter_rows(pos_ref, h_ref, xs_ref, sem, tb, tok0)


def _dispatch_first(pos_flat, tinfo_flat, hpk, tb, tm, max_tiles):
    n = hpk.shape[0] // SUBLANES
    grid_spec = pltpu.PrefetchScalarGridSpec(
        num_scalar_prefetch=2,
        grid=(n // tb,),
        in_specs=[pl.BlockSpec((tb * SUBLANES, LANES), lambda i, p, t: (i, 0))],
        out_specs=pl.BlockSpec(memory_space=pl.ANY),
        scratch_shapes=[pltpu.VMEM((tm * SUBLANES, LANES), F32), pltpu.SemaphoreType.DMA(()),
                        pltpu.SemaphoreType.DMA(())])
    return pl.pallas_call(
        functools.partial(_dispatch_first_kernel, tb=tb, tm=tm, max_tiles=max_tiles),
        grid_spec=grid_spec,
        out_shape=jax.ShapeDtypeStruct((max_tiles * tm * SUBLANES, LANES), F32),
        compiler_params=pltpu.CompilerParams(dimension_semantics=("arbitrary",), vmem_limit_bytes=VMEM_LIMIT),
        name="dispatch",
    )(pos_flat, tinfo_flat, hpk)


def _dispatch_more(pos_flat, hpk, xs, tb, tok0):
    n = hpk.shape[0] // SUBLANES
    grid_spec = pltpu.PrefetchScalarGridSpec(
        num_scalar_prefetch=1,
        grid=(n // tb,),
        in_specs=[pl.BlockSpec((tb * SUBLANES, LANES), lambda i, p: (i, 0)),
                  pl.BlockSpec(memory_space=pl.ANY)],
        out_specs=pl.BlockSpec(memory_space=pl.ANY),
        scratch_shapes=[pltpu.SemaphoreType.DMA(())])
    return pl.pallas_call(
        functools.partial(_dispatch_more_kernel, tb=tb, tok0=tok0),
        grid_spec=grid_spec,
        out_shape=jax.ShapeDtypeStruct(xs.shape, xs.dtype),
        input_output_aliases={2: 0},
        compiler_params=pltpu.CompilerParams(dimension_semantics=("arbitrary",), vmem_limit_bytes=VMEM_LIMIT),
        name="dispatch",
    )(pos_flat, hpk, xs)


def _store_row_tiles(ref, rows):
    n = rows.shape[0]
    for s in range(SUBLANES):
        ref[pl.ds(s, n, stride=SUBLANES), :] = rows[:, s * LANES:(s + 1) * LANES]


def _load_row_tiles(ref):
    n = ref.shape[0] // SUBLANES
    return jnp.concatenate([ref[pl.ds(s, n, stride=SUBLANES), :] for s in range(SUBLANES)], axis=1)


def _expert_kernel(te_ref, nt_ref, xs_ref, wg_ref, wu_ref, wd_ref, y_ref, wg_b, wu_b, wd_b):
    i = pl.program_id(0)

    @pl.when(i < nt_ref[0])
    def _():
        changed = jnp.logical_or(i == 0, te_ref[i] != te_ref[jnp.maximum(i - 1, 0)])

        @pl.when(changed)
        def _():
            wg_b[...] = wg_ref[0].astype(BF)
            wu_b[...] = wu_ref[0].astype(BF)
            wd_b[...] = wd_ref[0].astype(BF)

        xb = _load_row_tiles(xs_ref).astype(BF)
        gate = _dot(xb, wg_b[...])
        up = _dot(xb, wu_b[...])
        _store_row_tiles(y_ref, _dot((_silu(gate) * up).astype(BF), wd_b[...]))

    @pl.when(i >= nt_ref[0])
    def _():
        y_ref[...] = jnp.zeros_like(y_ref)


def _experts(xs, tile_expert, n_tiles, wg, wu, wd, tm, max_tiles):
    grid_spec = pltpu.PrefetchScalarGridSpec(
        num_scalar_prefetch=2,
        grid=(max_tiles,),
        in_specs=[pl.BlockSpec((tm * SUBLANES, LANES),
                               lambda i, te, nt: (jnp.minimum(i, nt[0] - 1), 0)),
                  pl.BlockSpec((1, D_MODEL, EXPERT_FF), lambda i, te, nt: (te[i], 0, 0)),
                  pl.BlockSpec((1, D_MODEL, EXPERT_FF), lambda i, te, nt: (te[i], 0, 0)),
                  pl.BlockSpec((1, EXPERT_FF, D_MODEL), lambda i, te, nt: (te[i], 0, 0))],
        out_specs=pl.BlockSpec((tm * SUBLANES, LANES), lambda i, te, nt: (i, 0)),
        scratch_shapes=[pltpu.VMEM((D_MODEL, EXPERT_FF), BF), pltpu.VMEM((D_MODEL, EXPERT_FF), BF),
                        pltpu.VMEM((EXPERT_FF, D_MODEL), BF)])
    return pl.pallas_call(
        _expert_kernel,
        grid_spec=grid_spec,
        out_shape=jax.ShapeDtypeStruct((max_tiles * tm * SUBLANES, LANES), F32),
        compiler_params=pltpu.CompilerParams(dimension_semantics=("arbitrary",), vmem_limit_bytes=VMEM_LIMIT),
        name="experts",
    )(tile_expert, n_tiles, xs, wg, wu, wd)


def _combine_kernel(pos_ref, x1_ref, route_ref, gfin_ref, ys_ref, y_ref, buf0, buf1, sems, *, tb, tok0):
    i = pl.program_id(0)
    n_steps = pl.num_programs(0)
    slot = i % 2

    def copies(step, half, r):
        base = 2 * (tok0 + step * tb + r)
        tile = lambda ref, row: ref.at[pl.ds(pl.multiple_of(row * SUBLANES, SUBLANES), SUBLANES), :]
        return (pltpu.make_async_copy(tile(ys_ref, pos_ref[base]), tile(buf0.at[half], r), sems.at[half]),
                pltpu.make_async_copy(tile(ys_ref, pos_ref[base + 1]), tile(buf1.at[half], r), sems.at[half]))

    def request(step, half):
        def body(r, c):
            c0, c1 = copies(step, half, r)
            c0.start(priority=0)
            c1.start(priority=1)
            return c

        lax.fori_loop(0, tb, body, 0, unroll=8)

    @pl.when(i == 0)
    def _():
        request(0, 0)

    @pl.when(i + 1 < n_steps)
    def _():
        request(i + 1, 1 - slot)

    def receive(r, c):
        c0, c1 = copies(i, slot, r)
        c0.wait()
        c1.wait()
        return c

    lax.fori_loop(0, tb, receive, 0, unroll=8)
    pad = jnp.zeros((LANES - SUBLANES, LANES), F32)
    cols = jnp.concatenate([jnp.concatenate([route_ref[:, j:j + LANES], pad], axis=0).T
                            for j in range(0, tb, LANES)], axis=0)
    out = x1_ref[...] + (cols[:, 2:3] * _load_row_tiles(buf0.at[slot]) + cols[:, 3:4] * _load_row_tiles(buf1.at[slot]))
    y_ref[...] = _rms(out, gfin_ref[...])


def _combine(pos_flat, x1, route, gfin, ys, tb, tok0):
    n = x1.shape[0]
    grid_spec = pltpu.PrefetchScalarGridSpec(
        num_scalar_prefetch=1,
        grid=(n // tb,),
        in_specs=[pl.BlockSpec((tb, D_MODEL), lambda i, p: (i, 0)),
                  pl.BlockSpec((SUBLANES, tb), lambda i, p: (0, i)),
                  pl.BlockSpec((1, D_MODEL), lambda i, p: (0, 0)),
                  pl.BlockSpec(memory_space=pl.ANY)],
        out_specs=pl.BlockSpec((tb, D_MODEL), lambda i, p: (i, 0)),
        scratch_shapes=[pltpu.VMEM((2, tb * SUBLANES, LANES), F32), pltpu.VMEM((2, tb * SUBLANES, LANES), F32),
                        pltpu.SemaphoreType.DMA((2,))])
    return pl.pallas_call(
        functools.partial(_combine_kernel, tb=tb, tok0=tok0),
        grid_spec=grid_spec,
        out_shape=jax.ShapeDtypeStruct((n, D_MODEL), F32),
        compiler_params=pltpu.CompilerParams(dimension_semantics=("arbitrary",), vmem_limit_bytes=VMEM_LIMIT),
        name="combine",
    )(pos_flat, x1, route, gfin, ys)


def kernel(x_prompt, x_sample, state_conv, state_hgrn, norm_mix_g, w_in, b_gates, conv_dw_w, conv_dw_b,
           conv_ln_g, conv_ln_b, w_conv_out, hgrn_lb_param, hgrn_norm_g, w_hgrn_out, w_out, norm_ffn_g,
           w_router_group, w_router_expert, w_expert_gate, w_expert_up, w_expert_down, norm_final_g):
    batch, seq, _ = x_prompt.shape
    dec_batch = x_sample.shape[0]
    assert x_sample.shape[1] == 1 and w_in.shape[0] == 1
    tb = min(256, seq)
    tm = 256
    bs = min(8, dec_batch)
    n_p = batch * seq
    n_all = n_p + dec_batch
    tbd = min(1024, n_p)
    tbc = min(512, n_p)
    assert seq % tb == 0 and tb % CHUNK == 0 and dec_batch % bs == 0 and n_p % tbd == 0 and n_p % tbc == 0

    w_router = jnp.concatenate(
        [w_router_group[0].T, w_router_expert[0].T,
         jnp.zeros((ROUTE_ROWS - N_GROUPS - N_EXPERTS, D_MODEL), F32)], axis=0).astype(BF)
    wts = dict(gmix=norm_mix_g, w_in=w_in[0].astype(BF), b_gates=b_gates, conv_w=conv_dw_w[0], conv_b=conv_dw_b,
               ln_g=conv_ln_g, ln_b=conv_ln_b, w_conv_out=w_conv_out[0].astype(BF), lb_param=hgrn_lb_param,
               hn_g=hgrn_norm_g, w_hgrn_out=w_hgrn_out[0].astype(BF), w_out=w_out[0].astype(BF),
               gffn=norm_ffn_g, w_router=w_router)

    x1_p, hpk_p, route_p, nconv_p, nhgrn_p = _prompt_mixer(x_prompt.reshape(n_p, D_MODEL), batch, seq, tb, wts)

    xs_tok = x_sample.reshape(dec_batch, D_MODEL)
    proj_s = _sample_proj(xs_tok, wts)
    nhgrn_s, o_s = _sample_state(proj_s, state_hgrn[0], wts, bs)
    conv2 = state_conv[0].reshape(dec_batch, (CONV_WIDTH - 1) * CONV_DIM)
    x1_s, hpk_s, route_s, nconv_s = _sample_tail(xs_tok, proj_s, o_s, conv2, wts)

    max_tiles = -(-((2 * n_all) // tm + N_EXPERTS) // SUBLANES) * SUBLANES
    pos, tinfo = _plan(route_p, route_s, tm, max_tiles)
    pos_flat = pos[0:2].T.reshape(-1)
    tile_expert, n_tiles = tinfo[0], tinfo[1, 0:1]

    xs = _dispatch_first(pos_flat, tinfo.reshape(-1), hpk_p, tbd, tm, max_tiles)
    xs = _dispatch_more(pos_flat, hpk_s, xs, dec_batch, n_p)
    ys = _experts(xs, tile_expert, n_tiles, w_expert_gate[0], w_expert_up[0], w_expert_down[0], tm, max_tiles)

    gfin = norm_final_g.reshape(1, D_MODEL)
    y_p = _combine(pos_flat, x1_p, route_p, gfin, ys, tbc, 0)
    y_s = _combine(pos_flat, x1_s, route_s, gfin, ys, dec_batch, n_p)

    return (y_p.reshape(batch, seq, D_MODEL), y_s.reshape(dec_batch, 1, D_MODEL),
            nconv_p[None, :, HIST_PAD:, :], nhgrn_p[None],
            nconv_s.reshape(1, dec_batch, CONV_WIDTH - 1, CONV_DIM), nhgrn_s[None])
```

```python
import functools

import jax
import jax.numpy as jnp
from jax import lax
from jax.experimental import pallas as pl
from jax.experimental.pallas import tpu as pltpu

D_MODEL = 1024
CONV_DIM = D_MODEL // 2
CONV_WIDTH = 31
HG_HEADS = 8
HG_DK = 128
HG_DV = 128
HG_DIM = HG_HEADS * HG_DK
N_GROUPS = 4
EXPERTS_PER_GROUP = 8
N_EXPERTS = N_GROUPS * EXPERTS_PER_GROUP
EXPERT_FF = D_MODEL // 2
EPS = 1e-6
IN_COLS = 2 * CONV_DIM + 4 * HG_DIM + 2 * D_MODEL

OFF_CV, OFF_CG = 0, CONV_DIM
OFF_Q = 2 * CONV_DIM
OFF_F = OFF_Q + HG_DIM
OFF_V = OFF_F + HG_DIM
OFF_OG = OFF_V + HG_DIM
OFF_GA = OFF_OG + HG_DIM
OFF_GB = OFF_GA + D_MODEL

SUBLANES = 8
LANES = 128
HIST_ROWS = 32
HIST_PAD = HIST_ROWS - (CONV_WIDTH - 1)
CHUNK = 64
NBLK = CHUNK // SUBLANES
CONV_ROWS = 32
PROJ_COLS = 256
ROUTE_ROWS = 128
VMEM_LIMIT = 56 * 1024 * 1024

BF = jnp.bfloat16
F32 = jnp.float32


def _dot(a, b):
    return jnp.dot(a, b, preferred_element_type=F32)


def _dot_nt(a, b):
    return lax.dot_general(a, b, (((1,), (1,)), ((), ())), preferred_element_type=F32)


def _dot_tn(a, b, precision=None):
    return lax.dot_general(a, b, (((0,), (0,)), ((), ())), preferred_element_type=F32,
                           precision=precision)


NEG_LOG2E = -1.4426950408889634


def _sigmoid(x):
    return 1.0 / (1.0 + jnp.exp2(x * NEG_LOG2E))


def _silu(x):
    return x * _sigmoid(x)


def _rms(xf, g):
    return xf * lax.rsqrt(jnp.mean(xf * xf, axis=-1, keepdims=True) + EPS) * g


def _lower_bound(lb_param):
    m = jnp.max(lb_param, axis=0, keepdims=True)
    e = jnp.exp(lb_param - m)
    return e[0:1] / jnp.sum(e, axis=0, keepdims=True)


def _conv_post(c, conv_b, ln_g, ln_b):
    c = c + conv_b
    mu = jnp.mean(c, axis=-1, keepdims=True)
    d = c - mu
    var = jnp.mean(d * d, axis=-1, keepdims=True)
    return _silu(d * lax.rsqrt(var + EPS) * ln_g + ln_b)


def _head_norm(o, g):
    return o * lax.rsqrt(jnp.mean(o * o, axis=-1, keepdims=True) + EPS) * g


def _route(logits_t):
    used = -(-(N_GROUPS + N_EXPERTS) // SUBLANES) * SUBLANES
    logits = logits_t[0:used, :]
    row = lax.broadcasted_iota(jnp.int32, logits.shape, 0).astype(F32)
    big = jnp.float32(1 << 20)
    neg = jnp.float32(-jnp.inf)
    gmask = row < N_GROUPS
    lg = jnp.where(gmask, logits, neg)
    gmax = jnp.max(lg, axis=0, keepdims=True)
    gsum = jnp.sum(jnp.where(gmask, jnp.exp(lg - gmax), 0.0), axis=0, keepdims=True)
    gval = 1.0 / gsum
    gidx = jnp.min(jnp.where(lg == gmax, row, big), axis=0, keepdims=True)
    lo = N_GROUPS + EXPERTS_PER_GROUP * gidx
    emask = (row >= lo) & (row < lo + EXPERTS_PER_GROUP)
    el = jnp.where(emask, logits, neg)
    m1 = jnp.max(el, axis=0, keepdims=True)
    i1 = jnp.min(jnp.where(el == m1, row, big), axis=0, keepdims=True)
    el2 = jnp.where(row == i1, neg, el)
    m2 = jnp.max(el2, axis=0, keepdims=True)
    i2 = jnp.min(jnp.where(el2 == m2, row, big), axis=0, keepdims=True)
    r = jnp.exp(m2 - m1)
    w1 = gval / (1.0 + r)
    w2 = gval * r / (1.0 + r)
    out_row = lax.broadcasted_iota(jnp.int32, (SUBLANES, logits.shape[1]), 0)
    return jnp.where(out_row == 0, i1 - N_GROUPS,
                     jnp.where(out_row == 1, i2 - N_GROUPS, jnp.where(out_row == 2, w1, jnp.where(out_row == 3, w2, 0.0))))


LEVELS = (NBLK // 2, NBLK // 4, NBLK // 8)
ROW_TOT, ROW_QE, ROW_KD, ROW_LEVEL = 0, NBLK, 2 * NBLK, 3 * NBLK
TABLE_ROWS = ROW_LEVEL + 2 * NBLK * len(LEVELS)


def _row_bcast(ref, row):
    return jnp.broadcast_to(ref[pl.ds(row, 1), :], (SUBLANES, LANES))


def _per_block(ref, base):
    return jnp.concatenate([_row_bcast(ref, base + j) for j in range(NBLK)], axis=0)


def _block_rows(ref, c):
    return jnp.concatenate([_row_bcast(ref, j * SUBLANES + c) for j in range(NBLK)], axis=0)


def _sparse_tile(x, ref, base, blocks):
    groups = []
    for g in range(0, NBLK, 2):
        if g not in blocks and g + 1 not in blocks:
            groups.append(jnp.zeros((2 * SUBLANES, LANES), BF))
            continue
        halves = [x[j * SUBLANES:(j + 1) * SUBLANES, :] * _row_bcast(ref, base + j) if j in blocks
                  else jnp.zeros((SUBLANES, LANES), F32) for j in (g, g + 1)]
        groups.append(jnp.concatenate(halves, axis=0).astype(BF))
    return jnp.concatenate(groups, axis=0)


def _chunk_cumsum(l2, bl_ref):
    rin = lax.broadcasted_iota(jnp.int32, (CHUNK, LANES), 0) % SUBLANES
    bl = l2
    for s in (1, 2, 4):
        bl = bl + jnp.where(rin >= s, pltpu.roll(bl, s, axis=0), 0.0)
    bl_ref[...] = bl
    return bl


def _chunk_table(bl_ref, tab_ref):
    tot = bl_ref[pl.ds(SUBLANES - 1, NBLK, stride=SUBLANES), :]
    brow = lax.broadcasted_iota(jnp.int32, (NBLK, LANES), 0)
    rb = tot
    for s in (1, 2, 4):
        rb = rb + jnp.where(brow >= s, pltpu.roll(rb, s, axis=0), 0.0)
    rb_prev = rb - tot
    total = rb[NBLK - 1:NBLK, :]

    tab_ref[pl.ds(ROW_TOT, NBLK), :] = tot
    tab_ref[pl.ds(ROW_QE, NBLK), :] = jnp.exp2(rb_prev)
    tab_ref[pl.ds(ROW_KD, NBLK), :] = jnp.exp2(total - rb)
    for lv, cb in enumerate(LEVELS):
        edges = [(j // (2 * cb)) * (2 * cb) + cb - 1 for j in range(NBLK)]
        rb_edge = jnp.concatenate([rb[e:e + 1, :] for e in edges], axis=0)
        base = ROW_LEVEL + 2 * NBLK * lv
        tab_ref[pl.ds(base, NBLK), :] = jnp.exp2(jnp.minimum(rb_prev - rb_edge, 0.0))
        tab_ref[pl.ds(base + NBLK, NBLK), :] = jnp.exp2(jnp.minimum(rb_edge - rb, 0.0))
    return total


def _chunk_scaled(q, k, bl, tab_ref):
    return q * jnp.exp2(bl), k * jnp.exp2(_per_block(tab_ref, ROW_TOT) - bl)


def _chunk_inter_operands(qp, kp, tab_ref):
    return (qp * _per_block(tab_ref, ROW_QE)).astype(BF), (kp * _per_block(tab_ref, ROW_KD)).astype(BF)


def _chunk_pair_operands(qp, kp, tab_ref):
    q_tiles, k_tiles = [], []
    for lv, cb in enumerate(LEVELS):
        base = ROW_LEVEL + 2 * NBLK * lv
        for p0 in range(0, NBLK, 2 * cb):
            q_tiles.append(_sparse_tile(qp, tab_ref, base, range(p0 + cb, p0 + 2 * cb)))
            k_tiles.append(_sparse_tile(kp, tab_ref, base + NBLK, range(p0, p0 + cb)))
    return jnp.concatenate(q_tiles, axis=1), jnp.concatenate(k_tiles, axis=1)


def _chunk_block_operands(q, k, bl, row_masks, bl_ref):
    k_b = k.astype(BF)
    lhs = [(q * jnp.exp2(jnp.minimum(bl - _block_rows(bl_ref, c), 0.0))).astype(BF) for c in range(SUBLANES)]
    rhs = [k_b * row_masks[c] for c in range(SUBLANES)]
    return jnp.concatenate(lhs, axis=1), jnp.concatenate(rhs, axis=1)


def _chunk_products(qe, kdec, v_b, pair_ops, block_ops, st):
    return (_dot_nt(qe, st.astype(BF)), _dot_tn(v_b, kdec), _dot_nt(*pair_ops), _dot_nt(*block_ops))


def _chunk_output(o_inter, off_diag, diag, v_b):
    ti = lax.broadcasted_iota(jnp.int32, (CHUNK, CHUNK), 0)
    si = lax.broadcasted_iota(jnp.int32, (CHUNK, CHUNK), 1)
    scores = jnp.where((ti // SUBLANES == si // SUBLANES) & (si <= ti), diag, off_diag)
    return o_inter + _dot(scores.astype(BF), v_b)


def _mixer_tail(x, m, w_out, g_ffn, w_router_t):
    x1 = x + _dot(m.astype(BF), w_out)
    h2 = _rms(x1, g_ffn).astype(BF)
    route = _route(_dot_nt(w_router_t, h2))
    return x1, h2.astype(F32), route


def _prompt_mixer_kernel(x_ref, gmix_ref, win_ref, bg_ref, cw_ref, cb_ref, lng_ref, lnb_ref, wco_ref,
                         lbp_ref, hng_ref, who_ref, wout_ref, gffn_ref, wr_ref, masks_ref,
                         x1_ref, h2_ref, route_ref, nconv_ref, nhgrn_ref,
                         hist_ref, phase_ref, conv_ref, st_ref, q_s, k_s, v_s, lf_s, o_s, og_s, ga_s, gb_s, *head_scr, tb):
    t = pl.program_id(1)

    @pl.when(t == 0)
    def _():
        hist_ref[pl.ds(0, HIST_ROWS), :] = jnp.zeros((HIST_ROWS, CONV_DIM), F32)
        st_ref[...] = jnp.zeros_like(st_ref)

    x = x_ref[...]
    h = _rms(x, gmix_ref[...]).astype(BF)

    cv = _dot(h, win_ref[:, OFF_CV:OFF_CV + CONV_DIM])
    cg = _dot(h, win_ref[:, OFF_CG:OFF_CG + CONV_DIM])
    hist_ref[pl.ds(HIST_ROWS, tb), :] = cv * _sigmoid(cg)

    lb = _lower_bound(lbp_ref[...])

    def project(seg, c0):
        z = _dot(h, win_ref[:, seg + c0:seg + c0 + PROJ_COLS])
        cols = slice(c0, c0 + PROJ_COLS)
        if seg == OFF_Q:
            q_s[:, cols] = _silu(z) * (HG_DK ** -0.5)
        elif seg == OFF_F:
            fg = lb[:, cols] + (1.0 - lb[:, cols]) * _sigmoid(z)
            lf_s[:, cols] = jnp.log2(fg)
            k_s[:, cols] = 1.0 - fg
        elif seg == OFF_V:
            v_s[:, cols] = z
        elif seg == OFF_OG:
            og_s[:, cols] = _silu(z)
        elif seg == OFF_GA:
            ga_s[:, cols] = _sigmoid(z + bg_ref[:, c0:c0 + PROJ_COLS])
        else:
            gb_s[:, cols] = _sigmoid(z + bg_ref[:, D_MODEL + c0:D_MODEL + c0 + PROJ_COLS])

    proj_jobs = [functools.partial(project, seg, c0)
                 for seg in (OFF_Q, OFF_F, OFF_V, OFF_OG, OFF_GA, OFF_GB) for c0 in range(0, HG_DIM, PROJ_COLS)]

    span = tb + SUBLANES * (-(-CONV_WIDTH // SUBLANES) - 1)

    def phase_copy(r):
        n = min(span, HIST_ROWS + tb - HIST_PAD - r)
        phase_ref[r, pl.ds(0, n), :] = hist_ref[pl.ds(HIST_PAD + r, n), :]

    def conv_rows(r0):
        acc = jnp.zeros((CONV_ROWS, CONV_DIM), F32)
        for j in range(CONV_WIDTH):
            w_j = jnp.broadcast_to(cw_ref[pl.ds(j, 1), :], (SUBLANES, CONV_DIM))
            acc = acc + (phase_ref[j % SUBLANES, pl.ds(r0 + j - j % SUBLANES, CONV_ROWS), :]
                         * jnp.concatenate([w_j] * (CONV_ROWS // SUBLANES), axis=0))
        conv_ref[pl.ds(r0, CONV_ROWS), :] = acc

    conv_jobs = ([functools.partial(phase_copy, r) for r in range(SUBLANES)]
                 + [functools.partial(conv_rows, r0) for r0 in range(0, tb, CONV_ROWS)])
    per_conv_job = -(-len(proj_jobs) // len(conv_jobs))
    for i, job in enumerate(conv_jobs):
        job()
        for pj in proj_jobs[i * per_conv_job:(i + 1) * per_conv_job]:
            pj()
    for pj in proj_jobs[len(conv_jobs) * per_conv_job:]:
        pj()

    tail = hist_ref[pl.ds(tb, HIST_ROWS), :]
    hist_ref[pl.ds(0, HIST_ROWS), :] = tail
    nconv_ref[0] = tail
    c = _conv_post(conv_ref[...], cb_ref[...], lng_ref[...], lnb_ref[...])
    y_a = _dot(c.astype(BF), wco_ref[...])

    row_masks = masks_ref[...]
    cols = [slice(hd * HG_DK, (hd + 1) * HG_DK) for hd in range(HG_HEADS)]
    bl_refs, tab_refs = head_scr[:HG_HEADS], head_scr[HG_HEADS:]
    heads = range(HG_HEADS)
    for r0 in range(0, tb, CHUNK):
        rows = pl.ds(r0, CHUNK)
        q = [q_s[rows, cols[hd]] for hd in heads]
        k = [k_s[rows, cols[hd]] for hd in heads]
        bl = [_chunk_cumsum(lf_s[rows, cols[hd]], bl_refs[hd]) for hd in heads]
        total = [_chunk_table(bl_refs[hd], tab_refs[hd]) for hd in heads]
        scaled = [_chunk_scaled(q[hd], k[hd], bl[hd], tab_refs[hd]) for hd in heads]
        inter = [_chunk_inter_operands(*scaled[hd], tab_refs[hd]) for hd in heads]
        pair_ops = [_chunk_pair_operands(*scaled[hd], tab_refs[hd]) for hd in heads]
        block_ops = [_chunk_block_operands(q[hd], k[hd], bl[hd], row_masks, bl_refs[hd]) for hd in heads]
        v_b = [v_s[rows, cols[hd]].astype(BF) for hd in heads]
        prods = [_chunk_products(*inter[hd], v_b[hd], pair_ops[hd], block_ops[hd], st_ref[hd]) for hd in heads]
        for hd in heads:
            o_inter, update, off_diag, diag = prods[hd]
            st_ref[hd] = st_ref[hd] * jnp.exp2(total[hd]) + update
            o = _chunk_output(o_inter, off_diag, diag, v_b[hd])
            o_s[rows, cols[hd]] = _head_norm(o, hng_ref[:, cols[hd]])
    for hd in range(HG_HEADS):
        nhgrn_ref[0, hd] = st_ref[hd].T

    y_b = _dot((o_s[...] * og_s[...]).astype(BF), who_ref[...])
    x1, h2, route = _mixer_tail(x, ga_s[...] * y_a + gb_s[...] * y_b, wout_ref[...], gffn_ref[...], wr_ref[...])
    x1_ref[...] = x1
    _store_row_tiles(h2_ref, h2)
    route_ref[...] = route


def _const_spec(shape):
    nd = len(shape)
    return pl.BlockSpec(shape, lambda *_: (0,) * nd, pipeline_mode=pl.Buffered(1))


def _row_masks():
    c = jnp.arange(SUBLANES)[:, None, None]
    r = jnp.arange(CHUNK)[None, :, None] % SUBLANES
    return jnp.broadcast_to(r == c, (SUBLANES, CHUNK, LANES)).astype(BF)


def _prompt_mixer(x2, batch, seq, tb, wts):
    nt = seq // tb
    n = batch * seq
    row_spec = lambda w: pl.BlockSpec((tb, w), lambda b, t: (b * nt + t, 0))
    consts = [wts['gmix'], wts['w_in'], wts['b_gates'], wts['conv_w'], wts['conv_b'], wts['ln_g'], wts['ln_b'],
              wts['w_conv_out'], wts['lb_param'], wts['hn_g'], wts['w_hgrn_out'], wts['w_out'], wts['gffn'],
              wts['w_router'], _row_masks()]
    span = tb + SUBLANES * (-(-CONV_WIDTH // SUBLANES) - 1)
    return pl.pallas_call(
        functools.partial(_prompt_mixer_kernel, tb=tb),
        grid=(batch, nt),
        in_specs=[row_spec(D_MODEL)] + [_const_spec(c.shape) for c in consts],
        out_specs=[row_spec(D_MODEL), pl.BlockSpec((tb * SUBLANES, LANES), lambda b, t: (b * nt + t, 0)),
                   pl.BlockSpec((SUBLANES, tb), lambda b, t: (0, b * nt + t)),
                   pl.BlockSpec((1, HIST_ROWS, CONV_DIM), lambda b, t: (b, 0, 0)),
                   pl.BlockSpec((1, HG_HEADS, HG_DK, HG_DV), lambda b, t: (b, 0, 0, 0))],
        out_shape=[jax.ShapeDtypeStruct((n, D_MODEL), F32), jax.ShapeDtypeStruct((n * SUBLANES, LANES), F32),
                   jax.ShapeDtypeStruct((SUBLANES, n), F32),
                   jax.ShapeDtypeStruct((batch, HIST_ROWS, CONV_DIM), F32),
                   jax.ShapeDtypeStruct((batch, HG_HEADS, HG_DK, HG_DV), F32)],
        scratch_shapes=[pltpu.VMEM((HIST_ROWS + tb, CONV_DIM), F32),
                        pltpu.VMEM((SUBLANES, span, CONV_DIM), F32),
                        pltpu.VMEM((tb, CONV_DIM), F32),
                        pltpu.VMEM((HG_HEADS, HG_DV, HG_DK), F32)]
                       + [pltpu.VMEM((tb, HG_DIM), F32)] * 8
                       + [pltpu.VMEM((CHUNK, LANES), F32)] * HG_HEADS
                       + [pltpu.VMEM((TABLE_ROWS, LANES), F32)] * HG_HEADS,
        compiler_params=pltpu.CompilerParams(dimension_semantics=("arbitrary", "arbitrary"),
                                             vmem_limit_bytes=VMEM_LIMIT),
        name="prompt_mixer",
    )(x2, *consts)


def _sample_proj_kernel(x_ref, gmix_ref, win_ref, proj_ref):
    h = _rms(x_ref[...], gmix_ref[...]).astype(BF)
    proj_ref[...] = _dot(h, win_ref[...])


def _sample_proj(xs, wts):
    n = xs.shape[0]
    return pl.pallas_call(
        _sample_proj_kernel,
        grid=(1,),
        in_specs=[_const_spec(xs.shape), _const_spec(wts['gmix'].shape), _const_spec(wts['w_in'].shape)],
        out_specs=pl.BlockSpec((n, IN_COLS), lambda i: (0, 0)),
        out_shape=jax.ShapeDtypeStruct((n, IN_COLS), F32),
        compiler_params=pltpu.CompilerParams(vmem_limit_bytes=VMEM_LIMIT),
        name="sample_proj",
    )(xs, wts['gmix'], wts['w_in'])


def _split3(x):
    hi = x.astype(BF).astype(F32)
    mid = (x - hi).astype(BF).astype(F32)
    lo = ((x - hi) - mid).astype(BF).astype(F32)
    return hi, mid, lo


def _sample_state_kernel(q_ref, f_ref, v_ref, lbp_ref, hng_ref, lane_masks_ref, s_ref, snew_ref, o_ref, *, bs):
    assert bs == SUBLANES
    lb = _lower_bound(lbp_ref[...])
    qf = _silu(q_ref[...]) * (HG_DK ** -0.5)
    fg = lb + (1.0 - lb) * _sigmoid(f_ref[...])
    kf = 1.0 - fg
    v = v_ref[...]
    ones = jnp.ones((bs, HG_DV), F32)
    zeros = jnp.zeros((bs, HG_DV), F32)
    n_groups = 9
    pad = jnp.zeros((LANES - n_groups * bs, LANES), F32)
    for hd in range(HG_HEADS):
        cs = slice(hd * HG_DK, (hd + 1) * HG_DK)
        f3, k3, q3, v3 = _split3(fg[:, cs]), _split3(kf[:, cs]), _split3(qf[:, cs]), _split3(v[:, cs])
        left = jnp.concatenate([f3[0], f3[1], f3[2], k3[0], k3[1], k3[0], q3[0], q3[1], q3[2], pad], axis=0)
        left_t = left.T.astype(BF)
        right = jnp.concatenate(
            [jnp.concatenate(blk, axis=1) for blk in
             [(ones, zeros, zeros)] * 3 + [(zeros, v3[0], zeros), (zeros, v3[0], zeros), (zeros, v3[1], zeros)]
             + [(zeros, zeros, ones)] * 3] + [jnp.concatenate((pad, pad, pad), axis=1)], axis=0).astype(BF)
        o_rows = []
        for r in range(bs):
            prod = _dot(left_t * lane_masks_ref[r], right)
            s_new = prod[:, :HG_DV] * s_ref[r, hd] + prod[:, HG_DV:2 * HG_DV]
            snew_ref[r, hd] = s_new
            o_rows.append(jnp.sum(prod[:, 2 * HG_DV:] * s_new, axis=0, keepdims=True))
        o_ref[:, cs] = _head_norm(jnp.concatenate(o_rows, axis=0), hng_ref[:, cs])


def _lane_masks():
    r = jnp.arange(SUBLANES)[:, None, None]
    lane = jnp.arange(LANES)[None, None, :] % SUBLANES
    return jnp.broadcast_to(lane == r, (SUBLANES, LANES, LANES)).astype(BF)


def _sample_state(proj, state, wts, bs):
    n = proj.shape[0]
    col_spec = lambda off: pl.BlockSpec((bs, HG_DIM), lambda i, off=off: (i, off // HG_DIM))
    st_spec = pl.BlockSpec((bs, HG_HEADS, HG_DK, HG_DV), lambda i: (i, 0, 0, 0))
    return pl.pallas_call(
        functools.partial(_sample_state_kernel, bs=bs),
        grid=(n // bs,),
        in_specs=[col_spec(OFF_Q), col_spec(OFF_F), col_spec(OFF_V),
                  _const_spec(wts['lb_param'].shape), _const_spec(wts['hn_g'].shape),
                  _const_spec((SUBLANES, LANES, LANES)), st_spec],
        out_specs=[st_spec, pl.BlockSpec((bs, HG_DIM), lambda i: (i, 0))],
        out_shape=[jax.ShapeDtypeStruct(state.shape, F32), jax.ShapeDtypeStruct((n, HG_DIM), F32)],
        compiler_params=pltpu.CompilerParams(dimension_semantics=("arbitrary",), vmem_limit_bytes=VMEM_LIMIT),
        name="sample_state",
    )(proj, proj, proj, wts['lb_param'], wts['hn_g'], _lane_masks(), state)


def _sample_tail_kernel(x_ref, proj_ref, o_ref, cst_ref, bg_ref, cw_ref, cb_ref, lng_ref, lnb_ref, wco_ref,
                        who_ref, wout_ref, gffn_ref, wr_ref, x1_ref, h2_ref, route_ref, nconv_ref):
    keep = (CONV_WIDTH - 2) * CONV_DIM
    u = proj_ref[:, OFF_CV:OFF_CV + CONV_DIM] * _sigmoid(proj_ref[:, OFF_CG:OFF_CG + CONV_DIM])
    acc = u * cw_ref[pl.ds(CONV_WIDTH - 1, 1), :]
    for j in range(CONV_WIDTH - 1):
        acc = acc + cst_ref[:, j * CONV_DIM:(j + 1) * CONV_DIM] * cw_ref[pl.ds(j, 1), :]
    nconv_ref[:, :keep] = cst_ref[:, CONV_DIM:]
    nconv_ref[:, keep:] = u
    c = _conv_post(acc, cb_ref[...], lng_ref[...], lnb_ref[...])
    y_a = _dot(c.astype(BF), wco_ref[...])
    og = proj_ref[:, OFF_OG:OFF_OG + HG_DIM]
    y_b = _dot((o_ref[...] * _silu(og)).astype(BF), who_ref[...])
    m = (_sigmoid(proj_ref[:, OFF_GA:OFF_GA + D_MODEL] + bg_ref[:, :D_MODEL]) * y_a
         + _sigmoid(proj_ref[:, OFF_GB:OFF_GB + D_MODEL] + bg_ref[:, D_MODEL:]) * y_b)
    x1, h2, route = _mixer_tail(x_ref[...], m, wout_ref[...], gffn_ref[...], wr_ref[...])
    x1_ref[...] = x1
    _store_row_tiles(h2_ref, h2)
    route_ref[...] = route


def _sample_tail(xs, proj, o, conv_state2, wts):
    n = xs.shape[0]
    ins = [xs, proj, o, conv_state2, wts['b_gates'], wts['conv_w'], wts['conv_b'], wts['ln_g'], wts['ln_b'],
           wts['w_conv_out'], wts['w_hgrn_out'], wts['w_out'], wts['gffn'], wts['w_router']]
    full = lambda shape: pl.BlockSpec(shape, lambda i: (0,) * len(shape))
    return pl.pallas_call(
        _sample_tail_kernel,
        grid=(1,),
        in_specs=[_const_spec(a.shape) for a in ins],
        out_specs=[full((n, D_MODEL)), full((n * SUBLANES, LANES)), full((SUBLANES, n)), full(conv_state2.shape)],
        out_shape=[jax.ShapeDtypeStruct((n, D_MODEL), F32), jax.ShapeDtypeStruct((n * SUBLANES, LANES), F32),
                   jax.ShapeDtypeStruct((SUBLANES, n), F32), jax.ShapeDtypeStruct(conv_state2.shape, F32)],
        compiler_params=pltpu.CompilerParams(vmem_limit_bytes=VMEM_LIMIT),
        name="sample_tail",
    )(*ins)


def _plan_kernel(route_p_ref, route_s_ref, pos_ref, tinfo_ref, e_ref, pre_ref, *, tm, n_tok, max_tiles):
    n_p = route_p_ref.shape[1]
    nblk = n_tok // LANES
    e_ref[:, pl.ds(0, n_p)] = route_p_ref[...]
    e_ref[:, pl.ds(n_p, n_tok - n_p)] = route_s_ref[...]
    eidx = lax.broadcasted_iota(jnp.int32, (N_EXPERTS, LANES), 0).astype(F32)
    ti = lax.broadcasted_iota(jnp.int32, (LANES, LANES), 0)
    si = lax.broadcasted_iota(jnp.int32, (LANES, LANES), 1)
    before = (ti < si).astype(BF)

    def onehots(b):
        c0 = pl.multiple_of(b * LANES, LANES)
        e1 = e_ref[0:1, pl.ds(c0, LANES)]
        e2 = e_ref[1:2, pl.ds(c0, LANES)]
        return (e1 == eidx).astype(F32), (e2 == eidx).astype(F32), c0

    def count(b, carry):
        h1, h2, c0 = onehots(b)
        h = h1 + h2
        pre_ref[:, pl.ds(c0, LANES)] = _dot(h.astype(BF), before) + carry
        return carry + jnp.sum(h, axis=1, keepdims=True)

    counts = lax.fori_loop(0, nblk, count, jnp.zeros((N_EXPERTS, 1), F32))
    tiles_per = jnp.floor((counts + (tm - 1)) * (1.0 / tm))
    ei = lax.broadcasted_iota(jnp.int32, (N_EXPERTS, N_EXPERTS), 0)
    ej = lax.broadcasted_iota(jnp.int32, (N_EXPERTS, N_EXPERTS), 1)
    upto = (ej <= ei).astype(BF)
    tile_end = _dot(upto, jnp.broadcast_to(tiles_per, (N_EXPERTS, LANES)).astype(BF))[:, 0:1]
    starts = (tile_end - tiles_per) * tm

    def place(b, c):
        h1, h2, c0 = onehots(b)
        dest = pre_ref[:, pl.ds(c0, LANES)] + starts
        pos_ref[0:1, pl.ds(c0, LANES)] = jnp.sum(h1 * dest, axis=0, keepdims=True).astype(jnp.int32)
        pos_ref[1:2, pl.ds(c0, LANES)] = jnp.sum(h2 * dest, axis=0, keepdims=True).astype(jnp.int32)
        return c

    pos_ref[...] = jnp.zeros_like(pos_ref)
    lax.fori_loop(0, nblk, place, 0)

    tile = lax.broadcasted_iota(jnp.int32, (N_EXPERTS, max_tiles), 1).astype(F32)
    t_exp = jnp.sum((tile_end <= tile).astype(F32), axis=0, keepdims=True)
    t_exp = jnp.minimum(t_exp, N_EXPERTS - 1.0).astype(jnp.int32)
    n_used = jnp.broadcast_to(tile_end[N_EXPERTS - 1:N_EXPERTS, :], (1, max_tiles)).astype(jnp.int32)
    own = (lax.broadcasted_iota(jnp.int32, (N_EXPERTS, max_tiles), 0)
           == lax.broadcasted_iota(jnp.int32, (N_EXPERTS, max_tiles), 1)).astype(F32)
    last_tile = jnp.sum(own * (tile_end - 1.0), axis=0, keepdims=True).astype(jnp.int32)
    n_of = jnp.sum(own * tiles_per, axis=0, keepdims=True).astype(jnp.int32)
    row = lax.broadcasted_iota(jnp.int32, (SUBLANES, max_tiles), 0)
    tinfo_ref[...] = jnp.where(row == 0, t_exp, jnp.where(row == 1, n_used,
                                                           jnp.where(row == 2, last_tile, jnp.where(row == 3, n_of, 0))))


def _plan(route_p, route_s, tm, max_tiles):
    n_tok = route_p.shape[1] + route_s.shape[1]
    assert route_p.shape[1] % LANES == 0 and route_s.shape[1] % LANES == 0
    vm = pl.BlockSpec(memory_space=pltpu.VMEM)
    return pl.pallas_call(
        functools.partial(_plan_kernel, tm=tm, n_tok=n_tok, max_tiles=max_tiles),
        in_specs=[vm, vm],
        out_specs=[vm, vm],
        out_shape=[jax.ShapeDtypeStruct((SUBLANES, n_tok), jnp.int32),
                   jax.ShapeDtypeStruct((SUBLANES, max_tiles), jnp.int32)],
        scratch_shapes=[pltpu.VMEM((SUBLANES, n_tok), F32), pltpu.VMEM((N_EXPERTS, n_tok), F32)],
        compiler_params=pltpu.CompilerParams(vmem_limit_bytes=VMEM_LIMIT),
        name="route_plan",
    )(route_p, route_s)


def _scatter_rows(pos_ref, h_ref, xs_ref, sem, tb, tok0):
    i = pl.program_id(0)

    def copies(r):
        base = 2 * (tok0 + i * tb + r)
        tile = lambda ref, row: ref.at[pl.ds(pl.multiple_of(row * SUBLANES, SUBLANES), SUBLANES), :]
        return (pltpu.make_async_copy(tile(h_ref, r), tile(xs_ref, pos_ref[base]), sem),
                pltpu.make_async_copy(tile(h_ref, r), tile(xs_ref, pos_ref[base + 1]), sem))

    def issue(r, c):
        c0, c1 = copies(r)
        c0.start(priority=0)
        c1.start(priority=1)
        return c

    def drain(r, c):
        c0, c1 = copies(r)
        c0.wait()
        c1.wait()
        return c

    lax.fori_loop(0, tb, issue, 0, unroll=8)
    lax.fori_loop(0, tb, drain, 0, unroll=8)


def _dispatch_first_kernel(pos_ref, tinfo_ref, h_ref, xs_ref, zeros_ref, sem, zero_sem, *, tb, tm, max_tiles):
    @pl.when(pl.program_id(0) == 0)
    def _():
        zeros_ref[...] = jnp.zeros_like(zeros_ref)
        n_used = tinfo_ref[max_tiles]

        def fill(tile):
            rows = pl.ds(pl.multiple_of(tile * (tm * SUBLANES), tm * SUBLANES), tm * SUBLANES)
            return pltpu.make_async_copy(zeros_ref, xs_ref.at[rows, :], zero_sem)

        def each_tile(action):
            for e in range(N_EXPERTS):
                @pl.when(tinfo_ref[3 * max_tiles + e] > 0)
                def _():
                    action(fill(tinfo_ref[2 * max_tiles + e]))

            def unused(tile, c):
                action(fill(tile))
                return c

            lax.fori_loop(n_used, max_tiles, unused, 0)

        each_tile(lambda copy: copy.start())
        each_tile(lambda copy: copy.wait())

    _scatter_rows(pos_ref, h_ref, xs_ref, sem, tb, 0)


def _dispatch_more_kernel(pos_ref, h_ref, xs_in_ref, xs_ref, sem, *, tb, tok0):
    del xs_in_ref
    _scatter_rows(pos_ref, h_ref, xs_ref, sem, tb, tok0)


def _dispatch_first(pos_flat, tinfo_flat, hpk, tb, tm, max_tiles):
    n = hpk.shape[0] // SUBLANES
    grid_spec = pltpu.PrefetchScalarGridSpec(
        num_scalar_prefetch=2,
        grid=(n // tb,),
        in_specs=[pl.BlockSpec((tb * SUBLANES, LANES), lambda i, p, t: (i, 0))],
        out_specs=pl.BlockSpec(memory_space=pl.ANY),
        scratch_shapes=[pltpu.VMEM((tm * SUBLANES, LANES), F32), pltpu.SemaphoreType.DMA(()),
                        pltpu.SemaphoreType.DMA(())])
    return pl.pallas_call(
        functools.partial(_dispatch_first_kernel, tb=tb, tm=tm, max_tiles=max_tiles),
        grid_spec=grid_spec,
        out_shape=jax.ShapeDtypeStruct((max_tiles * tm * SUBLANES, LANES), F32),
        compiler_params=pltpu.CompilerParams(dimension_semantics=("arbitrary",), vmem_limit_bytes=VMEM_LIMIT),
        name="dispatch",
    )(pos_flat, tinfo_flat, hpk)


def _dispatch_more(pos_flat, hpk, xs, tb, tok0):
    n = hpk.shape[0] // SUBLANES
    grid_spec = pltpu.PrefetchScalarGridSpec(
        num_scalar_prefetch=1,
        grid=(n // tb,),
        in_specs=[pl.BlockSpec((tb * SUBLANES, LANES), lambda i, p: (i, 0)),
                  pl.BlockSpec(memory_space=pl.ANY)],
        out_specs=pl.BlockSpec(memory_space=pl.ANY),
        scratch_shapes=[pltpu.SemaphoreType.DMA(())])
    return pl.pallas_call(
        functools.partial(_dispatch_more_kernel, tb=tb, tok0=tok0),
        grid_spec=grid_spec,
        out_shape=jax.ShapeDtypeStruct(xs.shape, xs.dtype),
        input_output_aliases={2: 0},
        compiler_params=pltpu.CompilerParams(dimension_semantics=("arbitrary",), vmem_limit_bytes=VMEM_LIMIT),
        name="dispatch",
    )(pos_flat, hpk, xs)


def _store_row_tiles(ref, rows):
    n = rows.shape[0]
    for s in range(SUBLANES):
        ref[pl.ds(s, n, stride=SUBLANES), :] = rows[:, s * LANES:(s + 1) * LANES]


def _load_row_tiles(ref):
    n = ref.shape[0] // SUBLANES
    return jnp.concatenate([ref[pl.ds(s, n, stride=SUBLANES), :] for s in range(SUBLANES)], axis=1)


def _expert_kernel(te_ref, nt_ref, xs_ref, wg_ref, wu_ref, wd_ref, y_ref, wg_b, wu_b, wd_b):
    i = pl.program_id(0)

    @pl.when(i < nt_ref[0])
    def _():
        changed = jnp.logical_or(i == 0, te_ref[i] != te_ref[jnp.maximum(i - 1, 0)])

        @pl.when(changed)
        def _():
            wg_b[...] = wg_ref[0].astype(BF)
            wu_b[...] = wu_ref[0].astype(BF)
            wd_b[...] = wd_ref[0].astype(BF)

        xb = _load_row_tiles(xs_ref).astype(BF)
        gate = _dot(xb, wg_b[...])
        up = _dot(xb, wu_b[...])
        _store_row_tiles(y_ref, _dot((_silu(gate) * up).astype(BF), wd_b[...]))

    @pl.when(i >= nt_ref[0])
    def _():
        y_ref[...] = jnp.zeros_like(y_ref)


def _experts(xs, tile_expert, n_tiles, wg, wu, wd, tm, max_tiles):
    grid_spec = pltpu.PrefetchScalarGridSpec(
        num_scalar_prefetch=2,
        grid=(max_tiles,),
        in_specs=[pl.BlockSpec((tm * SUBLANES, LANES),
                               lambda i, te, nt: (jnp.minimum(i, nt[0] - 1), 0)),
                  pl.BlockSpec((1, D_MODEL, EXPERT_FF), lambda i, te, nt: (te[i], 0, 0)),
                  pl.BlockSpec((1, D_MODEL, EXPERT_FF), lambda i, te, nt: (te[i], 0, 0)),
                  pl.BlockSpec((1, EXPERT_FF, D_MODEL), lambda i, te, nt: (te[i], 0, 0))],
        out_specs=pl.BlockSpec((tm * SUBLANES, LANES), lambda i, te, nt: (i, 0)),
        scratch_shapes=[pltpu.VMEM((D_MODEL, EXPERT_FF), BF), pltpu.VMEM((D_MODEL, EXPERT_FF), BF),
                        pltpu.VMEM((EXPERT_FF, D_MODEL), BF)])
    return pl.pallas_call(
        _expert_kernel,
        grid_spec=grid_spec,
        out_shape=jax.ShapeDtypeStruct((max_tiles * tm * SUBLANES, LANES), F32),
        compiler_params=pltpu.CompilerParams(dimension_semantics=("arbitrary",), vmem_limit_bytes=VMEM_LIMIT),
        name="experts",
    )(tile_expert, n_tiles, xs, wg, wu, wd)


def _combine_kernel(pos_ref, x1_ref, route_ref, gfin_ref, ys_ref, y_ref, buf0, buf1, sems, *, tb, tok0):
    i = pl.program_id(0)
    n_steps = pl.num_programs(0)
    slot = i % 2

    def copies(step, half, r):
        base = 2 * (tok0 + step * tb + r)
        tile = lambda ref, row: ref.at[pl.ds(pl.multiple_of(row * SUBLANES, SUBLANES), SUBLANES), :]
        return (pltpu.make_async_copy(tile(ys_ref, pos_ref[base]), tile(buf0.at[half], r), sems.at[half]),
                pltpu.make_async_copy(tile(ys_ref, pos_ref[base + 1]), tile(buf1.at[half], r), sems.at[half]))

    def request(step, half):
        def body(r, c):
            c0, c1 = copies(step, half, r)
            c0.start(priority=0)
            c1.start(priority=1)
            return c

        lax.fori_loop(0, tb, body, 0, unroll=8)

    @pl.when(i == 0)
    def _():
        request(0, 0)

    @pl.when(i + 1 < n_steps)
    def _():
        request(i + 1, 1 - slot)

    def receive(r, c):
        c0, c1 = copies(i, slot, r)
        c0.wait()
        c1.wait()
        return c

    lax.fori_loop(0, tb, receive, 0, unroll=8)
    pad = jnp.zeros((LANES - SUBLANES, LANES), F32)
    cols = jnp.concatenate([jnp.concatenate([route_ref[:, j:j + LANES], pad], axis=0).T
                            for j in range(0, tb, LANES)], axis=0)
    out = x1_ref[...] + (cols[:, 2:3] * _load_row_tiles(buf0.at[slot]) + cols[:, 3:4] * _load_row_tiles(buf1.at[slot]))
    y_ref[...] = _rms(out, gfin_ref[...])


def _combine(pos_flat, x1, route, gfin, ys, tb, tok0):
    n = x1.shape[0]
    grid_spec = pltpu.PrefetchScalarGridSpec(
        num_scalar_prefetch=1,
        grid=(n // tb,),
        in_specs=[pl.BlockSpec((tb, D_MODEL), lambda i, p: (i, 0)),
                  pl.BlockSpec((SUBLANES, tb), lambda i, p: (0, i)),
                  pl.BlockSpec((1, D_MODEL), lambda i, p: (0, 0)),
                  pl.BlockSpec(memory_space=pl.ANY)],
        out_specs=pl.BlockSpec((tb, D_MODEL), lambda i, p: (i, 0)),
        scratch_shapes=[pltpu.VMEM((2, tb * SUBLANES, LANES), F32), pltpu.VMEM((2, tb * SUBLANES, LANES), F32),
                        pltpu.SemaphoreType.DMA((2,))])
    return pl.pallas_call(
        functools.partial(_combine_kernel, tb=tb, tok0=tok0),
        grid_spec=grid_spec,
        out_shape=jax.ShapeDtypeStruct((n, D_MODEL), F32),
        compiler_params=pltpu.CompilerParams(dimension_semantics=("arbitrary",), vmem_limit_bytes=VMEM_LIMIT),
        name="combine",
    )(pos_flat, x1, route, gfin, ys)


def kernel(x_prompt, x_sample, state_conv, state_hgrn, norm_mix_g, w_in, b_gates, conv_dw_w, conv_dw_b,
           conv_ln_g, conv_ln_b, w_conv_out, hgrn_lb_param, hgrn_norm_g, w_hgrn_out, w_out, norm_ffn_g,
           w_router_group, w_router_expert, w_expert_gate, w_expert_up, w_expert_down, norm_final_g):
    batch, seq, _ = x_prompt.shape
    dec_batch = x_sample.shape[0]
    assert x_sample.shape[1] == 1 and w_in.shape[0] == 1
    tb = min(256, seq)
    tm = 256
    bs = min(8, dec_batch)
    n_p = batch * seq
    n_all = n_p + dec_batch
    tbd = min(2048, n_p)
    tbc = min(1024, n_p)
    assert seq % tb == 0 and tb % CHUNK == 0 and dec_batch % bs == 0 and n_p % tbd == 0 and n_p % tbc == 0

    w_router = jnp.concatenate(
        [w_router_group[0].T, w_router_expert[0].T,
         jnp.zeros((ROUTE_ROWS - N_GROUPS - N_EXPERTS, D_MODEL), F32)], axis=0).astype(BF)
    wts = dict(gmix=norm_mix_g, w_in=w_in[0].astype(BF), b_gates=b_gates, conv_w=conv_dw_w[0], conv_b=conv_dw_b,
               ln_g=conv_ln_g, ln_b=conv_ln_b, w_conv_out=w_conv_out[0].astype(BF), lb_param=hgrn_lb_param,
               hn_g=hgrn_norm_g, w_hgrn_out=w_hgrn_out[0].astype(BF), w_out=w_out[0].astype(BF),
               gffn=norm_ffn_g, w_router=w_router)

    x1_p, hpk_p, route_p, nconv_p, nhgrn_p = _prompt_mixer(x_prompt.reshape(n_p, D_MODEL), batch, seq, tb, wts)

    xs_tok = x_sample.reshape(dec_batch, D_MODEL)
    proj_s = _sample_proj(xs_tok, wts)
    nhgrn_s, o_s = _sample_state(proj_s, state_hgrn[0], wts, bs)
    conv2 = state_conv[0].reshape(dec_batch, (CONV_WIDTH - 1) * CONV_DIM)
    x1_s, hpk_s, route_s, nconv_s = _sample_tail(xs_tok, proj_s, o_s, conv2, wts)

    max_tiles = -(-((2 * n_all) // tm + N_EXPERTS) // SUBLANES) * SUBLANES
    pos, tinfo = _plan(route_p, route_s, tm, max_tiles)
    pos_flat = pos[0:2].T.reshape(-1)
    tile_expert, n_tiles = tinfo[0], tinfo[1, 0:1]

    xs = _dispatch_first(pos_flat, tinfo.reshape(-1), hpk_p, tbd, tm, max_tiles)
    xs = _dispatch_more(pos_flat, hpk_s, xs, dec_batch, n_p)
    ys = _experts(xs, tile_expert, n_tiles, w_expert_gate[0], w_expert_up[0], w_expert_down[0], tm, max_tiles)

    gfin = norm_final_g.reshape(1, D_MODEL)
    y_p = _combine(pos_flat, x1_p, route_p, gfin, ys, tbc, 0)
    y_s = _combine(pos_flat, x1_s, route_s, gfin, ys, dec_batch, n_p)

    return (y_p.reshape(batch, seq, D_MODEL), y_s.reshape(dec_batch, 1, D_MODEL),
            nconv_p[None, :, HIST_PAD:, :], nhgrn_p[None],
            nconv_s.reshape(1, dec_batch, CONV_WIDTH - 1, CONV_DIM), nhgrn_s[None])
```

```python
import functools

import jax
import jax.numpy as jnp
from jax import lax
from jax.experimental import pallas as pl
from jax.experimental.pallas import tpu as pltpu

D_MODEL = 1024
CONV_DIM = D_MODEL // 2
CONV_WIDTH = 31
HG_HEADS = 8
HG_DK = 128
HG_DV = 128
HG_DIM = HG_HEADS * HG_DK
N_GROUPS = 4
EXPERTS_PER_GROUP = 8
N_EXPERTS = N_GROUPS * EXPERTS_PER_GROUP
EXPERT_FF = D_MODEL // 2
EPS = 1e-6
IN_COLS = 2 * CONV_DIM + 4 * HG_DIM + 2 * D_MODEL

OFF_CV, OFF_CG = 0, CONV_DIM
OFF_Q = 2 * CONV_DIM
OFF_F = OFF_Q + HG_DIM
OFF_V = OFF_F + HG_DIM
OFF_OG = OFF_V + HG_DIM
OFF_GA = OFF_OG + HG_DIM
OFF_GB = OFF_GA + D_MODEL

SUBLANES = 8
LANES = 128
HIST_ROWS = 32
HIST_PAD = HIST_ROWS - (CONV_WIDTH - 1)
CHUNK = 64
NBLK = CHUNK // SUBLANES
CONV_ROWS = 32
PROJ_COLS = 256
ROUTE_ROWS = 128
VMEM_LIMIT = 56 * 1024 * 1024

BF = jnp.bfloat16
F32 = jnp.float32


def _dot(a, b):
    return jnp.dot(a, b, preferred_element_type=F32)


def _dot_nt(a, b):
    return lax.dot_general(a, b, (((1,), (1,)), ((), ())), preferred_element_type=F32)


def _dot_tn(a, b, precision=None):
    return lax.dot_general(a, b, (((0,), (0,)), ((), ())), preferred_element_type=F32,
                           precision=precision)


NEG_LOG2E = -1.4426950408889634


def _sigmoid(x):
    return 1.0 / (1.0 + jnp.exp2(x * NEG_LOG2E))


def _silu(x):
    return x * _sigmoid(x)


def _rms(xf, g):
    return xf * lax.rsqrt(jnp.mean(xf * xf, axis=-1, keepdims=True) + EPS) * g


def _lower_bound(lb_param):
    m = jnp.max(lb_param, axis=0, keepdims=True)
    e = jnp.exp(lb_param - m)
    return e[0:1] / jnp.sum(e, axis=0, keepdims=True)


def _conv_post(c, conv_b, ln_g, ln_b):
    c = c + conv_b
    mu = jnp.mean(c, axis=-1, keepdims=True)
    d = c - mu
    var = jnp.mean(d * d, axis=-1, keepdims=True)
    return _silu(d * lax.rsqrt(var + EPS) * ln_g + ln_b)


def _head_norm(o, g):
    return o * lax.rsqrt(jnp.mean(o * o, axis=-1, keepdims=True) + EPS) * g


def _route(logits_t):
    used = -(-(N_GROUPS + N_EXPERTS) // SUBLANES) * SUBLANES
    logits = logits_t[0:used, :]
    row = lax.broadcasted_iota(jnp.int32, logits.shape, 0).astype(F32)
    big = jnp.float32(1 << 20)
    neg = jnp.float32(-jnp.inf)
    gmask = row < N_GROUPS
    lg = jnp.where(gmask, logits, neg)
    gmax = jnp.max(lg, axis=0, keepdims=True)
    gsum = jnp.sum(jnp.where(gmask, jnp.exp(lg - gmax), 0.0), axis=0, keepdims=True)
    gval = 1.0 / gsum
    gidx = jnp.min(jnp.where(lg == gmax, row, big), axis=0, keepdims=True)
    lo = N_GROUPS + EXPERTS_PER_GROUP * gidx
    emask = (row >= lo) & (row < lo + EXPERTS_PER_GROUP)
    el = jnp.where(emask, logits, neg)
    m1 = jnp.max(el, axis=0, keepdims=True)
    i1 = jnp.min(jnp.where(el == m1, row, big), axis=0, keepdims=True)
    el2 = jnp.where(row == i1, neg, el)
    m2 = jnp.max(el2, axis=0, keepdims=True)
    i2 = jnp.min(jnp.where(el2 == m2, row, big), axis=0, keepdims=True)
    r = jnp.exp(m2 - m1)
    w1 = gval / (1.0 + r)
    w2 = gval * r / (1.0 + r)
    out_row = lax.broadcasted_iota(jnp.int32, (SUBLANES, logits.shape[1]), 0)
    return jnp.where(out_row == 0, i1 - N_GROUPS,
                     jnp.where(out_row == 1, i2 - N_GROUPS, jnp.where(out_row == 2, w1, jnp.where(out_row == 3, w2, 0.0))))


LEVELS = (NBLK // 2, NBLK // 4, NBLK // 8)
ROW_TOT, ROW_QE, ROW_KD, ROW_LEVEL = 0, NBLK, 2 * NBLK, 3 * NBLK
TABLE_ROWS = ROW_LEVEL + 2 * NBLK * len(LEVELS)


def _row_bcast(ref, row):
    return jnp.broadcast_to(ref[pl.ds(row, 1), :], (SUBLANES, LANES))


def _per_block(ref, base):
    return jnp.concatenate([_row_bcast(ref, base + j) for j in range(NBLK)], axis=0)


def _block_rows(ref, c):
    return jnp.concatenate([_row_bcast(ref, j * SUBLANES + c) for j in range(NBLK)], axis=0)


def _sparse_tile(x, ref, base, blocks):
    groups = []
    for g in range(0, NBLK, 2):
        if g not in blocks and g + 1 not in blocks:
            groups.append(jnp.zeros((2 * SUBLANES, LANES), BF))
            continue
        halves = [x[j * SUBLANES:(j + 1) * SUBLANES, :] * _row_bcast(ref, base + j) if j in blocks
                  else jnp.zeros((SUBLANES, LANES), F32) for j in (g, g + 1)]
        groups.append(jnp.concatenate(halves, axis=0).astype(BF))
    return jnp.concatenate(groups, axis=0)


def _chunk_cumsum(l2, bl_ref):
    rin = lax.broadcasted_iota(jnp.int32, (CHUNK, LANES), 0) % SUBLANES
    bl = l2
    for s in (1, 2, 4):
        bl = bl + jnp.where(rin >= s, pltpu.roll(bl, s, axis=0), 0.0)
    bl_ref[...] = bl
    return bl


def _chunk_table(bl_ref, tab_ref):
    tot = bl_ref[pl.ds(SUBLANES - 1, NBLK, stride=SUBLANES), :]
    brow = lax.broadcasted_iota(jnp.int32, (NBLK, LANES), 0)
    rb = tot
    for s in (1, 2, 4):
        rb = rb + jnp.where(brow >= s, pltpu.roll(rb, s, axis=0), 0.0)
    rb_prev = rb - tot
    total = rb[NBLK - 1:NBLK, :]

    tab_ref[pl.ds(ROW_TOT, NBLK), :] = tot
    tab_ref[pl.ds(ROW_QE, NBLK), :] = jnp.exp2(rb_prev)
    tab_ref[pl.ds(ROW_KD, NBLK), :] = jnp.exp2(total - rb)
    for lv, cb in enumerate(LEVELS):
        edges = [(j // (2 * cb)) * (2 * cb) + cb - 1 for j in range(NBLK)]
        rb_edge = jnp.concatenate([rb[e:e + 1, :] for e in edges], axis=0)
        base = ROW_LEVEL + 2 * NBLK * lv
        tab_ref[pl.ds(base, NBLK), :] = jnp.exp2(jnp.minimum(rb_prev - rb_edge, 0.0))
        tab_ref[pl.ds(base + NBLK, NBLK), :] = jnp.exp2(jnp.minimum(rb_edge - rb, 0.0))
    return total


def _chunk_scaled(q, k, bl, tab_ref):
    return q * jnp.exp2(bl), k * jnp.exp2(_per_block(tab_ref, ROW_TOT) - bl)


def _chunk_inter_operands(qp, kp, tab_ref):
    return (qp * _per_block(tab_ref, ROW_QE)).astype(BF), (kp * _per_block(tab_ref, ROW_KD)).astype(BF)


def _chunk_pair_operands(qp, kp, tab_ref):
    q_tiles, k_tiles = [], []
    for lv, cb in enumerate(LEVELS):
        base = ROW_LEVEL + 2 * NBLK * lv
        for p0 in range(0, NBLK, 2 * cb):
            q_tiles.append(_sparse_tile(qp, tab_ref, base, range(p0 + cb, p0 + 2 * cb)))
            k_tiles.append(_sparse_tile(kp, tab_ref, base + NBLK, range(p0, p0 + cb)))
    return jnp.concatenate(q_tiles, axis=1), jnp.concatenate(k_tiles, axis=1)


def _chunk_block_operands(q, k, bl, row_masks, bl_ref):
    k_b = k.astype(BF)
    lhs = [(q * jnp.exp2(jnp.minimum(bl - _block_rows(bl_ref, c), 0.0))).astype(BF) for c in range(SUBLANES)]
    rhs = [k_b * row_masks[c] for c in range(SUBLANES)]
    return jnp.concatenate(lhs, axis=1), jnp.concatenate(rhs, axis=1)


def _chunk_products(qe, kdec, v_b, pair_ops, block_ops, st):
    return (_dot_nt(qe, st.astype(BF)), _dot_tn(v_b, kdec), _dot_nt(*pair_ops), _dot_nt(*block_ops))


def _chunk_output(o_inter, off_diag, diag, v_b):
    ti = lax.broadcasted_iota(jnp.int32, (CHUNK, CHUNK), 0)
    si = lax.broadcasted_iota(jnp.int32, (CHUNK, CHUNK), 1)
    scores = jnp.where((ti // SUBLANES == si // SUBLANES) & (si <= ti), diag, off_diag)
    return o_inter + _dot(scores.astype(BF), v_b)


def _mixer_tail(x, m, w_out, g_ffn, w_router_t):
    x1 = x + _dot(m.astype(BF), w_out)
    h2 = _rms(x1, g_ffn).astype(BF)
    route = _route(_dot_nt(w_router_t, h2))
    return x1, h2.astype(F32), route


def _prompt_mixer_kernel(x_ref, gmix_ref, win_ref, bg_ref, cw_ref, cb_ref, lng_ref, lnb_ref, wco_ref,
                         lbp_ref, hng_ref, who_ref, wout_ref, gffn_ref, wr_ref, masks_ref,
                         x1_ref, h2_ref, route_ref, nconv_ref, nhgrn_ref,
                         hist_ref, phase_ref, conv_ref, st_ref, q_s, k_s, v_s, lf_s, o_s, og_s, ga_s, gb_s, *head_scr, tb):
    t = pl.program_id(1)

    @pl.when(t == 0)
    def _():
        hist_ref[pl.ds(0, HIST_ROWS), :] = jnp.zeros((HIST_ROWS, CONV_DIM), F32)
        st_ref[...] = jnp.zeros_like(st_ref)

    x = x_ref[...]
    h = _rms(x, gmix_ref[...]).astype(BF)

    cv = _dot(h, win_ref[:, OFF_CV:OFF_CV + CONV_DIM])
    cg = _dot(h, win_ref[:, OFF_CG:OFF_CG + CONV_DIM])
    hist_ref[pl.ds(HIST_ROWS, tb), :] = cv * _sigmoid(cg)

    lb = _lower_bound(lbp_ref[...])

    def project(seg, c0):
        z = _dot(h, win_ref[:, seg + c0:seg + c0 + PROJ_COLS])
        cols = slice(c0, c0 + PROJ_COLS)
        if seg == OFF_Q:
            q_s[:, cols] = _silu(z) * (HG_DK ** -0.5)
        elif seg == OFF_F:
            fg = lb[:, cols] + (1.0 - lb[:, cols]) * _sigmoid(z)
            lf_s[:, cols] = jnp.log2(fg)
            k_s[:, cols] = 1.0 - fg
        elif seg == OFF_V:
            v_s[:, cols] = z
        elif seg == OFF_OG:
            og_s[:, cols] = _silu(z)
        elif seg == OFF_GA:
            ga_s[:, cols] = _sigmoid(z + bg_ref[:, c0:c0 + PROJ_COLS])
        else:
            gb_s[:, cols] = _sigmoid(z + bg_ref[:, D_MODEL + c0:D_MODEL + c0 + PROJ_COLS])

    proj_jobs = [functools.partial(project, seg, c0)
                 for seg in (OFF_Q, OFF_F, OFF_V, OFF_OG, OFF_GA, OFF_GB) for c0 in range(0, HG_DIM, PROJ_COLS)]

    span = tb + SUBLANES * (-(-CONV_WIDTH // SUBLANES) - 1)

    def phase_copy(r):
        n = min(span, HIST_ROWS + tb - HIST_PAD - r)
        phase_ref[r, pl.ds(0, n), :] = hist_ref[pl.ds(HIST_PAD + r, n), :]

    def conv_rows(r0):
        acc = jnp.zeros((CONV_ROWS, CONV_DIM), F32)
        for j in range(CONV_WIDTH):
            w_j = jnp.broadcast_to(cw_ref[pl.ds(j, 1), :], (SUBLANES, CONV_DIM))
            acc = acc + (phase_ref[j % SUBLANES, pl.ds(r0 + j - j % SUBLANES, CONV_ROWS), :]
                         * jnp.concatenate([w_j] * (CONV_ROWS // SUBLANES), axis=0))
        conv_ref[pl.ds(r0, CONV_ROWS), :] = acc

    conv_jobs = ([functools.partial(phase_copy, r) for r in range(SUBLANES)]
                 + [functools.partial(conv_rows, r0) for r0 in range(0, tb, CONV_ROWS)])
    per_conv_job = -(-len(proj_jobs) // len(conv_jobs))
    for i, job in enumerate(conv_jobs):
        job()
        for pj in proj_jobs[i * per_conv_job:(i + 1) * per_conv_job]:
            pj()
    for pj in proj_jobs[len(conv_jobs) * per_conv_job:]:
        pj()

    tail = hist_ref[pl.ds(tb, HIST_ROWS), :]
    hist_ref[pl.ds(0, HIST_ROWS), :] = tail
    nconv_ref[0] = tail
    c = _conv_post(conv_ref[...], cb_ref[...], lng_ref[...], lnb_ref[...])
    y_a = _dot(c.astype(BF), wco_ref[...])

    row_masks = masks_ref[...]
    cols = [slice(hd * HG_DK, (hd + 1) * HG_DK) for hd in range(HG_HEADS)]
    bl_refs, tab_refs = head_scr[:HG_HEADS], head_scr[HG_HEADS:]
    heads = range(HG_HEADS)
    for r0 in range(0, tb, CHUNK):
        rows = pl.ds(r0, CHUNK)
        q = [q_s[rows, cols[hd]] for hd in heads]
        k = [k_s[rows, cols[hd]] for hd in heads]
        bl = [_chunk_cumsum(lf_s[rows, cols[hd]], bl_refs[hd]) for hd in heads]
        total = [_chunk_table(bl_refs[hd], tab_refs[hd]) for hd in heads]
        scaled = [_chunk_scaled(q[hd], k[hd], bl[hd], tab_refs[hd]) for hd in heads]
        inter = [_chunk_inter_operands(*scaled[hd], tab_refs[hd]) for hd in heads]
        pair_ops = [_chunk_pair_operands(*scaled[hd], tab_refs[hd]) for hd in heads]
        block_ops = [_chunk_block_operands(q[hd], k[hd], bl[hd], row_masks, bl_refs[hd]) for hd in heads]
        v_b = [v_s[rows, cols[hd]].astype(BF) for hd in heads]
        prods = [_chunk_products(*inter[hd], v_b[hd], pair_ops[hd], block_ops[hd], st_ref[hd]) for hd in heads]
        for hd in heads:
            o_inter, update, off_diag, diag = prods[hd]
            st_ref[hd] = st_ref[hd] * jnp.exp2(total[hd]) + update
            o = _chunk_output(o_inter, off_diag, diag, v_b[hd])
            o_s[rows, cols[hd]] = _head_norm(o, hng_ref[:, cols[hd]])
    for hd in range(HG_HEADS):
        nhgrn_ref[0, hd] = st_ref[hd].T

    y_b = _dot((o_s[...] * og_s[...]).astype(BF), who_ref[...])
    x1, h2, route = _mixer_tail(x, ga_s[...] * y_a + gb_s[...] * y_b, wout_ref[...], gffn_ref[...], wr_ref[...])
    x1_ref[...] = x1
    _store_row_tiles(h2_ref, h2)
    route_ref[...] = route


def _const_spec(shape):
    nd = len(shape)
    return pl.BlockSpec(shape, lambda *_: (0,) * nd, pipeline_mode=pl.Buffered(1))


def _row_masks():
    c = jnp.arange(SUBLANES)[:, None, None]
    r = jnp.arange(CHUNK)[None, :, None] % SUBLANES
    return jnp.broadcast_to(r == c, (SUBLANES, CHUNK, LANES)).astype(BF)


def _prompt_mixer(x2, batch, seq, tb, wts):
    nt = seq // tb
    n = batch * seq
    row_spec = lambda w: pl.BlockSpec((tb, w), lambda b, t: (b * nt + t, 0))
    consts = [wts['gmix'], wts['w_in'], wts['b_gates'], wts['conv_w'], wts['conv_b'], wts['ln_g'], wts['ln_b'],
              wts['w_conv_out'], wts['lb_param'], wts['hn_g'], wts['w_hgrn_out'], wts['w_out'], wts['gffn'],
              wts['w_router'], _row_masks()]
    span = tb + SUBLANES * (-(-CONV_WIDTH // SUBLANES) - 1)
    return pl.pallas_call(
        functools.partial(_prompt_mixer_kernel, tb=tb),
        grid=(batch, nt),
        in_specs=[row_spec(D_MODEL)] + [_const_spec(c.shape) for c in consts],
        out_specs=[row_spec(D_MODEL), pl.BlockSpec((tb * SUBLANES, LANES), lambda b, t: (b * nt + t, 0)),
                   pl.BlockSpec((SUBLANES, tb), lambda b, t: (0, b * nt + t)),
                   pl.BlockSpec((1, HIST_ROWS, CONV_DIM), lambda b, t: (b, 0, 0)),
                   pl.BlockSpec((1, HG_HEADS, HG_DK, HG_DV), lambda b, t: (b, 0, 0, 0))],
        out_shape=[jax.ShapeDtypeStruct((n, D_MODEL), F32), jax.ShapeDtypeStruct((n * SUBLANES, LANES), F32),
                   jax.ShapeDtypeStruct((SUBLANES, n), F32),
                   jax.ShapeDtypeStruct((batch, HIST_ROWS, CONV_DIM), F32),
                   jax.ShapeDtypeStruct((batch, HG_HEADS, HG_DK, HG_DV), F32)],
        scratch_shapes=[pltpu.VMEM((HIST_ROWS + tb, CONV_DIM), F32),
                        pltpu.VMEM((SUBLANES, span, CONV_DIM), F32),
                        pltpu.VMEM((tb, CONV_DIM), F32),
                        pltpu.VMEM((HG_HEADS, HG_DV, HG_DK), F32)]
                       + [pltpu.VMEM((tb, HG_DIM), F32)] * 8
                       + [pltpu.VMEM((CHUNK, LANES), F32)] * HG_HEADS
                       + [pltpu.VMEM((TABLE_ROWS, LANES), F32)] * HG_HEADS,
        compiler_params=pltpu.CompilerParams(dimension_semantics=("arbitrary", "arbitrary"),
                                             vmem_limit_bytes=VMEM_LIMIT),
        name="prompt_mixer",
    )(x2, *consts)


def _sample_proj_kernel(x_ref, gmix_ref, win_ref, proj_ref):
    h = _rms(x_ref[...], gmix_ref[...]).astype(BF)
    proj_ref[...] = _dot(h, win_ref[...])


def _sample_proj(xs, wts):
    n = xs.shape[0]
    return pl.pallas_call(
        _sample_proj_kernel,
        grid=(1,),
        in_specs=[_const_spec(xs.shape), _const_spec(wts['gmix'].shape), _const_spec(wts['w_in'].shape)],
        out_specs=pl.BlockSpec((n, IN_COLS), lambda i: (0, 0)),
        out_shape=jax.ShapeDtypeStruct((n, IN_COLS), F32),
        compiler_params=pltpu.CompilerParams(vmem_limit_bytes=VMEM_LIMIT),
        name="sample_proj",
    )(xs, wts['gmix'], wts['w_in'])


def _split3(x):
    hi = x.astype(BF).astype(F32)
    mid = (x - hi).astype(BF).astype(F32)
    lo = ((x - hi) - mid).astype(BF).astype(F32)
    return hi, mid, lo


def _sample_state_kernel(q_ref, f_ref, v_ref, lbp_ref, hng_ref, lane_masks_ref, s_ref, snew_ref, o_ref, *, bs):
    assert bs == SUBLANES
    lb = _lower_bound(lbp_ref[...])
    qf = _silu(q_ref[...]) * (HG_DK ** -0.5)
    fg = lb + (1.0 - lb) * _sigmoid(f_ref[...])
    kf = 1.0 - fg
    v = v_ref[...]
    ones = jnp.ones((bs, HG_DV), F32)
    zeros = jnp.zeros((bs, HG_DV), F32)
    n_groups = 9
    pad = jnp.zeros((LANES - n_groups * bs, LANES), F32)
    for hd in range(HG_HEADS):
        cs = slice(hd * HG_DK, (hd + 1) * HG_DK)
        f3, k3, q3, v3 = _split3(fg[:, cs]), _split3(kf[:, cs]), _split3(qf[:, cs]), _split3(v[:, cs])
        left = jnp.concatenate([f3[0], f3[1], f3[2], k3[0], k3[1], k3[0], q3[0], q3[1], q3[2], pad], axis=0)
        left_t = left.T.astype(BF)
        right = jnp.concatenate(
            [jnp.concatenate(blk, axis=1) for blk in
             [(ones, zeros, zeros)] * 3 + [(zeros, v3[0], zeros), (zeros, v3[0], zeros), (zeros, v3[1], zeros)]
             + [(zeros, zeros, ones)] * 3] + [jnp.concatenate((pad, pad, pad), axis=1)], axis=0).astype(BF)
        o_rows = []
        for r in range(bs):
            prod = _dot(left_t * lane_masks_ref[r], right)
            s_new = prod[:, :HG_DV] * s_ref[r, hd] + prod[:, HG_DV:2 * HG_DV]
            snew_ref[r, hd] = s_new
            o_rows.append(jnp.sum(prod[:, 2 * HG_DV:] * s_new, axis=0, keepdims=True))
        o_ref[:, cs] = _head_norm(jnp.concatenate(o_rows, axis=0), hng_ref[:, cs])


def _lane_masks():
    r = jnp.arange(SUBLANES)[:, None, None]
    lane = jnp.arange(LANES)[None, None, :] % SUBLANES
    return jnp.broadcast_to(lane == r, (SUBLANES, LANES, LANES)).astype(BF)


def _sample_state(proj, state, wts, bs):
    n = proj.shape[0]
    col_spec = lambda off: pl.BlockSpec((bs, HG_DIM), lambda i, off=off: (i, off // HG_DIM))
    st_spec = pl.BlockSpec((bs, HG_HEADS, HG_DK, HG_DV), lambda i: (i, 0, 0, 0))
    return pl.pallas_call(
        functools.partial(_sample_state_kernel, bs=bs),
        grid=(n // bs,),
        in_specs=[col_spec(OFF_Q), col_spec(OFF_F), col_spec(OFF_V),
                  _const_spec(wts['lb_param'].shape), _const_spec(wts['hn_g'].shape),
                  _const_spec((SUBLANES, LANES, LANES)), st_spec],
        out_specs=[st_spec, pl.BlockSpec((bs, HG_DIM), lambda i: (i, 0))],
        out_shape=[jax.ShapeDtypeStruct(state.shape, F32), jax.ShapeDtypeStruct((n, HG_DIM), F32)],
        compiler_params=pltpu.CompilerParams(dimension_semantics=("arbitrary",), vmem_limit_bytes=VMEM_LIMIT),
        name="sample_state",
    )(proj, proj, proj, wts['lb_param'], wts['hn_g'], _lane_masks(), state)


def _sample_tail_kernel(x_ref, proj_ref, o_ref, cst_ref, bg_ref, cw_ref, cb_ref, lng_ref, lnb_ref, wco_ref,
                        who_ref, wout_ref, gffn_ref, wr_ref, x1_ref, h2_ref, route_ref, nconv_ref):
    keep = (CONV_WIDTH - 2) * CONV_DIM
    u = proj_ref[:, OFF_CV:OFF_CV + CONV_DIM] * _sigmoid(proj_ref[:, OFF_CG:OFF_CG + CONV_DIM])
    acc = u * cw_ref[pl.ds(CONV_WIDTH - 1, 1), :]
    for j in range(CONV_WIDTH - 1):
        acc = acc + cst_ref[:, j * CONV_DIM:(j + 1) * CONV_DIM] * cw_ref[pl.ds(j, 1), :]
    nconv_ref[:, :keep] = cst_ref[:, CONV_DIM:]
    nconv_ref[:, keep:] = u
    c = _conv_post(acc, cb_ref[...], lng_ref[...], lnb_ref[...])
    y_a = _dot(c.astype(BF), wco_ref[...])
    og = proj_ref[:, OFF_OG:OFF_OG + HG_DIM]
    y_b = _dot((o_ref[...] * _silu(og)).astype(BF), who_ref[...])
    m = (_sigmoid(proj_ref[:, OFF_GA:OFF_GA + D_MODEL] + bg_ref[:, :D_MODEL]) * y_a
         + _sigmoid(proj_ref[:, OFF_GB:OFF_GB + D_MODEL] + bg_ref[:, D_MODEL:]) * y_b)
    x1, h2, route = _mixer_tail(x_ref[...], m, wout_ref[...], gffn_ref[...], wr_ref[...])
    x1_ref[...] = x1
    _store_row_tiles(h2_ref, h2)
    route_ref[...] = route


def _sample_tail(xs, proj, o, conv_state2, wts):
    n = xs.shape[0]
    ins = [xs, proj, o, conv_state2, wts['b_gates'], wts['conv_w'], wts['conv_b'], wts['ln_g'], wts['ln_b'],
           wts['w_conv_out'], wts['w_hgrn_out'], wts['w_out'], wts['gffn'], wts['w_router']]
    full = lambda shape: pl.BlockSpec(shape, lambda i: (0,) * len(shape))
    return pl.pallas_call(
        _sample_tail_kernel,
        grid=(1,),
        in_specs=[_const_spec(a.shape) for a in ins],
        out_specs=[full((n, D_MODEL)), full((n * SUBLANES, LANES)), full((SUBLANES, n)), full(conv_state2.shape)],
        out_shape=[jax.ShapeDtypeStruct((n, D_MODEL), F32), jax.ShapeDtypeStruct((n * SUBLANES, LANES), F32),
                   jax.ShapeDtypeStruct((SUBLANES, n), F32), jax.ShapeDtypeStruct(conv_state2.shape, F32)],
        compiler_params=pltpu.CompilerParams(vmem_limit_bytes=VMEM_LIMIT),
        name="sample_tail",
    )(*ins)


def _plan_kernel(route_p_ref, route_s_ref, pos_ref, tinfo_ref, e_ref, pre_ref, *, tm, n_tok, max_tiles):
    n_p = route_p_ref.shape[1]
    nblk = n_tok // LANES
    e_ref[:, pl.ds(0, n_p)] = route_p_ref[...]
    e_ref[:, pl.ds(n_p, n_tok - n_p)] = route_s_ref[...]
    eidx = lax.broadcasted_iota(jnp.int32, (N_EXPERTS, LANES), 0).astype(F32)
    ti = lax.broadcasted_iota(jnp.int32, (LANES, LANES), 0)
    si = lax.broadcasted_iota(jnp.int32, (LANES, LANES), 1)
    before = (ti < si).astype(BF)

    def onehots(b):
        c0 = pl.multiple_of(b * LANES, LANES)
        e1 = e_ref[0:1, pl.ds(c0, LANES)]
        e2 = e_ref[1:2, pl.ds(c0, LANES)]
        return (e1 == eidx).astype(F32), (e2 == eidx).astype(F32), c0

    def count(b, carry):
        h1, h2, c0 = onehots(b)
        h = h1 + h2
        pre_ref[:, pl.ds(c0, LANES)] = _dot(h.astype(BF), before) + carry
        return carry + jnp.sum(h, axis=1, keepdims=True)

    counts = lax.fori_loop(0, nblk, count, jnp.zeros((N_EXPERTS, 1), F32))
    tiles_per = jnp.floor((counts + (tm - 1)) * (1.0 / tm))
    ei = lax.broadcasted_iota(jnp.int32, (N_EXPERTS, N_EXPERTS), 0)
    ej = lax.broadcasted_iota(jnp.int32, (N_EXPERTS, N_EXPERTS), 1)
    upto = (ej <= ei).astype(BF)
    tile_end = _dot(upto, jnp.broadcast_to(tiles_per, (N_EXPERTS, LANES)).astype(BF))[:, 0:1]
    starts = (tile_end - tiles_per) * tm

    def place(b, c):
        h1, h2, c0 = onehots(b)
        dest = pre_ref[:, pl.ds(c0, LANES)] + starts
        pos_ref[0:1, pl.ds(c0, LANES)] = jnp.sum(h1 * dest, axis=0, keepdims=True).astype(jnp.int32)
        pos_ref[1:2, pl.ds(c0, LANES)] = jnp.sum(h2 * dest, axis=0, keepdims=True).astype(jnp.int32)
        return c

    pos_ref[...] = jnp.zeros_like(pos_ref)
    lax.fori_loop(0, nblk, place, 0)

    tile = lax.broadcasted_iota(jnp.int32, (N_EXPERTS, max_tiles), 1).astype(F32)
    t_exp = jnp.sum((tile_end <= tile).astype(F32), axis=0, keepdims=True)
    t_exp = jnp.minimum(t_exp, N_EXPERTS - 1.0).astype(jnp.int32)
    n_used = jnp.broadcast_to(tile_end[N_EXPERTS - 1:N_EXPERTS, :], (1, max_tiles)).astype(jnp.int32)
    own = (lax.broadcasted_iota(jnp.int32, (N_EXPERTS, max_tiles), 0)
           == lax.broadcasted_iota(jnp.int32, (N_EXPERTS, max_tiles), 1)).astype(F32)
    last_tile = jnp.sum(own * (tile_end - 1.0), axis=0, keepdims=True).astype(jnp.int32)
    n_of = jnp.sum(own * tiles_per, axis=0, keepdims=True).astype(jnp.int32)
    row = lax.broadcasted_iota(jnp.int32, (SUBLANES, max_tiles), 0)
    tinfo_ref[...] = jnp.where(row == 0, t_exp, jnp.where(row == 1, n_used,
                                                           jnp.where(row == 2, last_tile, jnp.where(row == 3, n_of, 0))))


def _plan(route_p, route_s, tm, max_tiles):
    n_tok = route_p.shape[1] + route_s.shape[1]
    assert route_p.shape[1] % LANES == 0 and route_s.shape[1] % LANES == 0
    vm = pl.BlockSpec(memory_space=pltpu.VMEM)
    return pl.pallas_call(
        functools.partial(_plan_kernel, tm=tm, n_tok=n_tok, max_tiles=max_tiles),
        in_specs=[vm, vm],
        out_specs=[vm, vm],
        out_shape=[jax.ShapeDtypeStruct((SUBLANES, n_tok), jnp.int32),
                   jax.ShapeDtypeStruct((SUBLANES, max_tiles), jnp.int32)],
        scratch_shapes=[pltpu.VMEM((SUBLANES, n_tok), F32), pltpu.VMEM((N_EXPERTS, n_tok), F32)],
        compiler_params=pltpu.CompilerParams(vmem_limit_bytes=VMEM_LIMIT),
        name="route_plan",
    )(route_p, route_s)


def _scatter_rows(pos_ref, h_ref, xs_ref, sem, tb, tok0):
    i = pl.program_id(0)

    def copies(r):
        base = 2 * (tok0 + i * tb + r)
        tile = lambda ref, row: ref.at[pl.ds(pl.multiple_of(row * SUBLANES, SUBLANES), SUBLANES), :]
        return (pltpu.make_async_copy(tile(h_ref, r), tile(xs_ref, pos_ref[base]), sem),
                pltpu.make_async_copy(tile(h_ref, r), tile(xs_ref, pos_ref[base + 1]), sem))

    def issue(r, c):
        c0, c1 = copies(r)
        c0.start(priority=0)
        c1.start(priority=1)
        return c

    def drain(r, c):
        c0, c1 = copies(r)
        c0.wait()
        c1.wait()
        return c

    lax.fori_loop(0, tb, issue, 0, unroll=8)
    lax.fori_loop(0, tb, drain, 0, unroll=8)


def _dispatch_first_kernel(pos_ref, tinfo_ref, h_ref, xs_ref, zeros_ref, sem, zero_sem, *, tb, tm, max_tiles):
    @pl.when(pl.program_id(0) == 0)
    def _():
        zeros_ref[...] = jnp.zeros_like(zeros_ref)
        n_used = tinfo_ref[max_tiles]

        def fill(tile):
            rows = pl.ds(pl.multiple_of(tile * (tm * SUBLANES), tm * SUBLANES), tm * SUBLANES)
            return pltpu.make_async_copy(zeros_ref, xs_ref.at[rows, :], zero_sem)

        def each_tile(action):
            for e in range(N_EXPERTS):
                @pl.when(tinfo_ref[3 * max_tiles + e] > 0)
                def _():
                    action(fill(tinfo_ref[2 * max_tiles + e]))

            def unused(tile, c):
                action(fill(tile))
                return c

            lax.fori_loop(n_used, max_tiles, unused, 0)

        each_tile(lambda copy: copy.start())
        each_tile(lambda copy: copy.wait())

    _scatter_rows(pos_ref, h_ref, xs_ref, sem, tb, 0)


def _dispatch_more_kernel(pos_ref, h_ref, xs_in_ref, xs_ref, sem, *, tb, tok0):
    del xs_in_ref
    _scatter_rows(pos_ref, h_ref, xs_ref, sem, tb, tok0)


def _dispatch_first(pos_flat, tinfo_flat, hpk, tb, tm, max_tiles):
    n = hpk.shape[0] // SUBLANES
    grid_spec = pltpu.PrefetchScalarGridSpec(
        num_scalar_prefetch=2,
        grid=(n // tb,),
        in_specs=[pl.BlockSpec((tb * SUBLANES, LANES), lambda i, p, t: (i, 0))],
        out_specs=pl.BlockSpec(memory_space=pl.ANY),
        scratch_shapes=[pltpu.VMEM((tm * SUBLANES, LANES), F32), pltpu.SemaphoreType.DMA(()),
                        pltpu.SemaphoreType.DMA(())])
    return pl.pallas_call(
        functools.partial(_dispatch_first_kernel, tb=tb, tm=tm, max_tiles=max_tiles),
        grid_spec=grid_spec,
        out_shape=jax.ShapeDtypeStruct((max_tiles * tm * SUBLANES, LANES), F32),
        compiler_params=pltpu.CompilerParams(dimension_semantics=("arbitrary",), vmem_limit_bytes=VMEM_LIMIT),
        name="dispatch",
    )(pos_flat, tinfo_flat, hpk)


def _dispatch_more(pos_flat, hpk, xs, tb, tok0):
    n = hpk.shape[0] // SUBLANES
    grid_spec = pltpu.PrefetchScalarGridSpec(
        num_scalar_prefetch=1,
        grid=(n // tb,),
        in_specs=[pl.BlockSpec((tb * SUBLANES, LANES), lambda i, p: (i, 0)),
                  pl.BlockSpec(memory_space=pl.ANY)],
        out_specs=pl.BlockSpec(memory_space=pl.ANY),
        scratch_shapes=[pltpu.SemaphoreType.DMA(())])
    return pl.pallas_call(
        functools.partial(_dispatch_more_kernel, tb=tb, tok0=tok0),
        grid_spec=grid_spec,
        out_shape=jax.ShapeDtypeStruct(xs.shape, xs.dtype),
        input_output_aliases={2: 0},
        compiler_params=pltpu.CompilerParams(dimension_semantics=("arbitrary",), vmem_limit_bytes=VMEM_LIMIT),
        name="dispatch",
    )(pos_flat, hpk, xs)


def _store_row_tiles(ref, rows):
    n = rows.shape[0]
    for s in range(SUBLANES):
        ref[pl.ds(s, n, stride=SUBLANES), :] = rows[:, s * LANES:(s + 1) * LANES]


def _load_row_tiles(ref):
    n = ref.shape[0] // SUBLANES
    return jnp.concatenate([ref[pl.ds(s, n, stride=SUBLANES), :] for s in range(SUBLANES)], axis=1)


def _expert_kernel(te_ref, nt_ref, tinfo_ref, xs_ref, wg_ref, wu_ref, wd_ref, y_ref,
                   wg_f, wu_f, wd_f, wg_b, wu_b, wd_b, seg_ref, sems, *, max_tiles):
    i = pl.program_id(0)
    n_used = nt_ref[0]

    def fetch(expert, half):
        return (pltpu.make_async_copy(wg_ref.at[expert], wg_f.at[half], sems.at[half]),
                pltpu.make_async_copy(wu_ref.at[expert], wu_f.at[half], sems.at[half]),
                pltpu.make_async_copy(wd_ref.at[expert], wd_f.at[half], sems.at[half]))

    @pl.when(i == 0)
    def _():
        seg_ref[0] = 0
        for copy in fetch(te_ref[0], 0):
            copy.start()

    @pl.when(i < n_used)
    def _():
        expert = te_ref[i]
        changed = jnp.logical_or(i == 0, expert != te_ref[jnp.maximum(i - 1, 0)])

        @pl.when(changed)
        def _():
            half = seg_ref[0] % 2
            seg_ref[0] = seg_ref[0] + 1
            for copy in fetch(expert, half):
                copy.wait()
            following = i + tinfo_ref[3 * max_tiles + expert]

            @pl.when(following < n_used)
            def _():
                for copy in fetch(te_ref[following], 1 - half):
                    copy.start()

            wg_b[...] = wg_f[half].astype(BF)
            wu_b[...] = wu_f[half].astype(BF)
            wd_b[...] = wd_f[half].astype(BF)

        xb = _load_row_tiles(xs_ref).astype(BF)
        gate = _dot(xb, wg_b[...])
        up = _dot(xb, wu_b[...])
        _store_row_tiles(y_ref, _dot((_silu(gate) * up).astype(BF), wd_b[...]))

    @pl.when(i >= nt_ref[0])
    def _():
        y_ref[...] = jnp.zeros_like(y_ref)


def _experts(xs, tile_expert, n_tiles, tinfo_flat, wg, wu, wd, tm, max_tiles):
    any_space = pl.BlockSpec(memory_space=pl.ANY)
    grid_spec = pltpu.PrefetchScalarGridSpec(
        num_scalar_prefetch=3,
        grid=(max_tiles,),
        in_specs=[pl.BlockSpec((tm * SUBLANES, LANES),
                               lambda i, te, nt, ti: (jnp.minimum(i, nt[0] - 1), 0)),
                  any_space, any_space, any_space],
        out_specs=pl.BlockSpec((tm * SUBLANES, LANES), lambda i, te, nt, ti: (i, 0)),
        scratch_shapes=[pltpu.VMEM((2, D_MODEL, EXPERT_FF), F32), pltpu.VMEM((2, D_MODEL, EXPERT_FF), F32),
                        pltpu.VMEM((2, EXPERT_FF, D_MODEL), F32),
                        pltpu.VMEM((D_MODEL, EXPERT_FF), BF), pltpu.VMEM((D_MODEL, EXPERT_FF), BF),
                        pltpu.VMEM((EXPERT_FF, D_MODEL), BF),
                        pltpu.SMEM((1,), jnp.int32), pltpu.SemaphoreType.DMA((2,))])
    return pl.pallas_call(
        functools.partial(_expert_kernel, max_tiles=max_tiles),
        grid_spec=grid_spec,
        out_shape=jax.ShapeDtypeStruct((max_tiles * tm * SUBLANES, LANES), F32),
        compiler_params=pltpu.CompilerParams(dimension_semantics=("arbitrary",), vmem_limit_bytes=VMEM_LIMIT),
        name="experts",
    )(tile_expert, n_tiles, tinfo_flat, xs, wg, wu, wd)


def _combine_kernel(pos_ref, x1_ref, route_ref, gfin_ref, ys_ref, y_ref, buf0, buf1, sems, *, tb, tok0):
    i = pl.program_id(0)
    n_steps = pl.num_programs(0)
    slot = i % 2

    def copies(step, half, r):
        base = 2 * (tok0 + step * tb + r)
        tile = lambda ref, row: ref.at[pl.ds(pl.multiple_of(row * SUBLANES, SUBLANES), SUBLANES), :]
        return (pltpu.make_async_copy(tile(ys_ref, pos_ref[base]), tile(buf0.at[half], r), sems.at[half]),
                pltpu.make_async_copy(tile(ys_ref, pos_ref[base + 1]), tile(buf1.at[half], r), sems.at[half]))

    def request(step, half):
        def body(r, c):
            c0, c1 = copies(step, half, r)
            c0.start(priority=0)
            c1.start(priority=1)
            return c

        lax.fori_loop(0, tb, body, 0, unroll=8)

    @pl.when(i == 0)
    def _():
        request(0, 0)

    @pl.when(i + 1 < n_steps)
    def _():
        request(i + 1, 1 - slot)

    def receive(r, c):
        c0, c1 = copies(i, slot, r)
        c0.wait()
        c1.wait()
        return c

    lax.fori_loop(0, tb, receive, 0, unroll=8)
    pad = jnp.zeros((LANES - SUBLANES, LANES), F32)
    cols = jnp.concatenate([jnp.concatenate([route_ref[:, j:j + LANES], pad], axis=0).T
                            for j in range(0, tb, LANES)], axis=0)
    out = x1_ref[...] + (cols[:, 2:3] * _load_row_tiles(buf0.at[slot]) + cols[:, 3:4] * _load_row_tiles(buf1.at[slot]))
    y_ref[...] = _rms(out, gfin_ref[...])


def _combine(pos_flat, x1, route, gfin, ys, tb, tok0):
    n = x1.shape[0]
    grid_spec = pltpu.PrefetchScalarGridSpec(
        num_scalar_prefetch=1,
        grid=(n // tb,),
        in_specs=[pl.BlockSpec((tb, D_MODEL), lambda i, p: (i, 0)),
                  pl.BlockSpec((SUBLANES, tb), lambda i, p: (0, i)),
                  pl.BlockSpec((1, D_MODEL), lambda i, p: (0, 0)),
                  pl.BlockSpec(memory_space=pl.ANY)],
        out_specs=pl.BlockSpec((tb, D_MODEL), lambda i, p: (i, 0)),
        scratch_shapes=[pltpu.VMEM((2, tb * SUBLANES, LANES), F32), pltpu.VMEM((2, tb * SUBLANES, LANES), F32),
                        pltpu.SemaphoreType.DMA((2,))])
    return pl.pallas_call(
        functools.partial(_combine_kernel, tb=tb, tok0=tok0),
        grid_spec=grid_spec,
        out_shape=jax.ShapeDtypeStruct((n, D_MODEL), F32),
        compiler_params=pltpu.CompilerParams(dimension_semantics=("arbitrary",), vmem_limit_bytes=VMEM_LIMIT),
        name="combine",
    )(pos_flat, x1, route, gfin, ys)


def kernel(x_prompt, x_sample, state_conv, state_hgrn, norm_mix_g, w_in, b_gates, conv_dw_w, conv_dw_b,
           conv_ln_g, conv_ln_b, w_conv_out, hgrn_lb_param, hgrn_norm_g, w_hgrn_out, w_out, norm_ffn_g,
           w_router_group, w_router_expert, w_expert_gate, w_expert_up, w_expert_down, norm_final_g):
    batch, seq, _ = x_prompt.shape
    dec_batch = x_sample.shape[0]
    assert x_sample.shape[1] == 1 and w_in.shape[0] == 1
    tb = min(256, seq)
    tm = 256
    bs = min(8, dec_batch)
    n_p = batch * seq
    n_all = n_p + dec_batch
    tbd = min(1024, n_p)
    tbc = min(512, n_p)
    assert seq % tb == 0 and tb % CHUNK == 0 and dec_batch % bs == 0 and n_p % tbd == 0 and n_p % tbc == 0

    w_router = jnp.concatenate(
        [w_router_group[0].T, w_router_expert[0].T,
         jnp.zeros((ROUTE_ROWS - N_GROUPS - N_EXPERTS, D_MODEL), F32)], axis=0).astype(BF)
    wts = dict(gmix=norm_mix_g, w_in=w_in[0].astype(BF), b_gates=b_gates, conv_w=conv_dw_w[0], conv_b=conv_dw_b,
               ln_g=conv_ln_g, ln_b=conv_ln_b, w_conv_out=w_conv_out[0].astype(BF), lb_param=hgrn_lb_param,
               hn_g=hgrn_norm_g, w_hgrn_out=w_hgrn_out[0].astype(BF), w_out=w_out[0].astype(BF),
               gffn=norm_ffn_g, w_router=w_router)

    x1_p, hpk_p, route_p, nconv_p, nhgrn_p = _prompt_mixer(x_prompt.reshape(n_p, D_MODEL), batch, seq, tb, wts)

    xs_tok = x_sample.reshape(dec_batch, D_MODEL)
    proj_s = _sample_proj(xs_tok, wts)
    nhgrn_s, o_s = _sample_state(proj_s, state_hgrn[0], wts, bs)
    conv2 = state_conv[0].reshape(dec_batch, (CONV_WIDTH - 1) * CONV_DIM)
    x1_s, hpk_s, route_s, nconv_s = _sample_tail(xs_tok, proj_s, o_s, conv2, wts)

    max_tiles = -(-((2 * n_all) // tm + N_EXPERTS) // SUBLANES) * SUBLANES
    pos, tinfo = _plan(route_p, route_s, tm, max_tiles)
    pos_flat = pos[0:2].T.reshape(-1)
    tile_expert, n_tiles = tinfo[0], tinfo[1, 0:1]

    xs = _dispatch_first(pos_flat, tinfo.reshape(-1), hpk_p, tbd, tm, max_tiles)
    xs = _dispatch_more(pos_flat, hpk_s, xs, dec_batch, n_p)
    ys = _experts(xs, tile_expert, n_tiles, tinfo.reshape(-1), w_expert_gate[0], w_expert_up[0], w_expert_down[0],
                  tm, max_tiles)

    gfin = norm_final_g.reshape(1, D_MODEL)
    y_p = _combine(pos_flat, x1_p, route_p, gfin, ys, tbc, 0)
    y_s = _combine(pos_flat, x1_s, route_s, gfin, ys, dec_batch, n_p)

    return (y_p.reshape(batch, seq, D_MODEL), y_s.reshape(dec_batch, 1, D_MODEL),
            nconv_p[None, :, HIST_PAD:, :], nhgrn_p[None],
            nconv_s.reshape(1, dec_batch, CONV_WIDTH - 1, CONV_DIM), nhgrn_s[None])
```
